```python
import math
import jax
import jax.numpy as jnp
from jax import lax
import numpy as np

D_MODEL = 1024
BATCH = 4
SEQ = 4096
DEPTH = 4
DEC_BATCH = 32
DEC_SEQ = 1
PAST_LEN = 8192
PAGE_SIZE = 128

N_MIXERS = 2
N_LAYERS_A = (DEPTH + 1) // 2
N_LAYERS_B = DEPTH // 2
EPS = 1e-6
NEG_INF = -1e30

GDN_QK_HEADS = 8
GDN_V_HEADS = 16
GDN_DK = 128
GDN_DV = 128
GDN_CONV = 4
GDN_CHUNK = 64
GDN_QK_DIM = GDN_QK_HEADS * GDN_DK
GDN_V_DIM = GDN_V_HEADS * GDN_DV
GDN_CONV_DIM = 2 * GDN_QK_DIM + GDN_V_DIM
GDN_IN_DIM = GDN_CONV_DIM + GDN_V_DIM + 2 * GDN_V_HEADS

DSW_GROUPS = ((128, 1), (512, 4), (2048, 16))
DSW_N_GROUPS = len(DSW_GROUPS)
DSW_HEADS = 8
DSW_HD = 128
DSW_WIDTH = DSW_HEADS * DSW_HD
DSW_IN_DIM = DSW_N_GROUPS * 3 * DSW_WIDTH

D_FF = 4 * D_MODEL

kernel_name = 'gdn_dilated_swa_hybrid_step'


def rmsnorm(x, g):
    xf = x.astype(jnp.float32)
    y = xf * lax.rsqrt(jnp.mean(xf * xf, axis=-1, keepdims=True) + EPS)
    return (y * g.astype(jnp.float32)).astype(x.dtype)


def l2norm(x):
    xf = x.astype(jnp.float32)
    return xf * lax.rsqrt(jnp.sum(xf * xf, axis=-1, keepdims=True) + EPS)


def softmax_with_lse(s):
    m = jnp.max(s, axis=-1, keepdims=True)
    e = jnp.exp(s - m)
    den = jnp.sum(e, axis=-1, keepdims=True)
    return e / den, (m + jnp.log(den))[..., 0]


def sqrelu_mlp(x, w_up, w_down):
    return jnp.square(jax.nn.relu(x @ w_up)) @ w_down


def causal_conv_silu(x, buf, w):
    T = x.shape[1]
    xe = jnp.concatenate([buf.astype(x.dtype), x], axis=1)
    y = sum(xe[:, j:j + T] * w[j] for j in range(GDN_CONV))
    return jax.nn.silu(y), xe[:, T:]


def gated_delta_chunked(q, k, v, g, beta, s0):
    B, T, H, dk = k.shape
    dv = v.shape[-1]
    C = min(GDN_CHUNK, T)
    pad = (-T) % C
    nc = (T + pad) // C

    def prep(a):
        a = jnp.pad(a, [(0, 0), (0, pad)] + [(0, 0)] * (a.ndim - 2))
        a = a.reshape((B, nc, C) + a.shape[2:])
        return jnp.moveaxis(a, 3, 1)

    q, k, v, g, beta = prep(q), prep(k), prep(v), prep(g), prep(beta)
    gc = jnp.cumsum(g, axis=-1)
    causal = jnp.tril(jnp.ones((C, C), dtype=bool))
    strict = jnp.tril(jnp.ones((C, C), dtype=bool), -1)
    diff = gc[..., :, None] - gc[..., None, :]
    decay = jnp.where(causal, jnp.exp(jnp.where(causal, diff, 0.0)), 0.0)
    kb = k * beta[..., None]
    a_strict = jnp.where(strict, jnp.einsum('bhnid,bhnjd->bhnij', kb, k) * decay, 0.0)
    eye = jnp.eye(C, dtype=jnp.float32)
    t_inv = lax.linalg.triangular_solve(eye + a_strict, jnp.broadcast_to(eye, a_strict.shape),
                                        left_side=True, lower=True, unit_diagonal=True)
    u = t_inv @ (v * beta[..., None])
    w = t_inv @ (kb * jnp.exp(gc)[..., None])
    attn = jnp.einsum('bhnid,bhnjd->bhnij', q, k) * decay
    qg = q * jnp.exp(gc)[..., None]
    kd = k * jnp.exp(gc[..., -1:] - gc)[..., None]
    glast = jnp.exp(gc[..., -1])
    xs = (jnp.moveaxis(u, 2, 0), jnp.moveaxis(w, 2, 0), jnp.moveaxis(attn, 2, 0),
          jnp.moveaxis(qg, 2, 0), jnp.moveaxis(kd, 2, 0), jnp.moveaxis(glast, 2, 0))

    def step(s, inp):
        u_c, w_c, a_c, qg_c, kd_c, gl_c = inp
        v_new = u_c - jnp.einsum('bhid,bhde->bhie', w_c, s)
        o = jnp.einsum('bhid,bhde->bhie', qg_c, s) + jnp.einsum('bhij,bhje->bhie', a_c, v_new)
        s = s * gl_c[..., None, None] + jnp.einsum('bhid,bhie->bhde', kd_c, v_new)
        return s, o

    s_final, o = lax.scan(step, s0, xs)
    o = jnp.moveaxis(jnp.moveaxis(o, 0, 2), 1, 3).reshape(B, T + pad, H, dv)[:, :T]
    return o, s_final


def gdn_mixer(x, conv_buf, s0, w_in, conv_w, a_log, dt_bias, o_norm, w_out):
    B, T, _ = x.shape
    f32 = jnp.float32
    proj = x @ w_in
    o1 = GDN_CONV_DIM
    o2 = o1 + GDN_V_DIM
    o3 = o2 + GDN_V_HEADS
    qkv, z, b, a = proj[..., :o1], proj[..., o1:o2], proj[..., o2:o3], proj[..., o3:]
    qkv, new_buf = causal_conv_silu(qkv, conv_buf, conv_w)
    rep = GDN_V_HEADS // GDN_QK_HEADS
    q = qkv[..., :GDN_QK_DIM].reshape(B, T, GDN_QK_HEADS, GDN_DK)
    k = qkv[..., GDN_QK_DIM:2 * GDN_QK_DIM].reshape(B, T, GDN_QK_HEADS, GDN_DK)
    v = qkv[..., 2 * GDN_QK_DIM:].reshape(B, T, GDN_V_HEADS, GDN_DV).astype(f32)
    q = jnp.repeat(l2norm(q), rep, axis=2) * (GDN_DK ** -0.5)
    k = jnp.repeat(l2norm(k), rep, axis=2)
    beta = jax.nn.sigmoid(b.astype(f32))
    g = -jnp.exp(a_log.astype(f32)) * jax.nn.softplus(a.astype(f32) + dt_bias.astype(f32))
    o, s_new = gated_delta_chunked(q, k, v, g, beta, s0.astype(f32))
    z = z.reshape(B, T, GDN_V_HEADS, GDN_DV).astype(f32)
    o = rmsnorm(o, o_norm) * jax.nn.silu(z)
    y = o.reshape(B, T, GDN_V_DIM).astype(x.dtype) @ w_out
    return y, new_buf, s_new.astype(s0.dtype)


def dsw_project(x, w_in, q_norm, k_norm):
    B, T, _ = x.shape
    proj = (x @ w_in).reshape(B, T, DSW_N_GROUPS, 3, DSW_HEADS, DSW_HD)
    q = rmsnorm(proj[:, :, :, 0], q_norm[:, None, :])
    k = rmsnorm(proj[:, :, :, 1], k_norm[:, None, :])
    v = proj[:, :, :, 2]
    return q, k, v


def dilated_band_attn(q, k, v, window, dil):
    B, T, H, hd = q.shape
    blk = window // dil
    P = (-T) % (dil * blk)
    Tp = T + P
    nb = Tp // (dil * blk)

    def blocks(a):
        a = jnp.pad(a, [(0, 0), (P, 0), (0, 0), (0, 0)])
        return a.reshape(B, nb, blk, dil, H, hd)

    def with_prev(a):
        prev = jnp.concatenate([jnp.zeros_like(a[:, :1]), a[:, :-1]], axis=1)
        return jnp.concatenate([prev, a], axis=2)

    qb = blocks(q)
    kk = with_prev(blocks(k))
    vv = with_prev(blocks(v))
    s = jnp.einsum('bnqrhd,bnkrhd->bnrhqk', qb, kk, preferred_element_type=jnp.float32) * (hd ** -0.5)
    n = jnp.arange(nb)[:, None, None, None]
    r = jnp.arange(dil)[None, :, None, None]
    qi = jnp.arange(blk)[None, None, :, None]
    ki = jnp.arange(2 * blk)[None, None, None, :]
    dist = blk + qi - ki
    key_pos = ((n - 1) * blk + ki) * dil + r
    valid = (dist >= 0) & (dist <= blk) & (key_pos >= P)
    s = jnp.where(valid[:, :, None], s, NEG_INF)
    p, lse = softmax_with_lse(s)
    o = jnp.einsum('bnrhqk,bnkrhd->bnqrhd', p.astype(vv.dtype), vv)
    o = o.reshape(B, Tp, H, hd)[:, P:]
    lse = jnp.transpose(lse, (0, 1, 4, 2, 3)).reshape(B, Tp, H)[:, P:]
    return o, lse


def dilated_gather_attn(q, k_ext, v_ext, window, dil):
    B, T, H, hd = q.shape
    L = k_ext.shape[1]
    nk = window // dil
    qpos = (L - T) + jnp.arange(T)
    idx = qpos[:, None] - dil * jnp.arange(nk + 1)[None, :]
    valid = idx >= 0
    idx = jnp.maximum(idx, 0)
    kg = jnp.take(k_ext, idx, axis=1)
    vg = jnp.take(v_ext, idx, axis=1)
    s = jnp.einsum('bthd,btjhd->bthj', q, kg, preferred_element_type=jnp.float32) * (hd ** -0.5)
    s = jnp.where(valid[:, None, :], s, NEG_INF)
    p, lse = softmax_with_lse(s)
    o = jnp.einsum('bthj,btjhd->bthd', p.astype(vg.dtype), vg)
    return o, lse


def merge_groups(outs, lses, dtype):
    wts = jax.nn.softmax(jnp.stack(lses, axis=0), axis=0)
    o = jnp.einsum('gbth,gbthd->bthd', wts, jnp.stack(outs, axis=0).astype(jnp.float32))
    B, T = o.shape[:2]
    return o.reshape(B, T, DSW_WIDTH).astype(dtype)


def dsw_prompt(x, w_in, q_norm, k_norm, w_out):
    T = x.shape[1]
    q, k, v = dsw_project(x, w_in, q_norm, k_norm)
    outs, lses, bufs = [], [], []
    for gi, (win, dil) in enumerate(DSW_GROUPS):
        o, l = dilated_band_attn(q[:, :, gi], k[:, :, gi], v[:, :, gi], win, dil)
        outs.append(o)
        lses.append(l)
        keep = min(win, T)
        bufs.append(jnp.stack([k[:, T - keep:, gi], v[:, T - keep:, gi]], axis=2))
    return merge_groups(outs, lses, x.dtype) @ w_out, bufs


def dsw_sample(x, past_bufs, w_in, q_norm, k_norm, w_out):
    T = x.shape[1]
    q, k, v = dsw_project(x, w_in, q_norm, k_norm)
    outs, lses, bufs = [], [], []
    for gi, (win, dil) in enumerate(DSW_GROUPS):
        buf = past_bufs[gi].astype(k.dtype)
        k_ext = jnp.concatenate([buf[:, :, 0], k[:, :, gi]], axis=1)
        v_ext = jnp.concatenate([buf[:, :, 1], v[:, :, gi]], axis=1)
        o, l = dilated_gather_attn(q[:, :, gi], k_ext, v_ext, win, dil)
        outs.append(o)
        lses.append(l)
        keep = min(win, buf.shape[1] + T)
        new_rows = jnp.stack([k[:, :, gi], v[:, :, gi]], axis=2)
        bufs.append(jnp.concatenate([buf, new_rows], axis=1)[:, -keep:])
    return merge_groups(outs, lses, x.dtype) @ w_out, bufs


def setup_inputs(seed: int = 0) -> dict:
    key = jax.random.key(seed)
    ks = jax.random.split(key, 21)
    f32 = jnp.float32

    def nrm(k, shape, scale):
        return jax.random.normal(k, shape, f32) * scale

    def gain(k, shape):
        return 1.0 + 0.02 * jax.random.normal(k, shape, f32)

    lens = [min(w, PAST_LEN) for (w, _) in DSW_GROUPS]
    dt = jnp.exp(jax.random.uniform(ks[12], (N_LAYERS_A, GDN_V_HEADS), f32,
                                    math.log(1e-3), math.log(1e-1)))
    return {
        'x_prompt': nrm(ks[0], (BATCH, SEQ, D_MODEL), 1.0),
        'x_sample': nrm(ks[1], (DEC_BATCH, DEC_SEQ, D_MODEL), 1.0),
        'state_gdn': nrm(ks[2], (N_LAYERS_A, DEC_BATCH, GDN_V_HEADS, GDN_DK, GDN_DV), 0.1),
        'state_conv': nrm(ks[3], (N_LAYERS_A, DEC_BATCH, GDN_CONV - 1, GDN_CONV_DIM), 1.0),
        'cache_kv_w128': nrm(ks[4], (N_LAYERS_B, DEC_BATCH, lens[0], 2, DSW_HEADS, DSW_HD), 1.0),
        'cache_kv_w512': nrm(ks[5], (N_LAYERS_B, DEC_BATCH, lens[1], 2, DSW_HEADS, DSW_HD), 1.0),
        'cache_kv_w2048': nrm(ks[6], (N_LAYERS_B, DEC_BATCH, lens[2], 2, DSW_HEADS, DSW_HD), 1.0),
        'norm_mix': gain(ks[7], (DEPTH, D_MODEL)),
        'norm_mlp': gain(ks[8], (DEPTH, D_MODEL)),
        'gdn_w_in': nrm(ks[9], (N_LAYERS_A, D_MODEL, GDN_IN_DIM), D_MODEL ** -0.5),
        'gdn_conv_w': nrm(ks[10], (N_LAYERS_A, GDN_CONV, GDN_CONV_DIM), GDN_CONV ** -0.5),
        'gdn_a_log': jnp.log(jax.random.uniform(ks[11], (N_LAYERS_A, GDN_V_HEADS), f32, 1.0, 16.0)),
        'gdn_dt_bias': dt + jnp.log(-jnp.expm1(-dt)),
        'gdn_o_norm': gain(ks[13], (N_LAYERS_A, GDN_DV)),
        'gdn_w_out': nrm(ks[14], (N_LAYERS_A, GDN_V_DIM, D_MODEL), GDN_V_DIM ** -0.5),
        'dsw_w_in': nrm(ks[15], (N_LAYERS_B, D_MODEL, DSW_IN_DIM), D_MODEL ** -0.5),
        'dsw_q_norm': gain(ks[16], (N_LAYERS_B, DSW_N_GROUPS, DSW_HD)),
        'dsw_k_norm': gain(ks[17], (N_LAYERS_B, DSW_N_GROUPS, DSW_HD)),
        'dsw_w_out': nrm(ks[18], (N_LAYERS_B, DSW_WIDTH, D_MODEL), DSW_WIDTH ** -0.5),
        'mlp_w_up': nrm(ks[19], (DEPTH, D_MODEL, D_FF), D_MODEL ** -0.5),
        'mlp_w_down': nrm(ks[20], (DEPTH, D_FF, D_MODEL), D_FF ** -0.5),
    }


def reference(x_prompt, x_sample, state_gdn, state_conv, cache_kv_w128, cache_kv_w512, cache_kv_w2048,
              norm_mix, norm_mlp, gdn_w_in, gdn_conv_w, gdn_a_log, gdn_dt_bias, gdn_o_norm, gdn_w_out,
              dsw_w_in, dsw_q_norm, dsw_k_norm, dsw_w_out, mlp_w_up, mlp_w_down):
    B = x_prompt.shape[0]
    yp, ys = x_prompt, x_sample
    p_gdn, p_conv, s_gdn, s_conv = [], [], [], []
    p_kv = [[] for _ in DSW_GROUPS]
    s_kv = [[] for _ in DSW_GROUPS]
    for i in range(DEPTH):
        j = i // N_MIXERS
        hp = rmsnorm(yp, norm_mix[i])
        hs = rmsnorm(ys, norm_mix[i])
        if i % N_MIXERS == 0:
            wts = (gdn_w_in[j], gdn_conv_w[j], gdn_a_log[j], gdn_dt_bias[j], gdn_o_norm[j], gdn_w_out[j])
            conv0 = jnp.zeros((B, GDN_CONV - 1, GDN_CONV_DIM), hp.dtype)
            s0 = jnp.zeros((B, GDN_V_HEADS, GDN_DK, GDN_DV), state_gdn.dtype)
            op, cbp, sp = gdn_mixer(hp, conv0, s0, *wts)
            osm, cbs, ss = gdn_mixer(hs, state_conv[j], state_gdn[j], *wts)
            p_gdn.append(sp)
            p_conv.append(cbp)
            s_gdn.append(ss)
            s_conv.append(cbs)
        else:
            wts = (dsw_w_in[j], dsw_q_norm[j], dsw_k_norm[j], dsw_w_out[j])
            op, bufs_p = dsw_prompt(hp, *wts)
            osm, bufs_s = dsw_sample(hs, (cache_kv_w128[j], cache_kv_w512[j], cache_kv_w2048[j]), *wts)
            for gi in range(DSW_N_GROUPS):
                p_kv[gi].append(bufs_p[gi])
                s_kv[gi].append(bufs_s[gi])
        yp = yp + op
        ys = ys + osm
        yp = yp + sqrelu_mlp(rmsnorm(yp, norm_mlp[i]), mlp_w_up[i], mlp_w_down[i])
        ys = ys + sqrelu_mlp(rmsnorm(ys, norm_mlp[i]), mlp_w_up[i], mlp_w_down[i])
    return (yp, ys,
            jnp.stack(p_gdn), jnp.stack(p_conv),
            jnp.stack(p_kv[0]), jnp.stack(p_kv[1]), jnp.stack(p_kv[2]),
            jnp.stack(s_gdn), jnp.stack(s_conv),
            jnp.stack(s_kv[0]), jnp.stack(s_kv[1]), jnp.stack(s_kv[2]))
```

```python
import functools

import jax
import jax.numpy as jnp
from jax import lax
from jax.experimental import pallas as pl
from jax.experimental.pallas import tpu as pltpu

F32 = jnp.float32
BF16 = jnp.bfloat16

EPS = 1e-6
NEG_INF = -1e30
LANES = 128
CONV_TAPS = 4
CHUNK = 64
HEAD_DIM = 128
DSW_BLOCK = 128
DSW_GROUPS = ((128, 1), (512, 4), (2048, 16))
VMEM_LIMIT = 56 * 1024 * 1024


def _params(n_axes, vmem=VMEM_LIMIT):
    return pltpu.CompilerParams(dimension_semantics=("arbitrary",) * n_axes,
                                vmem_limit_bytes=vmem)


def _silu(x):
    return x * (1.0 / (1.0 + jnp.exp(-x)))


def _softplus(x):
    return jnp.maximum(x, 0.0) + jnp.log(1.0 + jnp.exp(-jnp.abs(x)))


def _rms_rows(x, gain_row):
    ms = jnp.mean(x * x, axis=-1, keepdims=True)
    return x * lax.rsqrt(ms + EPS) * gain_row


def _norm_matmul_kernel(x_ref, g_ref, w_ref, o_ref, xn_ref):
    @pl.when(pl.program_id(1) == 0)
    def _():
        xn_ref[...] = _rms_rows(x_ref[...].astype(F32), g_ref[...]).astype(BF16)

    o_ref[...] = jnp.dot(xn_ref[...], w_ref[...], preferred_element_type=F32).astype(o_ref.dtype)


def norm_matmul(x, gain, w, out_dtype, tm, tn):
    m, k = x.shape
    n = w.shape[1]
    tm = min(tm, m)
    tn = min(tn, n)
    return pl.pallas_call(
        _norm_matmul_kernel,
        grid=(m // tm, n // tn),
        in_specs=[pl.BlockSpec((tm, k), lambda i, j: (i, 0)),
                  pl.BlockSpec((1, k), lambda i, j: (0, 0)),
                  pl.BlockSpec((k, tn), lambda i, j: (0, j))],
        out_specs=pl.BlockSpec((tm, tn), lambda i, j: (i, j)),
        out_shape=jax.ShapeDtypeStruct((m, n), out_dtype),
        scratch_shapes=[pltpu.VMEM((tm, k), BF16)],
        compiler_params=_params(2),
        name="norm_matmul",
    )(x, gain.reshape(1, k), w)


def _matmul_residual_kernel(x_ref, a_ref, w_ref, o_ref):
    o_ref[...] = x_ref[...] + jnp.dot(a_ref[...], w_ref[...], preferred_element_type=F32)


def matmul_residual(x, a, w, tm):
    m, d = x.shape
    k = a.shape[1]
    tm = min(tm, m)
    return pl.pallas_call(
        _matmul_residual_kernel,
        grid=(m // tm,),
        in_specs=[pl.BlockSpec((tm, d), lambda i: (i, 0)),
                  pl.BlockSpec((tm, k), lambda i: (i, 0)),
                  pl.BlockSpec((k, d), lambda i: (0, 0))],
        out_specs=pl.BlockSpec((tm, d), lambda i: (i, 0)),
        out_shape=jax.ShapeDtypeStruct((m, d), F32),
        compiler_params=_params(1),
        name="matmul_residual",
    )(x, a, w)


def _mlp_kernel(x_ref, g_ref, wu_ref, wd_ref, o_ref, xn_ref, acc_ref):
    f = pl.program_id(1)

    @pl.when(f == 0)
    def _():
        xn_ref[...] = _rms_rows(x_ref[...], g_ref[...]).astype(BF16)
        acc_ref[...] = jnp.zeros_like(acc_ref)

    h = jnp.dot(xn_ref[...], wu_ref[...], preferred_element_type=F32)
    h = jnp.square(jnp.maximum(h, 0.0)).astype(BF16)
    acc_ref[...] += jnp.dot(h, wd_ref[...], preferred_element_type=F32)

    @pl.when(f == pl.num_programs(1) - 1)
    def _():
        o_ref[...] = x_ref[...] + acc_ref[...]


def mlp(x, gain, w_up, w_down, tm, tf):
    m, d = x.shape
    ff = w_up.shape[1]
    tm = min(tm, m)
    return pl.pallas_call(
        _mlp_kernel,
        grid=(m // tm, ff // tf),
        in_specs=[pl.BlockSpec((tm, d), lambda i, f: (i, 0)),
                  pl.BlockSpec((1, d), lambda i, f: (0, 0)),
                  pl.BlockSpec((d, tf), lambda i, f: (0, f)),
                  pl.BlockSpec((tf, d), lambda i, f: (f, 0))],
        out_specs=pl.BlockSpec((tm, d), lambda i, f: (i, 0)),
        out_shape=jax.ShapeDtypeStruct((m, d), F32),
        scratch_shapes=[pltpu.VMEM((tm, d), BF16), pltpu.VMEM((tm, d), F32)],
        compiler_params=_params(2),
        name="mlp",
    )(x, gain.reshape(1, d), w_up, w_down)


def _dsw_inproj_kernel(*refs, n_res):
    x_refs = refs[:n_res]
    g_ref, hn_ref, w_ref, o_ref, xn_ref = refs[n_res:]
    rows = x_refs[0].shape[0]
    width = w_ref.shape[2]
    for rr in range(n_res):
        xn_ref[rr * rows:(rr + 1) * rows, :] = _rms_rows(x_refs[rr][...], g_ref[...]).astype(BF16)
    for t in range(3):
        acc = jnp.dot(xn_ref[...], w_ref[t], preferred_element_type=F32)
        for h in range(width // HEAD_DIM):
            a = acc[:, h * HEAD_DIM:(h + 1) * HEAD_DIM]
            if t < 2:
                a = _rms_rows(a, hn_ref[t:t + 1, :])
            a = a.astype(o_ref.dtype)
            for rr in range(n_res):
                o_ref[rr, :, t * width + h * HEAD_DIM:t * width + (h + 1) * HEAD_DIM] = (
                    a[rr * rows:(rr + 1) * rows, :])


def dsw_inproj(x, gain, head_gains, w3, dil, tile_rows):
    b, t, d = x.shape
    width = w3.shape[2]
    sub = t // dil
    rows = min(tile_rows, sub)
    n_res = min(max(tile_rows // rows, 1), dil)
    xv = x.reshape(b, sub, dil * d)
    x_specs = [pl.BlockSpec((None, rows, d),
                            functools.partial(lambda bi, rb, i, rr: (bi, i, rb * n_res + rr), rr=rr))
               for rr in range(n_res)]
    return pl.pallas_call(
        functools.partial(_dsw_inproj_kernel, n_res=n_res),
        grid=(b, dil // n_res, sub // rows),
        in_specs=x_specs + [pl.BlockSpec((1, d), lambda bi, rb, i: (0, 0)),
                            pl.BlockSpec((2, HEAD_DIM), lambda bi, rb, i: (0, 0)),
                            pl.BlockSpec((3, d, width), lambda bi, rb, i: (0, 0, 0))],
        out_specs=pl.BlockSpec((None, n_res, rows, 3 * width), lambda bi, rb, i: (bi, rb, i, 0)),
        out_shape=jax.ShapeDtypeStruct((b, dil, sub, 3 * width), BF16),
        scratch_shapes=[pltpu.VMEM((n_res * rows, d), BF16)],
        compiler_params=_params(3),
        name="dsw_inproj",
    )(*([xv] * n_res), gain.reshape(1, d), head_gains, w3)


TRI_BASE = 8


def _tri_inverse(a):
    n = a.shape[0]
    ii = lax.broadcasted_iota(jnp.int32, (n, n), 0)
    jj = lax.broadcasted_iota(jnp.int32, (n, n), 1)
    same_block = lambda size: (ii // size) == (jj // size)
    mm = lambda u, w: jnp.dot(u.astype(BF16), w.astype(BF16), preferred_element_type=F32)
    p = jnp.where(same_block(TRI_BASE), a, 0.0)
    x = jnp.where(ii == jj, 1.0, 0.0) - p
    for _ in range((TRI_BASE - 1).bit_length() - 1):
        p = mm(p, p)
        x = x + mm(x, p)
    size = TRI_BASE
    while size < n:
        a_off = jnp.where(same_block(2 * size) & jnp.logical_not(same_block(size)), a, 0.0)
        x = x - mm(mm(x, a_off), x)
        size *= 2
    return x


def _gdn_prompt_kernel(qkv_ref, z_ref, gate_ref, cw_ref, alog_ref, dtb_ref, onorm_ref,
                       o_ref, s_out_ref,
                       xe_ref, qk_ref, v_ref, beta_ref, gc_ref, gct_ref, s_ref, oscr_ref,
                       *, n_qk, n_v):
    tb = qkv_ref.shape[0]
    nchunk = tb // CHUNK
    rep = n_v // n_qk
    t_idx = pl.program_id(1)
    halo = 8

    @pl.when(t_idx == 0)
    def _():
        xe_ref[:, halo - (CONV_TAPS - 1):halo, :] = jnp.zeros((xe_ref.shape[0], CONV_TAPS - 1, LANES), F32)
        s_ref[...] = jnp.zeros_like(s_ref)

    beta_ref[...] = 1.0 / (1.0 + jnp.exp(-gate_ref[:, :LANES]))
    g = -jnp.exp(alog_ref[...]) * _softplus(gate_ref[:, LANES:] + dtb_ref[...])
    pos = lax.broadcasted_iota(jnp.int32, (tb, LANES), 0) % CHUNK
    shift = 1
    while shift < CHUNK:
        g = g + jnp.where(pos >= shift, pltpu.roll(g, shift, axis=0), 0.0)
        shift *= 2
    gc_ref[...] = g
    gt = g.T
    for c in range(nchunk):
        gct_ref[c] = gt[:, c * CHUNK:(c + 1) * CHUNK]

    for cb in range(n_qk * 2 + n_v):
        lo = cb * LANES
        xe_ref[cb, halo:halo + tb, :] = qkv_ref[:, lo:lo + LANES].astype(F32)
        y = cw_ref[0:1, lo:lo + LANES] * xe_ref[cb, halo - 3:halo - 3 + tb, :]
        for j in range(1, CONV_TAPS):
            y = y + cw_ref[j:j + 1, lo:lo + LANES] * xe_ref[cb, halo - 3 + j:halo - 3 + j + tb, :]
        y = _silu(y)
        xe_ref[cb, halo - 3:halo, :] = xe_ref[cb, halo + tb - 3:halo + tb, :]
        if cb < 2 * n_qk:
            y = y * lax.rsqrt(jnp.sum(y * y, axis=-1, keepdims=True) + EPS)
            if cb < n_qk:
                y = y * (HEAD_DIM ** -0.5)
            yb = y.astype(BF16)
            half = 0 if cb < n_qk else CHUNK
            for c in range(nchunk):
                qk_ref[cb % n_qk, c, half:half + CHUNK, :] = yb[c * CHUNK:(c + 1) * CHUNK, :]
        else:
            v_ref[cb - 2 * n_qk] = y

    ii = lax.broadcasted_iota(jnp.int32, (CHUNK, CHUNK), 0)
    jj = lax.broadcasted_iota(jnp.int32, (CHUNK, CHUNK), 1)
    causal = ii >= jj
    strict = ii > jj
    lane = lax.broadcasted_iota(jnp.int32, (CHUNK, LANES), 1)

    def group_body(j, carry):
        states = [s_ref[rep * j + e] for e in range(rep)]
        for c in range(nchunk):
            r0 = c * CHUNK
            qk = qk_ref[j, c]
            k = qk[CHUNK:, :]
            gram = lax.dot_general(qk, k, (((1,), (1,)), ((), ())), preferred_element_type=F32)
            q_kt, k_kt = gram[:CHUNK], gram[CHUNK:]
            s_cat = jnp.concatenate([s.astype(BF16) for s in states], axis=1)
            proj = jnp.dot(qk, s_cat, preferred_element_type=F32)
            for e in range(rep):
                h = rep * j + e
                sel = lane == h
                beta_c = jnp.sum(jnp.where(sel, beta_ref[r0:r0 + CHUNK, :], 0.0), axis=1, keepdims=True)
                gc_c = jnp.sum(jnp.where(sel, gc_ref[r0:r0 + CHUNK, :], 0.0), axis=1, keepdims=True)
                gc_r = gct_ref[c, pl.ds(h, 1), :]
                g_last = gc_r[:, CHUNK - 1:CHUNK]
                decay = jnp.where(causal, jnp.exp(jnp.where(causal, gc_c - gc_r, 0.0)), 0.0)
                t_inv = _tri_inverse(jnp.where(strict, beta_c * k_kt * decay, 0.0))
                e_gc = jnp.exp(gc_c)
                q_s = proj[:CHUNK, e * HEAD_DIM:(e + 1) * HEAD_DIM]
                k_s = proj[CHUNK:, e * HEAD_DIM:(e + 1) * HEAD_DIM]
                rhs = beta_c * (v_ref[h, r0:r0 + CHUNK, :] - e_gc * k_s)
                v_new = jnp.dot(t_inv.astype(BF16), rhs.astype(BF16), preferred_element_type=F32)
                v_new_b = v_new.astype(BF16)
                oscr_ref[h, r0:r0 + CHUNK, :] = e_gc * q_s + jnp.dot(
                    (q_kt * decay).astype(BF16), v_new_b, preferred_element_type=F32)
                v_dec = (jnp.exp(g_last - gc_c) * v_new).astype(BF16)
                states[e] = jnp.exp(g_last) * states[e] + lax.dot_general(
                    k, v_dec, (((0,), (0,)), ((), ())), preferred_element_type=F32)
        for e in range(rep):
            s_ref[rep * j + e] = states[e]
        return carry

    lax.fori_loop(0, n_qk, group_body, 0)

    for h in range(n_v):
        lo = h * HEAD_DIM
        o = _rms_rows(oscr_ref[h], onorm_ref[...])
        o_ref[:, lo:lo + HEAD_DIM] = (o * _silu(z_ref[:, lo:lo + HEAD_DIM].astype(F32))).astype(o_ref.dtype)

    @pl.when(t_idx == pl.num_programs(1) - 1)
    def _():
        s_out_ref[...] = s_ref[...]


def gdn_prompt(qkvz, gates, conv_w, a_log, dt_bias, o_norm, n_qk, n_v, tb):
    b, t, _ = qkvz.shape
    conv_dim = (2 * n_qk + n_v) * HEAD_DIM
    v_dim = n_v * HEAD_DIM
    tb = min(tb, t)
    nchunk = tb // CHUNK
    pad = lambda p: jnp.zeros((1, LANES), F32).at[0, :n_v].set(p.astype(F32))
    return pl.pallas_call(
        functools.partial(_gdn_prompt_kernel, n_qk=n_qk, n_v=n_v),
        grid=(b, t // tb),
        in_specs=[pl.BlockSpec((None, tb, conv_dim), lambda bi, ti: (bi, ti, 0)),
                  pl.BlockSpec((None, tb, v_dim), lambda bi, ti: (bi, ti, conv_dim // v_dim)),
                  pl.BlockSpec((None, tb, 2 * LANES), lambda bi, ti: (bi, ti, 0)),
                  pl.BlockSpec((CONV_TAPS, conv_dim), lambda bi, ti: (0, 0)),
                  pl.BlockSpec((1, LANES), lambda bi, ti: (0, 0)),
                  pl.BlockSpec((1, LANES), lambda bi, ti: (0, 0)),
                  pl.BlockSpec((1, HEAD_DIM), lambda bi, ti: (0, 0))],
        out_specs=[pl.BlockSpec((None, tb, v_dim), lambda bi, ti: (bi, ti, 0)),
                   pl.BlockSpec((None, n_v, HEAD_DIM, HEAD_DIM), lambda bi, ti: (bi, 0, 0, 0))],
        out_shape=[jax.ShapeDtypeStruct((b, t, v_dim), BF16),
                   jax.ShapeDtypeStruct((b, n_v, HEAD_DIM, HEAD_DIM), F32)],
        scratch_shapes=[pltpu.VMEM((2 * n_qk + n_v, tb + 8, LANES), F32),
                        pltpu.VMEM((n_qk, nchunk, 2 * CHUNK, HEAD_DIM), BF16),
                        pltpu.VMEM((n_v, tb, HEAD_DIM), F32),
                        pltpu.VMEM((tb, LANES), F32),
                        pltpu.VMEM((tb, LANES), F32),
                        pltpu.VMEM((nchunk, LANES, CHUNK), F32),
                        pltpu.VMEM((n_v, HEAD_DIM, HEAD_DIM), F32),
                        pltpu.VMEM((n_v, tb, HEAD_DIM), F32)],
        compiler_params=_params(2),
        name="gdn_prompt",
    )(qkvz, qkvz, gates, conv_w, pad(a_log), pad(dt_bias), o_norm.reshape(1, HEAD_DIM))


GDN_QK_HEADS = 8
GDN_V_HEADS = 16
GDN_CONV_DIM = (2 * GDN_QK_HEADS + GDN_V_HEADS) * HEAD_DIM
GDN_V_DIM = GDN_V_HEADS * HEAD_DIM


def prep_gdn_weights(w_in, w_out):
    d = w_in.shape[0]
    main = GDN_CONV_DIM + GDN_V_DIM
    w_main = w_in[:, :main].astype(BF16)
    w_gate = jnp.zeros((d, 2 * LANES), F32)
    w_gate = w_gate.at[:, :GDN_V_HEADS].set(w_in[:, main:main + GDN_V_HEADS])
    w_gate = w_gate.at[:, LANES:LANES + GDN_V_HEADS].set(w_in[:, main + GDN_V_HEADS:])
    return w_main, w_gate.astype(BF16), w_out.astype(BF16)


def gdn_layer_prompt(x, gain, weights, conv_w, a_log, dt_bias, o_norm):
    w_main, w_gate, w_out = weights
    b, t, d = x.shape
    x2 = x.reshape(b * t, d)
    qkvz = norm_matmul(x2, gain, w_main, BF16, 1024, 512).reshape(b, t, -1)
    gates = norm_matmul(x2, gain, w_gate, F32, 1024, 2 * LANES).reshape(b, t, -1)
    o, state = gdn_prompt(qkvz, gates, conv_w, a_log, dt_bias, o_norm, GDN_QK_HEADS, GDN_V_HEADS, 256)
    y = matmul_residual(x2, o.reshape(b * t, -1), w_out, 1024).reshape(b, t, d)
    conv_tail = qkvz[:, t - (CONV_TAPS - 1):, :GDN_CONV_DIM].astype(F32)
    return y, conv_tail, state


def _gdn_step_kernel(qkvz_ref, gate_ref, conv_ref, s_ref, cw_ref, alog_ref, dtb_ref, onorm_ref,
                     o_ref, convn_ref, sn_ref, *, n_qk, n_v):
    rep = n_v // n_qk
    conv_dim = (2 * n_qk + n_v) * HEAD_DIM
    x = qkvz_ref[:, :conv_dim]
    y = cw_ref[CONV_TAPS - 1:CONV_TAPS, :] * x
    for j in range(CONV_TAPS - 1):
        y = y + cw_ref[j:j + 1, :] * conv_ref[j:j + 1, :]
    y = _silu(y)
    convn_ref[0:CONV_TAPS - 2, :] = conv_ref[1:CONV_TAPS - 1, :]
    convn_ref[CONV_TAPS - 2:CONV_TAPS - 1, :] = x

    beta = 1.0 / (1.0 + jnp.exp(-gate_ref[:, :LANES]))
    g = -jnp.exp(alog_ref[...]) * _softplus(gate_ref[:, LANES:] + dtb_ref[...])
    eye = (lax.broadcasted_iota(jnp.int32, (HEAD_DIM, HEAD_DIM), 0)
           == lax.broadcasted_iota(jnp.int32, (HEAD_DIM, HEAD_DIM), 1))

    def column(row):
        return jnp.sum(jnp.where(eye, row, 0.0), axis=1, keepdims=True)

    def l2(row):
        return row * lax.rsqrt(jnp.sum(row * row, axis=-1, keepdims=True) + EPS)

    for j in range(n_qk):
        q_col = column(l2(y[:, j * HEAD_DIM:(j + 1) * HEAD_DIM]) * (HEAD_DIM ** -0.5))
        k_col = column(l2(y[:, (n_qk + j) * HEAD_DIM:(n_qk + j + 1) * HEAD_DIM]))
        for e in range(rep):
            h = rep * j + e
            lo = h * HEAD_DIM
            v = y[:, 2 * n_qk * HEAD_DIM + lo:2 * n_qk * HEAD_DIM + lo + HEAD_DIM]
            s = s_ref[h]
            e_g = jnp.exp(g[:, h:h + 1])
            k_s = jnp.sum(s * k_col, axis=0, keepdims=True)
            v_new = beta[:, h:h + 1] * (v - e_g * k_s)
            s_new = e_g * s + k_col * v_new
            sn_ref[h] = s_new
            o = _rms_rows(jnp.sum(s_new * q_col, axis=0, keepdims=True), onorm_ref[...])
            z = qkvz_ref[:, conv_dim + lo:conv_dim + lo + HEAD_DIM]
            o_ref[:, lo:lo + HEAD_DIM] = (o * _silu(z)).astype(o_ref.dtype)


def gdn_step(qkvz, gates, conv_state, state, conv_w, a_log, dt_bias, o_norm, n_qk, n_v):
    b = qkvz.shape[0]
    conv_dim = (2 * n_qk + n_v) * HEAD_DIM
    v_dim = n_v * HEAD_DIM
    pad = lambda p: jnp.zeros((1, LANES), F32).at[0, :n_v].set(p.astype(F32))
    row = lambda n: pl.BlockSpec((None, 1, n), lambda bi: (bi, 0, 0))
    const = lambda shape: pl.BlockSpec(shape, lambda bi: (0,) * len(shape))
    st = pl.BlockSpec((None, n_v, HEAD_DIM, HEAD_DIM), lambda bi: (bi, 0, 0, 0))
    cv = pl.BlockSpec((None, CONV_TAPS - 1, conv_dim), lambda bi: (bi, 0, 0))
    return pl.pallas_call(
        functools.partial(_gdn_step_kernel, n_qk=n_qk, n_v=n_v),
        grid=(b,),
        in_specs=[row(conv_dim + v_dim), row(2 * LANES), cv, st,
                  const((CONV_TAPS, conv_dim)), const((1, LANES)), const((1, LANES)), const((1, HEAD_DIM))],
        out_specs=[row(v_dim), cv, st],
        out_shape=[jax.ShapeDtypeStruct((b, 1, v_dim), BF16),
                   jax.ShapeDtypeStruct(conv_state.shape, F32),
                   jax.ShapeDtypeStruct(state.shape, F32)],
        compiler_params=_params(1),
        name="gdn_step",
    )(qkvz.reshape(b, 1, -1), gates.reshape(b, 1, -1), conv_state, state,
      conv_w, pad(a_log), pad(dt_bias), o_norm.reshape(1, HEAD_DIM))


def gdn_layer_sample(x, gain, weights, conv_w, a_log, dt_bias, o_norm, conv_state, state):
    w_main, w_gate, w_out = weights
    b = x.shape[0]
    qkvz = norm_matmul(x, gain, w_main, F32, b, 512)
    gates = norm_matmul(x, gain, w_gate, F32, b, 2 * LANES)
    o, conv_new, state_new = gdn_step(qkvz, gates, conv_state, state, conv_w, a_log, dt_bias, o_norm,
                                      GDN_QK_HEADS, GDN_V_HEADS)
    return matmul_residual(x, o.reshape(b, -1), w_out, b), conv_new, state_new


DSW_HEADS = 8
DSW_WIDTH = DSW_HEADS * HEAD_DIM
DSW_TILE = DSW_BLOCK * max(d for _, d in DSW_GROUPS)


def _dsw_attn_kernel(*refs):
    n_groups = len(DSW_GROUPS)
    ins = [refs[5 * g:5 * g + 5] for g in range(n_groups)]
    o_ref = refs[5 * n_groups]
    scratch = refs[5 * n_groups + 1:]
    kf_refs, vf_refs = scratch[:n_groups], scratch[n_groups:2 * n_groups]
    og_ref, lg_ref = scratch[2 * n_groups:]
    first_tile = pl.program_id(2) == 0
    blk = DSW_BLOCK
    qi = lax.broadcasted_iota(jnp.int32, (blk, 2 * blk), 0)
    ki = lax.broadcasted_iota(jnp.int32, (blk, 2 * blk), 1)
    band = (ki >= qi) & (ki <= qi + blk)
    scale = HEAD_DIM ** -0.5

    for g, (_, dil) in enumerate(DSW_GROUPS):
        q_ref, kc_ref, vc_ref, kp_ref, vp_ref = ins[g]
        kf_ref, vf_ref = kf_refs[g], vf_refs[g]
        per_res = q_ref.shape[1] // blk
        kf_ref[:, :blk, :] = kp_ref[...]
        kf_ref[:, blk:, :] = kc_ref[...]
        vf_ref[:, :blk, :] = vp_ref[...]
        vf_ref[:, blk:, :] = vc_ref[...]

        def block_body(bi, carry, *, g=g, dil=dil, per_res=per_res, q_ref=q_ref, kf_ref=kf_ref, vf_ref=vf_ref):
            r = bi // per_res
            m = bi % per_res
            row0 = pl.multiple_of(m * blk, blk)
            q = q_ref[r, pl.ds(row0, blk), :]
            k2 = kf_ref[r, pl.ds(row0, 2 * blk), :]
            v2 = vf_ref[r, pl.ds(row0, 2 * blk), :]
            s = lax.dot_general(q, k2, (((1,), (1,)), ((), ())), preferred_element_type=F32) * scale
            valid = band & ((ki >= blk) | (m > 0) | jnp.logical_not(first_tile))
            s = jnp.where(valid, s, NEG_INF)
            mx = jnp.max(s, axis=-1, keepdims=True)
            p = jnp.exp(s - mx)
            den = jnp.sum(p, axis=-1, keepdims=True)
            o = jnp.dot(p.astype(BF16), v2, preferred_element_type=F32) * (1.0 / den)
            lse = jnp.broadcast_to(mx + jnp.log(den), (blk, LANES))
            start = row0 * dil + r
            if dil == 1:
                og_ref[g, pl.ds(start, blk), :] = o
                lg_ref[g, pl.ds(start, blk), :] = lse
            else:
                og_ref[g, pl.ds(start, blk, stride=dil), :] = o
                lg_ref[g, pl.ds(start, blk, stride=dil), :] = lse
            return carry

        lax.fori_loop(0, dil * per_res, block_body, 0)

    top = lg_ref[0]
    for g in range(1, n_groups):
        top = jnp.maximum(top, lg_ref[g])
    num = jnp.zeros_like(top)
    den = jnp.zeros_like(top)
    for g in range(n_groups):
        w = jnp.exp(lg_ref[g] - top)
        num = num + w * og_ref[g]
        den = den + w
    o_ref[...] = (num * (1.0 / den)).astype(o_ref.dtype)


def dsw_attn(projs):
    b = projs[0].shape[0]
    t = projs[0].shape[1] * projs[0].shape[2]
    tile = DSW_TILE
    heads = DSW_HEADS
    in_specs, args, kv_scratch = [], [], []
    for (_, dil), p in zip(DSW_GROUPS, projs):
        rows = tile // dil
        per_res = rows // DSW_BLOCK
        cur = lambda col: pl.BlockSpec((None, dil, rows, HEAD_DIM),
                                       functools.partial(lambda bi, h, n, col: (bi, 0, n, col * heads + h), col=col))
        prev = lambda col: pl.BlockSpec(
            (None, dil, DSW_BLOCK, HEAD_DIM),
            functools.partial(lambda bi, h, n, col, per_res: (bi, 0, jnp.maximum(n * per_res - 1, 0), col * heads + h),
                              col=col, per_res=per_res))
        in_specs += [cur(0), cur(1), cur(2), prev(1), prev(2)]
        args += [p] * 5
        kv_scratch.append(pltpu.VMEM((dil, DSW_BLOCK + rows, HEAD_DIM), BF16))
    n_groups = len(DSW_GROUPS)
    return pl.pallas_call(
        _dsw_attn_kernel,
        grid=(b, heads, t // tile),
        in_specs=in_specs,
        out_specs=pl.BlockSpec((None, tile, HEAD_DIM), lambda bi, h, n: (bi, n, h)),
        out_shape=jax.ShapeDtypeStruct((b, t, heads * HEAD_DIM), BF16),
        scratch_shapes=kv_scratch + kv_scratch + [pltpu.VMEM((n_groups, tile, HEAD_DIM), F32),
                                                  pltpu.VMEM((n_groups, tile, LANES), F32)],
        compiler_params=_params(3),
        name="dsw_attn",
    )(*args)


def prep_dsw_weights(w_in, w_out):
    d = w_in.shape[0]
    w = w_in.reshape(d, len(DSW_GROUPS), 3, DSW_WIDTH).transpose(1, 2, 0, 3).astype(BF16)
    return [w[g] for g in range(len(DSW_GROUPS))], w_out.astype(BF16)


def dsw_layer_prompt(x, gain, weights, q_norm, k_norm):
    w_groups, w_out = weights
    b, t, d = x.shape
    projs = [dsw_inproj(x, gain, jnp.stack([q_norm[g], k_norm[g]]), w_groups[g], dil, 512)
             for g, (_, dil) in enumerate(DSW_GROUPS)]
    o = dsw_attn(projs)
    y = matmul_residual(x.reshape(b * t, d), o.reshape(b * t, -1), w_out, 1024).reshape(b, t, d)
    return y, projs


def _dsw_cache_kernel(*refs):
    *p_refs, o_ref = refs
    layer = pl.program_id(0)
    for li, p_ref in enumerate(p_refs):
        @pl.when(layer == li)
        def _(p_ref=p_ref):
            o_ref[...] = p_ref[...].astype(o_ref.dtype)


def dsw_prompt_caches(projs_layers):
    n_layers = len(projs_layers)
    outs = []
    for g, (window, dil) in enumerate(DSW_GROUPS):
        ps = [pl_[g] for pl_ in projs_layers]
        b, _, sub, _ = ps[0].shape
        last = sub // DSW_BLOCK - 1

        def in_map(l, bi, r, kv, *, li):
            before, after = l < li, l > li
            pick = lambda lo, x, hi: jnp.where(before, lo, jnp.where(after, hi, x))
            return (pick(0, bi, b - 1), pick(0, r, dil - 1), last, 1 + pick(0, kv, 1))

        out = pl.pallas_call(
            _dsw_cache_kernel,
            grid=(n_layers, b, dil, 2),
            in_specs=[pl.BlockSpec((None, None, DSW_BLOCK, DSW_WIDTH), functools.partial(in_map, li=li))
                      for li in range(n_layers)],
            out_specs=pl.BlockSpec((None, None, DSW_BLOCK, DSW_WIDTH),
                                   lambda l, bi, r, kv: (l, bi, 0, r * 2 + kv)),
            out_shape=jax.ShapeDtypeStruct((n_layers, b, DSW_BLOCK, dil * 2 * DSW_WIDTH), F32),
            compiler_params=_params(4),
            name="dsw_prompt_cache",
        )(*ps)
        outs.append(out.reshape(n_layers, b, window, 2, DSW_HEADS, HEAD_DIM))
    return outs


def _dsw_decode_kernel(p_ref, *refs):
    n_groups = len(DSW_GROUPS)
    c_refs = refs[:n_groups]
    o_ref, new_ref = refs[n_groups:]
    scale = HEAD_DIM ** -0.5
    pr = p_ref[...].astype(F32)
    new_ref[...] = pr[:, DSW_WIDTH:]
    for h in range(DSW_HEADS):
        lo = h * HEAD_DIM
        outs, lses = [], []
        for g in range(n_groups):
            q = pr[g:g + 1, lo:lo + HEAD_DIM]
            k_new = pr[g:g + 1, DSW_WIDTH + lo:DSW_WIDTH + lo + HEAD_DIM]
            v_new = pr[g:g + 1, 2 * DSW_WIDTH + lo:2 * DSW_WIDTH + lo + HEAD_DIM]
            k = c_refs[g][:, lo:lo + HEAD_DIM]
            v = c_refs[g][:, DSW_WIDTH + lo:DSW_WIDTH + lo + HEAD_DIM]
            s = jnp.sum(k * q, axis=1, keepdims=True) * scale
            s_new = jnp.sum(k_new * q, axis=1, keepdims=True) * scale
            mx = jnp.maximum(jnp.max(s, axis=0, keepdims=True), s_new)
            p = jnp.exp(s - mx)
            p_new = jnp.exp(s_new - mx)
            den = jnp.sum(p, axis=0, keepdims=True) + p_new
            outs.append((jnp.sum(p * v, axis=0, keepdims=True) + p_new * v_new) * (1.0 / den))
            lses.append(mx + jnp.log(den))
        top = functools.reduce(jnp.maximum, lses)
        ws = [jnp.exp(l - top) for l in lses]
        num = sum(w * o for w, o in zip(ws, outs))
        o_ref[:, lo:lo + HEAD_DIM] = (num * (1.0 / sum(ws))).astype(o_ref.dtype)


def dsw_decode(proj, caches, layer):
    b = proj.shape[0]
    c_specs, c_args = [], []
    for (window, dil), c in zip(DSW_GROUPS, caches):
        c_args.append(c.reshape(c.shape[0], b, window // dil, dil * 2 * DSW_WIDTH))
        c_specs.append(pl.BlockSpec((None, None, window // dil, 2 * DSW_WIDTH),
                                    lambda bi: (layer, bi, 0, 0)))
    n_groups = len(DSW_GROUPS)
    return pl.pallas_call(
        _dsw_decode_kernel,
        grid=(b,),
        in_specs=[pl.BlockSpec((None, n_groups, 3 * DSW_WIDTH), lambda bi: (bi, 0, 0))] + c_specs,
        out_specs=[pl.BlockSpec((None, 1, DSW_WIDTH), lambda bi: (bi, 0, 0)),
                   pl.BlockSpec((None, n_groups, 2 * DSW_WIDTH), lambda bi: (bi, 0, 0))],
        out_shape=[jax.ShapeDtypeStruct((b, 1, DSW_WIDTH), BF16),
                   jax.ShapeDtypeStruct((b, n_groups, 2 * DSW_WIDTH), F32)],
        compiler_params=_params(1),
        name="dsw_decode",
    )(proj, *c_args)


def dsw_layer_sample(x, gain, weights, q_norm, k_norm, caches, layer):
    w_groups, w_out = weights
    b, d = x.shape
    proj = jnp.stack([dsw_inproj(x.reshape(1, b, d), gain, jnp.stack([q_norm[g], k_norm[g]]), w_groups[g], 1, b)
                      .reshape(b, -1) for g in range(len(DSW_GROUPS))], axis=1)
    o, new_rows = dsw_decode(proj, caches, layer)
    return matmul_residual(x, o.reshape(b, -1), w_out, b), new_rows


def _shift_kernel(*refs):
    n_groups = len(DSW_GROUPS)
    new_ref = refs[0]
    c_refs = refs[1:1 + n_groups]
    o_refs = refs[1 + n_groups:1 + 2 * n_groups]
    sem = refs[1 + 2 * n_groups]
    copies = []
    for g in range(n_groups):
        n_layers, _, rows = c_refs[g].shape[:3]
        for l in range(n_layers):
            copies.append(pltpu.make_async_copy(c_refs[g].at[l, :, pl.ds(1, rows - 1)],
                                                o_refs[g].at[l, :, pl.ds(0, rows - 1)],
                                                sem.at[len(copies)]))
            copies.append(pltpu.make_async_copy(new_ref.at[l, g], o_refs[g].at[l, :, rows - 1],
                                                sem.at[len(copies)]))
    for cp in copies:
        cp.start()
    for cp in copies:
        cp.wait()


def shift_caches(caches, new_rows_layers):
    n_groups = len(DSW_GROUPS)
    n_layers, b = caches[0].shape[:2]
    new = jnp.stack(new_rows_layers).transpose(0, 2, 1, 3).reshape(n_layers, n_groups, b, 2, DSW_HEADS, HEAD_DIM)
    return pl.pallas_call(
        _shift_kernel,
        in_specs=[pl.BlockSpec(memory_space=pltpu.VMEM)] + [pl.BlockSpec(memory_space=pl.ANY)] * n_groups,
        out_specs=[pl.BlockSpec(memory_space=pl.ANY)] * n_groups,
        out_shape=[jax.ShapeDtypeStruct(c.shape, c.dtype) for c in caches],
        scratch_shapes=[pltpu.SemaphoreType.DMA((2 * n_groups * n_layers,))],
        compiler_params=pltpu.CompilerParams(vmem_limit_bytes=VMEM_LIMIT),
        name="shift_caches",
    )(new, *caches)


def kernel(x_prompt, x_sample, state_gdn, state_conv, cache_kv_w128, cache_kv_w512, cache_kv_w2048,
           norm_mix, norm_mlp, gdn_w_in, gdn_conv_w, gdn_a_log, gdn_dt_bias, gdn_o_norm, gdn_w_out,
           dsw_w_in, dsw_q_norm, dsw_k_norm, dsw_w_out, mlp_w_up, mlp_w_down):
    b, t, d = x_prompt.shape
    bs = x_sample.shape[0]
    depth = norm_mix.shape[0]
    caches = [cache_kv_w128, cache_kv_w512, cache_kv_w2048]
    yp = x_prompt
    ys = x_sample.reshape(bs, d)
    p_gdn, p_conv, s_gdn, s_conv, p_projs, s_rows = [], [], [], [], [], []
    for i in range(depth):
        j = i // 2
        if i % 2 == 0:
            wts = prep_gdn_weights(gdn_w_in[j], gdn_w_out[j])
            par = (gdn_conv_w[j], gdn_a_log[j], gdn_dt_bias[j], gdn_o_norm[j])
            yp, conv_p, state_p = gdn_layer_prompt(yp, norm_mix[i], wts, *par)
            ys, conv_s, state_s = gdn_layer_sample(ys, norm_mix[i], wts, *par, state_conv[j], state_gdn[j])
            p_gdn.append(state_p)
            p_conv.append(conv_p)
            s_gdn.append(state_s)
            s_conv.append(conv_s)
        else:
            wts = prep_dsw_weights(dsw_w_in[j], dsw_w_out[j])
            yp, projs = dsw_layer_prompt(yp, norm_mix[i], wts, dsw_q_norm[j], dsw_k_norm[j])
            ys, rows = dsw_layer_sample(ys, norm_mix[i], wts, dsw_q_norm[j], dsw_k_norm[j], caches, j)
            p_projs.append(projs)
            s_rows.append(rows)
        w_up, w_down = mlp_w_up[i].astype(BF16), mlp_w_down[i].astype(BF16)
        yp = mlp(yp.reshape(b * t, d), norm_mlp[i], w_up, w_down, 1024, 512).reshape(b, t, d)
        ys = mlp(ys, norm_mlp[i], w_up, w_down, bs, 512)
    p_kv = dsw_prompt_caches(p_projs)
    s_kv = shift_caches(caches, s_rows)
    return (yp, ys.reshape(x_sample.shape),
            jnp.stack(p_gdn), jnp.stack(p_conv), p_kv[0], p_kv[1], p_kv[2],
            jnp.stack(s_gdn), jnp.stack(s_conv), s_kv[0], s_kv[1], s_kv[2])
```

```python
import functools

import jax
import jax.numpy as jnp
from jax import lax
from jax.experimental import pallas as pl
from jax.experimental.pallas import tpu as pltpu

F32 = jnp.float32
BF16 = jnp.bfloat16

EPS = 1e-6
NEG_INF = -1e30
LANES = 128
CONV_TAPS = 4
CHUNK = 64
HEAD_DIM = 128
DSW_BLOCK = 128
DSW_GROUPS = ((128, 1), (512, 4), (2048, 16))
VMEM_LIMIT = 56 * 1024 * 1024


def _params(n_axes, vmem=VMEM_LIMIT):
    return pltpu.CompilerParams(dimension_semantics=("arbitrary",) * n_axes,
                                vmem_limit_bytes=vmem)


def _silu(x):
    return x * (1.0 / (1.0 + jnp.exp(-x)))


def _softplus(x):
    return jnp.maximum(x, 0.0) + jnp.log(1.0 + jnp.exp(-jnp.abs(x)))


def _rms_rows(x, gain_row):
    ms = jnp.mean(x * x, axis=-1, keepdims=True)
    return x * lax.rsqrt(ms + EPS) * gain_row


def _norm_matmul_kernel(x_ref, g_ref, w_ref, o_ref, xn_ref):
    @pl.when(pl.program_id(1) == 0)
    def _():
        xn_ref[...] = _rms_rows(x_ref[...].astype(F32), g_ref[...]).astype(BF16)

    o_ref[...] = jnp.dot(xn_ref[...], w_ref[...], preferred_element_type=F32).astype(o_ref.dtype)


def norm_matmul(x, gain, w, out_dtype, tm, tn):
    m, k = x.shape
    n = w.shape[1]
    tm = min(tm, m)
    tn = min(tn, n)
    return pl.pallas_call(
        _norm_matmul_kernel,
        grid=(m // tm, n // tn),
        in_specs=[pl.BlockSpec((tm, k), lambda i, j: (i, 0)),
                  pl.BlockSpec((1, k), lambda i, j: (0, 0)),
                  pl.BlockSpec((k, tn), lambda i, j: (0, j))],
        out_specs=pl.BlockSpec((tm, tn), lambda i, j: (i, j)),
        out_shape=jax.ShapeDtypeStruct((m, n), out_dtype),
        scratch_shapes=[pltpu.VMEM((tm, k), BF16)],
        compiler_params=_params(2),
        name="norm_matmul",
    )(x, gain.reshape(1, k), w)


def _matmul_residual_kernel(x_ref, a_ref, w_ref, o_ref):
    o_ref[...] = x_ref[...] + jnp.dot(a_ref[...], w_ref[...], preferred_element_type=F32)


def matmul_residual(x, a, w, tm):
    m, d = x.shape
    k = a.shape[1]
    tm = min(tm, m)
    return pl.pallas_call(
        _matmul_residual_kernel,
        grid=(m // tm,),
        in_specs=[pl.BlockSpec((tm, d), lambda i: (i, 0)),
                  pl.BlockSpec((tm, k), lambda i: (i, 0)),
                  pl.BlockSpec((k, d), lambda i: (0, 0))],
        out_specs=pl.BlockSpec((tm, d), lambda i: (i, 0)),
        out_shape=jax.ShapeDtypeStruct((m, d), F32),
        compiler_params=_params(1),
        name="matmul_residual",
    )(x, a, w)


def _mlp_kernel(x_ref, g_ref, wu_ref, wd_ref, o_ref, xn_ref, acc_ref):
    f = pl.program_id(1)

    @pl.when(f == 0)
    def _():
        xn_ref[...] = _rms_rows(x_ref[...], g_ref[...]).astype(BF16)
        acc_ref[...] = jnp.zeros_like(acc_ref)

    h = jnp.dot(xn_ref[...], wu_ref[...], preferred_element_type=F32)
    h = jnp.square(jnp.maximum(h, 0.0)).astype(BF16)
    acc_ref[...] += jnp.dot(h, wd_ref[...], preferred_element_type=F32)

    @pl.when(f == pl.num_programs(1) - 1)
    def _():
        o_ref[...] = x_ref[...] + acc_ref[...]


def mlp(x, gain, w_up, w_down, tm, tf):
    m, d = x.shape
    ff = w_up.shape[1]
    tm = min(tm, m)
    return pl.pallas_call(
        _mlp_kernel,
        grid=(m // tm, ff // tf),
        in_specs=[pl.BlockSpec((tm, d), lambda i, f: (i, 0)),
                  pl.BlockSpec((1, d), lambda i, f: (0, 0)),
                  pl.BlockSpec((d, tf), lambda i, f: (0, f)),
                  pl.BlockSpec((tf, d), lambda i, f: (f, 0))],
        out_specs=pl.BlockSpec((tm, d), lambda i, f: (i, 0)),
        out_shape=jax.ShapeDtypeStruct((m, d), F32),
        scratch_shapes=[pltpu.VMEM((tm, d), BF16), pltpu.VMEM((tm, d), F32)],
        compiler_params=_params(2),
        name="mlp",
    )(x, gain.reshape(1, d), w_up, w_down)


def _dsw_inproj_kernel(*refs, n_res):
    x_refs = refs[:n_res]
    g_ref, hn_ref, w_ref, o_ref, xn_ref = refs[n_res:]
    rows = x_refs[0].shape[0]
    width = w_ref.shape[2]
    for rr in range(n_res):
        xn_ref[rr * rows:(rr + 1) * rows, :] = _rms_rows(x_refs[rr][...], g_ref[...]).astype(BF16)
    for t in range(3):
        acc = jnp.dot(xn_ref[...], w_ref[t], preferred_element_type=F32)
        for h in range(width // HEAD_DIM):
            a = acc[:, h * HEAD_DIM:(h + 1) * HEAD_DIM]
            if t < 2:
                a = _rms_rows(a, hn_ref[t:t + 1, :])
            a = a.astype(o_ref.dtype)
            for rr in range(n_res):
                o_ref[rr, :, t * width + h * HEAD_DIM:t * width + (h + 1) * HEAD_DIM] = (
                    a[rr * rows:(rr + 1) * rows, :])


def dsw_inproj(x, gain, head_gains, w3, dil, tile_rows):
    b, t, d = x.shape
    width = w3.shape[2]
    sub = t // dil
    rows = min(tile_rows, sub)
    n_res = min(max(tile_rows // rows, 1), dil)
    xv = x.reshape(b, sub, dil * d)
    x_specs = [pl.BlockSpec((None, rows, d),
                            functools.partial(lambda bi, rb, i, rr: (bi, i, rb * n_res + rr), rr=rr))
               for rr in range(n_res)]
    return pl.pallas_call(
        functools.partial(_dsw_inproj_kernel, n_res=n_res),
        grid=(b, dil // n_res, sub // rows),
        in_specs=x_specs + [pl.BlockSpec((1, d), lambda bi, rb, i: (0, 0)),
                            pl.BlockSpec((2, HEAD_DIM), lambda bi, rb, i: (0, 0)),
                            pl.BlockSpec((3, d, width), lambda bi, rb, i: (0, 0, 0))],
        out_specs=pl.BlockSpec((None, n_res, rows, 3 * width), lambda bi, rb, i: (bi, rb, i, 0)),
        out_shape=jax.ShapeDtypeStruct((b, dil, sub, 3 * width), BF16),
        scratch_shapes=[pltpu.VMEM((n_res * rows, d), BF16)],
        compiler_params=_params(3),
        name="dsw_inproj",
    )(*([xv] * n_res), gain.reshape(1, d), head_gains, w3)


TRI_BASE = 8


def _pair_block_diag(m):
    n = m.shape[0]
    mb = m.astype(BF16)
    lane = lax.broadcasted_iota(jnp.int32, mb.shape, 1)
    zero = jnp.zeros_like(mb)
    return jnp.concatenate([jnp.where(lane < n, mb, zero), jnp.where(lane >= n, mb, zero)], axis=0)


def _pair_mm(u, w):
    return jnp.dot(u.astype(BF16), _pair_block_diag(w), preferred_element_type=F32)


def _tri_inverse_pairs(a_list):
    n = a_list[0].shape[0]
    ii = lax.broadcasted_iota(jnp.int32, (n, 2 * n), 0)
    jj = lax.broadcasted_iota(jnp.int32, (n, 2 * n), 1) % n
    same_block = lambda size: (ii // size) == (jj // size)
    ps = [jnp.where(same_block(TRI_BASE), a, 0.0) for a in a_list]
    xs = [jnp.where(ii == jj, 1.0, 0.0) - p for p in ps]
    for _ in range((TRI_BASE - 1).bit_length() - 1):
        ps = [_pair_mm(p, p) for p in ps]
        xs = [x + _pair_mm(x, p) for x, p in zip(xs, ps)]
    size = TRI_BASE
    while size < n:
        off = same_block(2 * size) & jnp.logical_not(same_block(size))
        ys = [_pair_mm(x, jnp.where(off, a, 0.0)) for x, a in zip(xs, a_list)]
        xs = [x - _pair_mm(y, x) for x, y in zip(xs, ys)]
        size *= 2
    return xs


def _gdn_prompt_kernel(qkv_ref, z_ref, gate_ref, cw_ref, alog_ref, dtb_ref, onorm_ref,
                       o_ref, s_out_ref,
                       xe_ref, qk_ref, v_ref, bb_ref, gcb_ref, tp_ref, pp_ref, s_ref, oscr_ref,
                       *, n_qk, n_v):
    tb = qkv_ref.shape[0]
    nchunk = tb // CHUNK
    t_idx = pl.program_id(1)
    halo = 8

    @pl.when(t_idx == 0)
    def _():
        xe_ref[:, halo - (CONV_TAPS - 1):halo, :] = jnp.zeros((xe_ref.shape[0], CONV_TAPS - 1, LANES), F32)
        s_ref[...] = jnp.zeros_like(s_ref)

    beta = 1.0 / (1.0 + jnp.exp(-gate_ref[:, :LANES]))
    g = -jnp.exp(alog_ref[...]) * _softplus(gate_ref[:, LANES:] + dtb_ref[...])
    pos = lax.broadcasted_iota(jnp.int32, (tb, LANES), 0) % CHUNK
    shift = 1
    while shift < CHUNK:
        g = g + jnp.where(pos >= shift, pltpu.roll(g, shift, axis=0), 0.0)
        shift *= 2
    for h in range(n_v):
        bb_ref[h] = jnp.broadcast_to(beta[:, h:h + 1], (tb, LANES))
        gcb_ref[h] = jnp.broadcast_to(g[:, h:h + 1], (tb, LANES))

    for cb in range(n_qk * 2 + n_v):
        lo = cb * LANES
        xe_ref[cb, halo:halo + tb, :] = qkv_ref[:, lo:lo + LANES].astype(F32)
        y = cw_ref[0:1, lo:lo + LANES] * xe_ref[cb, halo - 3:halo - 3 + tb, :]
        for j in range(1, CONV_TAPS):
            y = y + cw_ref[j:j + 1, lo:lo + LANES] * xe_ref[cb, halo - 3 + j:halo - 3 + j + tb, :]
        y = _silu(y)
        xe_ref[cb, halo - 3:halo, :] = xe_ref[cb, halo + tb - 3:halo + tb, :]
        if cb < 2 * n_qk:
            y = y * lax.rsqrt(jnp.sum(y * y, axis=-1, keepdims=True) + EPS)
            if cb < n_qk:
                y = y * (HEAD_DIM ** -0.5)
            yb = y.astype(BF16)
            half = 0 if cb < n_qk else CHUNK
            for c in range(nchunk):
                qk_ref[cb % n_qk, c, half:half + CHUNK, :] = yb[c * CHUNK:(c + 1) * CHUNK, :]
        else:
            v_ref[cb - 2 * n_qk] = y

    ii = lax.broadcasted_iota(jnp.int32, (CHUNK, 2 * CHUNK), 0)
    lane = lax.broadcasted_iota(jnp.int32, (CHUNK, 2 * CHUNK), 1)
    jj = lane % CHUNK
    causal = ii >= jj
    strict = ii > jj
    pairs_per_iter = 2

    def inverse_body(it, carry):
        probs = [(it * pairs_per_iter + dj, c) for dj in range(pairs_per_iter) for c in range(nchunk)]
        grams, a_list, decays = [], [], []
        for j, c in probs:
            qk = qk_ref[j, c]
            k2 = jnp.concatenate([qk[CHUNK:], qk[CHUNK:]], axis=0)
            grams.append(lax.dot_general(qk, k2, (((1,), (1,)), ((), ())), preferred_element_type=F32))
        for (j, c), gram in zip(probs, grams):
            rows = slice(c * CHUNK, (c + 1) * CHUNK)
            gc = jnp.where(lane < CHUNK, gcb_ref[2 * j, rows, :], gcb_ref[2 * j + 1, rows, :])
            bt = jnp.where(lane < CHUNK, bb_ref[2 * j, rows, :], bb_ref[2 * j + 1, rows, :])
            gc_row = jnp.sum(jnp.where(ii == jj, gc, 0.0), axis=0, keepdims=True)
            decay = jnp.where(causal, jnp.exp(jnp.where(causal, gc - gc_row, 0.0)), 0.0)
            a_list.append(jnp.where(strict, bt * gram[CHUNK:] * decay, 0.0))
            pp_ref[j, c] = (gram[:CHUNK] * decay).astype(BF16)
        for (j, c), t_inv in zip(probs, _tri_inverse_pairs(a_list)):
            tp_ref[j, c] = t_inv.astype(BF16)
        return carry

    lax.fori_loop(0, n_qk // pairs_per_iter, inverse_body, 0)

    def block_diag2(m):
        mb = m.astype(BF16)
        zero = jnp.zeros((CHUNK, HEAD_DIM), BF16)
        return jnp.concatenate([jnp.concatenate([mb[:, :HEAD_DIM], zero], axis=1),
                                jnp.concatenate([zero, mb[:, HEAD_DIM:]], axis=1)], axis=0)

    def chunk_body(c, carry):
        rows = pl.ds(pl.multiple_of(c * CHUNK, CHUNK), CHUNK)
        heads = range(n_qk)
        pair = lambda ref, j: jnp.concatenate([ref[2 * j, rows, :], ref[2 * j + 1, rows, :]], axis=1)
        qks = [qk_ref[j, c] for j in heads]
        projs = [jnp.dot(qks[j], jnp.concatenate([s_ref[2 * j], s_ref[2 * j + 1]], axis=1).astype(BF16),
                         preferred_element_type=F32) for j in heads]
        gcs = [pair(gcb_ref, j) for j in heads]
        e_gcs = [jnp.exp(gc) for gc in gcs]
        rhss = [pair(bb_ref, j) * (pair(v_ref, j) - e_gcs[j] * projs[j][CHUNK:]) for j in heads]
        v_news = [jnp.dot(tp_ref[j, c], block_diag2(rhss[j]), preferred_element_type=F32) for j in heads]
        outs = [e_gcs[j] * projs[j][:CHUNK]
                + jnp.dot(pp_ref[j, c], block_diag2(v_news[j]), preferred_element_type=F32) for j in heads]
        g_lasts = [gc[CHUNK - 1:CHUNK, :] for gc in gcs]
        v_decs = [(jnp.exp(g_lasts[j] - gcs[j]) * v_news[j]).astype(BF16) for j in heads]
        d_states = [lax.dot_general(qks[j][CHUNK:], v_decs[j], (((0,), (0,)), ((), ())),
                                    preferred_element_type=F32) for j in heads]
        for j in heads:
            for e in range(2):
                lanes = slice(e * HEAD_DIM, (e + 1) * HEAD_DIM)
                oscr_ref[2 * j + e, rows, :] = outs[j][:, lanes]
                s_ref[2 * j + e] = jnp.exp(g_lasts[j][:, lanes]) * s_ref[2 * j + e] + d_states[j][:, lanes]
        return carry

    lax.fori_loop(0, nchunk, chunk_body, 0)

    for h in range(n_v):
        lo = h * HEAD_DIM
        o = _rms_rows(oscr_ref[h], onorm_ref[...])
        o_ref[:, lo:lo + HEAD_DIM] = (o * _silu(z_ref[:, lo:lo + HEAD_DIM].astype(F32))).astype(o_ref.dtype)

    @pl.when(t_idx == pl.num_programs(1) - 1)
    def _():
        s_out_ref[...] = s_ref[...]


def gdn_prompt(qkvz, gates, conv_w, a_log, dt_bias, o_norm, n_qk, n_v, tb):
    b, t, _ = qkvz.shape
    assert n_v == 2 * n_qk and 2 * CHUNK == LANES
    conv_dim = (2 * n_qk + n_v) * HEAD_DIM
    v_dim = n_v * HEAD_DIM
    tb = min(tb, t)
    nchunk = tb // CHUNK
    pad = lambda p: jnp.zeros((1, LANES), F32).at[0, :n_v].set(p.astype(F32))
    return pl.pallas_call(
        functools.partial(_gdn_prompt_kernel, n_qk=n_qk, n_v=n_v),
        grid=(b, t // tb),
        in_specs=[pl.BlockSpec((None, tb, conv_dim), lambda bi, ti: (bi, ti, 0)),
                  pl.BlockSpec((None, tb, v_dim), lambda bi, ti: (bi, ti, conv_dim // v_dim)),
                  pl.BlockSpec((None, tb, 2 * LANES), lambda bi, ti: (bi, ti, 0)),
                  pl.BlockSpec((CONV_TAPS, conv_dim), lambda bi, ti: (0, 0)),
                  pl.BlockSpec((1, LANES), lambda bi, ti: (0, 0)),
                  pl.BlockSpec((1, LANES), lambda bi, ti: (0, 0)),
                  pl.BlockSpec((1, HEAD_DIM), lambda bi, ti: (0, 0))],
        out_specs=[pl.BlockSpec((None, tb, v_dim), lambda bi, ti: (bi, ti, 0)),
                   pl.BlockSpec((None, n_v, HEAD_DIM, HEAD_DIM), lambda bi, ti: (bi, 0, 0, 0))],
        out_shape=[jax.ShapeDtypeStruct((b, t, v_dim), BF16),
                   jax.ShapeDtypeStruct((b, n_v, HEAD_DIM, HEAD_DIM), F32)],
        scratch_shapes=[pltpu.VMEM((2 * n_qk + n_v, tb + 8, LANES), F32),
                        pltpu.VMEM((n_qk, nchunk, 2 * CHUNK, HEAD_DIM), BF16),
                        pltpu.VMEM((n_v, tb, HEAD_DIM), F32),
                        pltpu.VMEM((n_v, tb, LANES), F32),
                        pltpu.VMEM((n_v, tb, LANES), F32),
                        pltpu.VMEM((n_qk, nchunk, CHUNK, 2 * CHUNK), BF16),
                        pltpu.VMEM((n_qk, nchunk, CHUNK, 2 * CHUNK), BF16),
                        pltpu.VMEM((n_v, HEAD_DIM, HEAD_DIM), F32),
                        pltpu.VMEM((n_v, tb, HEAD_DIM), F32)],
        compiler_params=_params(2),
        name="gdn_prompt",
    )(qkvz, qkvz, gates, conv_w, pad(a_log), pad(dt_bias), o_norm.reshape(1, HEAD_DIM))


GDN_QK_HEADS = 8
GDN_V_HEADS = 16
GDN_CONV_DIM = (2 * GDN_QK_HEADS + GDN_V_HEADS) * HEAD_DIM
GDN_V_DIM = GDN_V_HEADS * HEAD_DIM


def prep_gdn_weights(w_in, w_out):
    d = w_in.shape[0]
    main = GDN_CONV_DIM + GDN_V_DIM
    w_main = w_in[:, :main].astype(BF16)
    w_gate = jnp.zeros((d, 2 * LANES), F32)
    w_gate = w_gate.at[:, :GDN_V_HEADS].set(w_in[:, main:main + GDN_V_HEADS])
    w_gate = w_gate.at[:, LANES:LANES + GDN_V_HEADS].set(w_in[:, main + GDN_V_HEADS:])
    return w_main, w_gate.astype(BF16), w_out.astype(BF16)


def gdn_layer_prompt(x, gain, weights, conv_w, a_log, dt_bias, o_norm):
    w_main, w_gate, w_out = weights
    b, t, d = x.shape
    x2 = x.reshape(b * t, d)
    qkvz = norm_matmul(x2, gain, w_main, BF16, 1024, 512).reshape(b, t, -1)
    gates = norm_matmul(x2, gain, w_gate, F32, 1024, 2 * LANES).reshape(b, t, -1)
    o, state = gdn_prompt(qkvz, gates, conv_w, a_log, dt_bias, o_norm, GDN_QK_HEADS, GDN_V_HEADS, 256)
    y = matmul_residual(x2, o.reshape(b * t, -1), w_out, 1024).reshape(b, t, d)
    conv_tail = qkvz[:, t - (CONV_TAPS - 1):, :GDN_CONV_DIM].astype(F32)
    return y, conv_tail, state


def _gdn_step_kernel(qkvz_ref, gate_ref, conv_ref, s_ref, cw_ref, alog_ref, dtb_ref, onorm_ref,
                     o_ref, convn_ref, sn_ref, *, n_qk, n_v):
    rep = n_v // n_qk
    conv_dim = (2 * n_qk + n_v) * HEAD_DIM
    x = qkvz_ref[:, :conv_dim]
    y = cw_ref[CONV_TAPS - 1:CONV_TAPS, :] * x
    for j in range(CONV_TAPS - 1):
        y = y + cw_ref[j:j + 1, :] * conv_ref[j:j + 1, :]
    y = _silu(y)
    convn_ref[0:CONV_TAPS - 2, :] = conv_ref[1:CONV_TAPS - 1, :]
    convn_ref[CONV_TAPS - 2:CONV_TAPS - 1, :] = x

    beta = 1.0 / (1.0 + jnp.exp(-gate_ref[:, :LANES]))
    g = -jnp.exp(alog_ref[...]) * _softplus(gate_ref[:, LANES:] + dtb_ref[...])
    eye = (lax.broadcasted_iota(jnp.int32, (HEAD_DIM, HEAD_DIM), 0)
           == lax.broadcasted_iota(jnp.int32, (HEAD_DIM, HEAD_DIM), 1))

    def column(row):
        return jnp.sum(jnp.where(eye, row, 0.0), axis=1, keepdims=True)

    def l2(row):
        return row * lax.rsqrt(jnp.sum(row * row, axis=-1, keepdims=True) + EPS)

    for j in range(n_qk):
        q_col = column(l2(y[:, j * HEAD_DIM:(j + 1) * HEAD_DIM]) * (HEAD_DIM ** -0.5))
        k_col = column(l2(y[:, (n_qk + j) * HEAD_DIM:(n_qk + j + 1) * HEAD_DIM]))
        for e in range(rep):
            h = rep * j + e
            lo = h * HEAD_DIM
            v = y[:, 2 * n_qk * HEAD_DIM + lo:2 * n_qk * HEAD_DIM + lo + HEAD_DIM]
            s = s_ref[h]
            e_g = jnp.exp(g[:, h:h + 1])
            k_s = jnp.sum(s * k_col, axis=0, keepdims=True)
            v_new = beta[:, h:h + 1] * (v - e_g * k_s)
            s_new = e_g * s + k_col * v_new
            sn_ref[h] = s_new
            o = _rms_rows(jnp.sum(s_new * q_col, axis=0, keepdims=True), onorm_ref[...])
            z = qkvz_ref[:, conv_dim + lo:conv_dim + lo + HEAD_DIM]
            o_ref[:, lo:lo + HEAD_DIM] = (o * _silu(z)).astype(o_ref.dtype)


def gdn_step(qkvz, gates, conv_state, state, conv_w, a_log, dt_bias, o_norm, n_qk, n_v):
    b = qkvz.shape[0]
    conv_dim = (2 * n_qk + n_v) * HEAD_DIM
    v_dim = n_v * HEAD_DIM
    pad = lambda p: jnp.zeros((1, LANES), F32).at[0, :n_v].set(p.astype(F32))
    row = lambda n: pl.BlockSpec((None, 1, n), lambda bi: (bi, 0, 0))
    const = lambda shape: pl.BlockSpec(shape, lambda bi: (0,) * len(shape))
    st = pl.BlockSpec((None, n_v, HEAD_DIM, HEAD_DIM), lambda bi: (bi, 0, 0, 0))
    cv = pl.BlockSpec((None, CONV_TAPS - 1, conv_dim), lambda bi: (bi, 0, 0))
    return pl.pallas_call(
        functools.partial(_gdn_step_kernel, n_qk=n_qk, n_v=n_v),
        grid=(b,),
        in_specs=[row(conv_dim + v_dim), row(2 * LANES), cv, st,
                  const((CONV_TAPS, conv_dim)), const((1, LANES)), const((1, LANES)), const((1, HEAD_DIM))],
        out_specs=[row(v_dim), cv, st],
        out_shape=[jax.ShapeDtypeStruct((b, 1, v_dim), BF16),
                   jax.ShapeDtypeStruct(conv_state.shape, F32),
                   jax.ShapeDtypeStruct(state.shape, F32)],
        compiler_params=_params(1),
        name="gdn_step",
    )(qkvz.reshape(b, 1, -1), gates.reshape(b, 1, -1), conv_state, state,
      conv_w, pad(a_log), pad(dt_bias), o_norm.reshape(1, HEAD_DIM))


def gdn_layer_sample(x, gain, weights, conv_w, a_log, dt_bias, o_norm, conv_state, state):
    w_main, w_gate, w_out = weights
    b = x.shape[0]
    qkvz = norm_matmul(x, gain, w_main, F32, b, 512)
    gates = norm_matmul(x, gain, w_gate, F32, b, 2 * LANES)
    o, conv_new, state_new = gdn_step(qkvz, gates, conv_state, state, conv_w, a_log, dt_bias, o_norm,
                                      GDN_QK_HEADS, GDN_V_HEADS)
    return matmul_residual(x, o.reshape(b, -1), w_out, b), conv_new, state_new


DSW_HEADS = 8
DSW_WIDTH = DSW_HEADS * HEAD_DIM
DSW_TILE = DSW_BLOCK * max(d for _, d in DSW_GROUPS)


def _dsw_attn_kernel(*refs):
    n_groups = len(DSW_GROUPS)
    ins = [refs[5 * g:5 * g + 5] for g in range(n_groups)]
    o_ref = refs[5 * n_groups]
    scratch = refs[5 * n_groups + 1:]
    kf_refs, vf_refs = scratch[:n_groups], scratch[n_groups:2 * n_groups]
    og_ref, lg_ref = scratch[2 * n_groups:]
    first_tile = pl.program_id(2) == 0
    blk = DSW_BLOCK
    qi = lax.broadcasted_iota(jnp.int32, (blk, 2 * blk), 0)
    ki = lax.broadcasted_iota(jnp.int32, (blk, 2 * blk), 1)
    band = (ki >= qi) & (ki <= qi + blk)
    scale = HEAD_DIM ** -0.5

    for g, (_, dil) in enumerate(DSW_GROUPS):
        q_ref, kc_ref, vc_ref, kp_ref, vp_ref = ins[g]
        kf_ref, vf_ref = kf_refs[g], vf_refs[g]
        per_res = q_ref.shape[1] // blk
        kf_ref[:, :blk, :] = kp_ref[...]
        kf_ref[:, blk:, :] = kc_ref[...]
        vf_ref[:, :blk, :] = vp_ref[...]
        vf_ref[:, blk:, :] = vc_ref[...]

        def block_body(bi, carry, *, g=g, dil=dil, per_res=per_res, q_ref=q_ref, kf_ref=kf_ref, vf_ref=vf_ref):
            r = bi // per_res
            m = bi % per_res
            row0 = pl.multiple_of(m * blk, blk)
            q = q_ref[r, pl.ds(row0, blk), :]
            k2 = kf_ref[r, pl.ds(row0, 2 * blk), :]
            v2 = vf_ref[r, pl.ds(row0, 2 * blk), :]
            s = lax.dot_general(q, k2, (((1,), (1,)), ((), ())), preferred_element_type=F32) * scale
            valid = band & ((ki >= blk) | (m > 0) | jnp.logical_not(first_tile))
            s = jnp.where(valid, s, NEG_INF)
            mx = jnp.max(s, axis=-1, keepdims=True)
            p = jnp.exp(s - mx)
            den = jnp.sum(p, axis=-1, keepdims=True)
            o = jnp.dot(p.astype(BF16), v2, preferred_element_type=F32) * (1.0 / den)
            lse = jnp.broadcast_to(mx + jnp.log(den), (blk, LANES))
            start = row0 * dil + r
            if dil == 1:
                og_ref[g, pl.ds(start, blk), :] = o
                lg_ref[g, pl.ds(start, blk), :] = lse
            else:
                og_ref[g, pl.ds(start, blk, stride=dil), :] = o
                lg_ref[g, pl.ds(start, blk, stride=dil), :] = lse
            return carry

        lax.fori_loop(0, dil * per_res, block_body, 0)

    top = lg_ref[0]
    for g in range(1, n_groups):
        top = jnp.maximum(top, lg_ref[g])
    num = jnp.zeros_like(top)
    den = jnp.zeros_like(top)
    for g in range(n_groups):
        w = jnp.exp(lg_ref[g] - top)
        num = num + w * og_ref[g]
        den = den + w
    o_ref[...] = (num * (1.0 / den)).astype(o_ref.dtype)


def dsw_attn(projs):
    b = projs[0].shape[0]
    t = projs[0].shape[1] * projs[0].shape[2]
    tile = DSW_TILE
    heads = DSW_HEADS
    in_specs, args, kv_scratch = [], [], []
    for (_, dil), p in zip(DSW_GROUPS, projs):
        rows = tile // dil
        per_res = rows // DSW_BLOCK
        cur = lambda col: pl.BlockSpec((None, dil, rows, HEAD_DIM),
                                       functools.partial(lambda bi, h, n, col: (bi, 0, n, col * heads + h), col=col))
        prev = lambda col: pl.BlockSpec(
            (None, dil, DSW_BLOCK, HEAD_DIM),
            functools.partial(lambda bi, h, n, col, per_res: (bi, 0, jnp.maximum(n * per_res - 1, 0), col * heads + h),
                              col=col, per_res=per_res))
        in_specs += [cur(0), cur(1), cur(2), prev(1), prev(2)]
        args += [p] * 5
        kv_scratch.append(pltpu.VMEM((dil, DSW_BLOCK + rows, HEAD_DIM), BF16))
    n_groups = len(DSW_GROUPS)
    return pl.pallas_call(
        _dsw_attn_kernel,
        grid=(b, heads, t // tile),
        in_specs=in_specs,
        out_specs=pl.BlockSpec((None, tile, HEAD_DIM), lambda bi, h, n: (bi, n, h)),
        out_shape=jax.ShapeDtypeStruct((b, t, heads * HEAD_DIM), BF16),
        scratch_shapes=kv_scratch + kv_scratch + [pltpu.VMEM((n_groups, tile, HEAD_DIM), F32),
                                                  pltpu.VMEM((n_groups, tile, LANES), F32)],
        compiler_params=_params(3),
        name="dsw_attn",
    )(*args)


def prep_dsw_weights(w_in, w_out):
    d = w_in.shape[0]
    w = w_in.reshape(d, len(DSW_GROUPS), 3, DSW_WIDTH).transpose(1, 2, 0, 3).astype(BF16)
    return [w[g] for g in range(len(DSW_GROUPS))], w_out.astype(BF16)


def dsw_layer_prompt(x, gain, weights, q_norm, k_norm):
    w_groups, w_out = weights
    b, t, d = x.shape
    projs = [dsw_inproj(x, gain, jnp.stack([q_norm[g], k_norm[g]]), w_groups[g], dil, 512)
             for g, (_, dil) in enumerate(DSW_GROUPS)]
    o = dsw_attn(projs)
    y = matmul_residual(x.reshape(b * t, d), o.reshape(b * t, -1), w_out, 1024).reshape(b, t, d)
    return y, projs


def _dsw_cache_kernel(*refs):
    *p_refs, o_ref = refs
    layer = pl.program_id(0)
    for li, p_ref in enumerate(p_refs):
        @pl.when(layer == li)
        def _(p_ref=p_ref):
            o_ref[...] = p_ref[...].astype(o_ref.dtype).reshape(o_ref.shape)


def dsw_prompt_caches(projs_layers):
    n_layers = len(projs_layers)
    outs = []
    for g, (window, dil) in enumerate(DSW_GROUPS):
        ps = [pl_[g] for pl_ in projs_layers]
        b, _, sub, _ = ps[0].shape
        last = sub // DSW_BLOCK - 1

        def in_map(l, bi, r, kv, *, li):
            before, after = l < li, l > li
            pick = lambda lo, x, hi: jnp.where(before, lo, jnp.where(after, hi, x))
            return (pick(0, bi, b - 1), pick(0, r, dil - 1), last, 1 + pick(0, kv, 1))

        out = pl.pallas_call(
            _dsw_cache_kernel,
            grid=(n_layers, b, dil, 2),
            in_specs=[pl.BlockSpec((None, None, DSW_BLOCK, DSW_WIDTH), functools.partial(in_map, li=li))
                      for li in range(n_layers)],
            out_specs=pl.BlockSpec((None, None, DSW_BLOCK, None, DSW_HEADS, HEAD_DIM),
                                   lambda l, bi, r, kv: (l, bi, 0, r * 2 + kv, 0, 0)),
            out_shape=jax.ShapeDtypeStruct((n_layers, b, DSW_BLOCK, dil * 2, DSW_HEADS, HEAD_DIM), F32),
            compiler_params=_params(4),
            name="dsw_prompt_cache",
        )(*ps)
        outs.append(out.reshape(n_layers, b, window, 2, DSW_HEADS, HEAD_DIM))
    return outs


def _dsw_decode_kernel(p_ref, *refs):
    n_groups = len(DSW_GROUPS)
    c_refs = refs[:n_groups]
    o_ref, new_ref = refs[n_groups:]
    scale = HEAD_DIM ** -0.5
    outs, lses = [], []
    for g in range(n_groups):
        q = p_ref[g, 0].astype(F32)
        k_new = p_ref[g, 1].astype(F32)
        v_new = p_ref[g, 2].astype(F32)
        new_ref[g, 0] = k_new
        new_ref[g, 1] = v_new
        s = jnp.sum(c_refs[g][:, 0] * q[None], axis=-1, keepdims=True) * scale
        s_new = jnp.sum(k_new * q, axis=-1, keepdims=True) * scale
        mx = jnp.maximum(jnp.max(s, axis=0), s_new)
        p = jnp.exp(s - mx[None])
        p_new = jnp.exp(s_new - mx)
        den = jnp.sum(p, axis=0) + p_new
        outs.append((jnp.sum(p * c_refs[g][:, 1], axis=0) + p_new * v_new) * (1.0 / den))
        lses.append(mx + jnp.log(den))
    top = functools.reduce(jnp.maximum, lses)
    ws = [jnp.exp(l - top) for l in lses]
    num = sum(w * o for w, o in zip(ws, outs))
    o_ref[...] = (num * (1.0 / sum(ws))).astype(o_ref.dtype)


def dsw_decode(proj, caches, layer):
    b = proj.shape[0]
    n_groups = len(DSW_GROUPS)
    c_specs, c_args = [], []
    for (window, dil), c in zip(DSW_GROUPS, caches):
        c_args.append(c.reshape(c.shape[0], b, window // dil, dil, 2, DSW_HEADS, HEAD_DIM))
        c_specs.append(pl.BlockSpec((None, None, window // dil, None, 2, DSW_HEADS, HEAD_DIM),
                                    lambda bi: (layer, bi, 0, 0, 0, 0, 0)))
    return pl.pallas_call(
        _dsw_decode_kernel,
        grid=(b,),
        in_specs=[pl.BlockSpec((None, n_groups, 3, DSW_HEADS, HEAD_DIM), lambda bi: (bi, 0, 0, 0, 0))] + c_specs,
        out_specs=[pl.BlockSpec((None, DSW_HEADS, HEAD_DIM), lambda bi: (bi, 0, 0)),
                   pl.BlockSpec((None, n_groups, 2, DSW_HEADS, HEAD_DIM), lambda bi: (bi, 0, 0, 0, 0))],
        out_shape=[jax.ShapeDtypeStruct((b, DSW_HEADS, HEAD_DIM), BF16),
                   jax.ShapeDtypeStruct((b, n_groups, 2, DSW_HEADS, HEAD_DIM), F32)],
        compiler_params=_params(1),
        name="dsw_decode",
    )(proj, *c_args)


def dsw_layer_sample(x, gain, weights, q_norm, k_norm, caches, layer):
    w_groups, w_out = weights
    b, d = x.shape
    proj = jnp.stack([dsw_inproj(x.reshape(1, b, d), gain, jnp.stack([q_norm[g], k_norm[g]]), w_groups[g], 1, b)
                      .reshape(b, 3, DSW_HEADS, HEAD_DIM) for g in range(len(DSW_GROUPS))], axis=1)
    o, new_rows = dsw_decode(proj, caches, layer)
    return matmul_residual(x, o.reshape(b, -1), w_out, b), new_rows


SHIFT_ROWS = 256


def _shift_kernel(c_ref, next_ref, new_ref, o_ref):
    rows = c_ref.shape[0]
    is_last = pl.program_id(2) == pl.num_programs(2) - 1
    o_ref[0:rows - 1] = c_ref[1:rows]

    @pl.when(is_last)
    def _():
        o_ref[rows - 1] = new_ref[...]

    @pl.when(jnp.logical_not(is_last))
    def _():
        o_ref[rows - 1] = next_ref[0]


def shift_caches(caches, new_rows_layers):
    new = jnp.stack(new_rows_layers)
    outs = []
    for g, c in enumerate(caches):
        n_layers, b, window = c.shape[:3]
        rows = min(SHIFT_ROWS, window)
        tile = (2, DSW_HEADS, HEAD_DIM)
        blk = pl.BlockSpec((None, None, rows) + tile, lambda l, bi, i: (l, bi, i, 0, 0, 0))
        nxt = pl.BlockSpec((None, None, 1) + tile,
                           lambda l, bi, i: (l, bi, jnp.minimum((i + 1) * rows, window - 1), 0, 0, 0))
        outs.append(pl.pallas_call(
            _shift_kernel,
            grid=(n_layers, b, window // rows),
            in_specs=[blk, nxt,
                      pl.BlockSpec((None, None, None) + tile, functools.partial(lambda l, bi, i, g: (l, bi, g, 0, 0, 0), g=g))],
            out_specs=blk,
            out_shape=jax.ShapeDtypeStruct(c.shape, c.dtype),
            compiler_params=_params(3),
            name="shift_cache",
        )(c, c, new))
    return outs


def kernel(x_prompt, x_sample, state_gdn, state_conv, cache_kv_w128, cache_kv_w512, cache_kv_w2048,
           norm_mix, norm_mlp, gdn_w_in, gdn_conv_w, gdn_a_log, gdn_dt_bias, gdn_o_norm, gdn_w_out,
           dsw_w_in, dsw_q_norm, dsw_k_norm, dsw_w_out, mlp_w_up, mlp_w_down):
    b, t, d = x_prompt.shape
    bs = x_sample.shape[0]
    depth = norm_mix.shape[0]
    caches = [cache_kv_w128, cache_kv_w512, cache_kv_w2048]
    yp = x_prompt
    ys = x_sample.reshape(bs, d)
    p_gdn, p_conv, s_gdn, s_conv, p_projs, s_rows = [], [], [], [], [], []
    for i in range(depth):
        j = i // 2
        if i % 2 == 0:
            wts = prep_gdn_weights(gdn_w_in[j], gdn_w_out[j])
            par = (gdn_conv_w[j], gdn_a_log[j], gdn_dt_bias[j], gdn_o_norm[j])
            yp, conv_p, state_p = gdn_layer_prompt(yp, norm_mix[i], wts, *par)
            ys, conv_s, state_s = gdn_layer_sample(ys, norm_mix[i], wts, *par, state_conv[j], state_gdn[j])
            p_gdn.append(state_p)
            p_conv.append(conv_p)
            s_gdn.append(state_s)
            s_conv.append(conv_s)
        else:
            wts = prep_dsw_weights(dsw_w_in[j], dsw_w_out[j])
            yp, projs = dsw_layer_prompt(yp, norm_mix[i], wts, dsw_q_norm[j], dsw_k_norm[j])
            ys, rows = dsw_layer_sample(ys, norm_mix[i], wts, dsw_q_norm[j], dsw_k_norm[j], caches, j)
            p_projs.append(projs)
            s_rows.append(rows)
        w_up, w_down = mlp_w_up[i].astype(BF16), mlp_w_down[i].astype(BF16)
        yp = mlp(yp.reshape(b * t, d), norm_mlp[i], w_up, w_down, 1024, 512).reshape(b, t, d)
        ys = mlp(ys, norm_mlp[i], w_up, w_down, bs, 512)
    p_kv = dsw_prompt_caches(p_projs)
    s_kv = shift_caches(caches, s_rows)
    return (yp, ys.reshape(x_sample.shape),
            jnp.stack(p_gdn), jnp.stack(p_conv), p_kv[0], p_kv[1], p_kv[2],
            jnp.stack(s_gdn), jnp.stack(s_conv), s_kv[0], s_kv[1], s_kv[2])
```

```python
import functools

import jax
import jax.numpy as jnp
from jax import lax
from jax.experimental import pallas as pl
from jax.experimental.pallas import tpu as pltpu

F32 = jnp.float32
BF16 = jnp.bfloat16

EPS = 1e-6
NEG_INF = -1e30
LANES = 128
CONV_TAPS = 4
CHUNK = 64
HEAD_DIM = 128
DSW_BLOCK = 128
DSW_GROUPS = ((128, 1), (512, 4), (2048, 16))
VMEM_LIMIT = 56 * 1024 * 1024


def _params(n_axes, vmem=VMEM_LIMIT):
    return pltpu.CompilerParams(dimension_semantics=("arbitrary",) * n_axes,
                                vmem_limit_bytes=vmem)


def _sigmoid(x):
    return 0.5 + 0.5 * jnp.tanh(0.5 * x)


def _silu(x):
    h = 0.5 * x
    return h + h * jnp.tanh(h)


def _softplus(x):
    return jnp.maximum(x, 0.0) + jnp.log(1.0 + jnp.exp(-jnp.abs(x)))


def _rms_rows(x, gain_row):
    ms = jnp.mean(x * x, axis=-1, keepdims=True)
    return x * lax.rsqrt(ms + EPS) * gain_row


def _norm_matmul_kernel(x_ref, g_ref, w_ref, o_ref, xn_ref):
    @pl.when(pl.program_id(1) == 0)
    def _():
        xn_ref[...] = _rms_rows(x_ref[...].astype(F32), g_ref[...]).astype(BF16)

    o_ref[...] = jnp.dot(xn_ref[...], w_ref[...], preferred_element_type=F32).astype(o_ref.dtype)


def norm_matmul(x, gain, w, out_dtype, tm, tn):
    m, k = x.shape
    n = w.shape[1]
    tm = min(tm, m)
    tn = min(tn, n)
    return pl.pallas_call(
        _norm_matmul_kernel,
        grid=(m // tm, n // tn),
        in_specs=[pl.BlockSpec((tm, k), lambda i, j: (i, 0)),
                  pl.BlockSpec((1, k), lambda i, j: (0, 0)),
                  pl.BlockSpec((k, tn), lambda i, j: (0, j))],
        out_specs=pl.BlockSpec((tm, tn), lambda i, j: (i, j)),
        out_shape=jax.ShapeDtypeStruct((m, n), out_dtype),
        scratch_shapes=[pltpu.VMEM((tm, k), BF16)],
        compiler_params=_params(2),
        name="norm_matmul",
    )(x, gain.reshape(1, k), w)


def _matmul_residual_kernel(x_ref, a_ref, w_ref, o_ref):
    o_ref[...] = x_ref[...] + jnp.dot(a_ref[...], w_ref[...], preferred_element_type=F32)


def matmul_residual(x, a, w, tm):
    m, d = x.shape
    k = a.shape[1]
    tm = min(tm, m)
    return pl.pallas_call(
        _matmul_residual_kernel,
        grid=(m // tm,),
        in_specs=[pl.BlockSpec((tm, d), lambda i: (i, 0)),
                  pl.BlockSpec((tm, k), lambda i: (i, 0)),
                  pl.BlockSpec((k, d), lambda i: (0, 0))],
        out_specs=pl.BlockSpec((tm, d), lambda i: (i, 0)),
        out_shape=jax.ShapeDtypeStruct((m, d), F32),
        compiler_params=_params(1),
        name="matmul_residual",
    )(x, a, w)


def _mlp_kernel(x_ref, g_ref, wu_ref, wd_ref, o_ref, xn_ref, acc_ref):
    f = pl.program_id(1)

    @pl.when(f == 0)
    def _():
        xn_ref[...] = _rms_rows(x_ref[...], g_ref[...]).astype(BF16)
        acc_ref[...] = jnp.zeros_like(acc_ref)

    h = jnp.dot(xn_ref[...], wu_ref[...], preferred_element_type=F32)
    h = jnp.square(jnp.maximum(h, 0.0)).astype(BF16)
    acc_ref[...] += jnp.dot(h, wd_ref[...], preferred_element_type=F32)

    @pl.when(f == pl.num_programs(1) - 1)
    def _():
        o_ref[...] = x_ref[...] + acc_ref[...]


def mlp(x, gain, w_up, w_down, tm, tf):
    m, d = x.shape
    ff = w_up.shape[1]
    tm = min(tm, m)
    return pl.pallas_call(
        _mlp_kernel,
        grid=(m // tm, ff // tf),
        in_specs=[pl.BlockSpec((tm, d), lambda i, f: (i, 0)),
                  pl.BlockSpec((1, d), lambda i, f: (0, 0)),
                  pl.BlockSpec((d, tf), lambda i, f: (0, f)),
                  pl.BlockSpec((tf, d), lambda i, f: (f, 0))],
        out_specs=pl.BlockSpec((tm, d), lambda i, f: (i, 0)),
        out_shape=jax.ShapeDtypeStruct((m, d), F32),
        scratch_shapes=[pltpu.VMEM((tm, d), BF16), pltpu.VMEM((tm, d), F32)],
        compiler_params=_params(2),
        name="mlp",
    )(x, gain.reshape(1, d), w_up, w_down)


def _dsw_inproj_kernel(*refs, n_res):
    x_refs = refs[:n_res]
    g_ref, hn_ref, w_ref, o_ref, xn_ref = refs[n_res:]
    rows = x_refs[0].shape[0]
    width = w_ref.shape[2]
    for rr in range(n_res):
        xn_ref[rr * rows:(rr + 1) * rows, :] = _rms_rows(x_refs[rr][...], g_ref[...]).astype(BF16)
    for t in range(3):
        acc = jnp.dot(xn_ref[...], w_ref[t], preferred_element_type=F32)
        for h in range(width // HEAD_DIM):
            a = acc[:, h * HEAD_DIM:(h + 1) * HEAD_DIM]
            if t < 2:
                a = _rms_rows(a, hn_ref[t:t + 1, :])
            a = a.astype(o_ref.dtype)
            for rr in range(n_res):
                o_ref[rr, :, t * width + h * HEAD_DIM:t * width + (h + 1) * HEAD_DIM] = (
                    a[rr * rows:(rr + 1) * rows, :])


def dsw_inproj(x, gain, head_gains, w3, dil, tile_rows):
    b, t, d = x.shape
    width = w3.shape[2]
    sub = t // dil
    rows = min(tile_rows, sub)
    n_res = min(max(tile_rows // rows, 1), dil)
    xv = x.reshape(b, sub, dil * d)
    x_specs = [pl.BlockSpec((None, rows, d),
                            functools.partial(lambda bi, rb, i, rr: (bi, i, rb * n_res + rr), rr=rr))
               for rr in range(n_res)]
    return pl.pallas_call(
        functools.partial(_dsw_inproj_kernel, n_res=n_res),
        grid=(b, dil // n_res, sub // rows),
        in_specs=x_specs + [pl.BlockSpec((1, d), lambda bi, rb, i: (0, 0)),
                            pl.BlockSpec((2, HEAD_DIM), lambda bi, rb, i: (0, 0)),
                            pl.BlockSpec((3, d, width), lambda bi, rb, i: (0, 0, 0))],
        out_specs=pl.BlockSpec((None, n_res, rows, 3 * width), lambda bi, rb, i: (bi, rb, i, 0)),
        out_shape=jax.ShapeDtypeStruct((b, dil, sub, 3 * width), BF16),
        scratch_shapes=[pltpu.VMEM((n_res * rows, d), BF16)],
        compiler_params=_params(3),
        name="dsw_inproj",
    )(*([xv] * n_res), gain.reshape(1, d), head_gains, w3)


TRI_BASE = 8


def _pair_block_diag(m):
    n = m.shape[0]
    mb = m.astype(BF16)
    lane = lax.broadcasted_iota(jnp.int32, mb.shape, 1)
    zero = jnp.zeros_like(mb)
    return jnp.concatenate([jnp.where(lane < n, mb, zero), jnp.where(lane >= n, mb, zero)], axis=0)


def _pair_mm(u, w):
    return jnp.dot(u.astype(BF16), _pair_block_diag(w), preferred_element_type=F32)


def _tri_inverse_pairs(a_list):
    n = a_list[0].shape[0]
    ii = lax.broadcasted_iota(jnp.int32, (n, 2 * n), 0)
    jj = lax.broadcasted_iota(jnp.int32, (n, 2 * n), 1) % n
    same_block = lambda size: (ii // size) == (jj // size)
    ps = [jnp.where(same_block(TRI_BASE), a, 0.0) for a in a_list]
    xs = [jnp.where(ii == jj, 1.0, 0.0) - p for p in ps]
    for _ in range((TRI_BASE - 1).bit_length() - 1):
        ps = [_pair_mm(p, p) for p in ps]
        xs = [x + _pair_mm(x, p) for x, p in zip(xs, ps)]
    size = TRI_BASE
    while size < n:
        off = same_block(2 * size) & jnp.logical_not(same_block(size))
        ys = [_pair_mm(x, jnp.where(off, a, 0.0)) for x, a in zip(xs, a_list)]
        xs = [x - _pair_mm(y, x) for x, y in zip(xs, ys)]
        size *= 2
    return xs


def _gdn_prompt_kernel(qkv_ref, z_ref, gate_ref, cw_ref, alog_ref, dtb_ref, onorm_ref,
                       o_ref, s_out_ref,
                       xe_ref, qk_ref, v_ref, gate_scr, bb_ref, gcb_ref, tp_ref, pp_ref, s_ref, oscr_ref,
                       *, n_qk, n_v):
    tb = qkv_ref.shape[0]
    nchunk = tb // CHUNK
    t_idx = pl.program_id(1)
    halo = 8

    @pl.when(t_idx == 0)
    def _():
        xe_ref[:, halo - (CONV_TAPS - 1):halo, :] = jnp.zeros((xe_ref.shape[0], CONV_TAPS - 1, LANES), F32)
        s_ref[...] = jnp.zeros_like(s_ref)

    gate_scr[0] = _sigmoid(gate_ref[:, :LANES])
    g = -jnp.exp(alog_ref[...]) * _softplus(gate_ref[:, LANES:] + dtb_ref[...])
    pos = lax.broadcasted_iota(jnp.int32, (tb, LANES), 0) % CHUNK
    shift = 1
    while shift < CHUNK:
        g = g + jnp.where(pos >= shift, pltpu.roll(g, shift, axis=0), 0.0)
        shift *= 2
    gate_scr[1] = g
    for h in range(n_v):
        bb_ref[h] = jnp.broadcast_to(gate_scr[0, :, h:h + 1], (tb, LANES))
        gcb_ref[h] = jnp.broadcast_to(gate_scr[1, :, h:h + 1], (tb, LANES))

    for cb in range(n_qk * 2 + n_v):
        lo = cb * LANES
        xe_ref[cb, halo:halo + tb, :] = qkv_ref[:, lo:lo + LANES].astype(F32)
        y = cw_ref[0:1, lo:lo + LANES] * xe_ref[cb, halo - 3:halo - 3 + tb, :]
        for j in range(1, CONV_TAPS):
            y = y + cw_ref[j:j + 1, lo:lo + LANES] * xe_ref[cb, halo - 3 + j:halo - 3 + j + tb, :]
        y = _silu(y)
        xe_ref[cb, halo - 3:halo, :] = xe_ref[cb, halo + tb - 3:halo + tb, :]
        if cb < 2 * n_qk:
            y = y * lax.rsqrt(jnp.sum(y * y, axis=-1, keepdims=True) + EPS)
            if cb < n_qk:
                y = y * (HEAD_DIM ** -0.5)
            yb = y.astype(BF16)
            half = 0 if cb < n_qk else CHUNK
            for c in range(nchunk):
                qk_ref[cb % n_qk, c, half:half + CHUNK, :] = yb[c * CHUNK:(c + 1) * CHUNK, :]
        else:
            v_ref[cb - 2 * n_qk] = y

    ii = lax.broadcasted_iota(jnp.int32, (CHUNK, 2 * CHUNK), 0)
    lane = lax.broadcasted_iota(jnp.int32, (CHUNK, 2 * CHUNK), 1)
    jj = lane % CHUNK
    causal = ii >= jj
    strict = ii > jj
    pairs_per_iter = 4

    def inverse_body(it, carry):
        probs = [(it * pairs_per_iter + dj, c) for dj in range(pairs_per_iter) for c in range(nchunk)]
        grams, a_list, decays = [], [], []
        for j, c in probs:
            qk = qk_ref[j, c]
            k2 = jnp.concatenate([qk[CHUNK:], qk[CHUNK:]], axis=0)
            grams.append(lax.dot_general(qk, k2, (((1,), (1,)), ((), ())), preferred_element_type=F32))
        for (j, c), gram in zip(probs, grams):
            rows = slice(c * CHUNK, (c + 1) * CHUNK)
            gc = jnp.where(lane < CHUNK, gcb_ref[2 * j, rows, :], gcb_ref[2 * j + 1, rows, :])
            bt = jnp.where(lane < CHUNK, bb_ref[2 * j, rows, :], bb_ref[2 * j + 1, rows, :])
            gc_row = jnp.sum(jnp.where(ii == jj, gc, 0.0), axis=0, keepdims=True)
            decay = jnp.where(causal, jnp.exp(jnp.where(causal, gc - gc_row, 0.0)), 0.0)
            a_list.append(jnp.where(strict, bt * gram[CHUNK:] * decay, 0.0))
            pp_ref[j, c] = (gram[:CHUNK] * decay).astype(BF16)
        for (j, c), t_inv in zip(probs, _tri_inverse_pairs(a_list)):
            tp_ref[j, c] = t_inv.astype(BF16)
        return carry

    lax.fori_loop(0, n_qk // pairs_per_iter, inverse_body, 0)

    def block_diag2(m):
        mb = m.astype(BF16)
        zero = jnp.zeros((CHUNK, HEAD_DIM), BF16)
        return jnp.concatenate([jnp.concatenate([mb[:, :HEAD_DIM], zero], axis=1),
                                jnp.concatenate([zero, mb[:, HEAD_DIM:]], axis=1)], axis=0)

    def chunk_body(c, carry):
        rows = pl.ds(pl.multiple_of(c * CHUNK, CHUNK), CHUNK)
        heads = range(n_qk)
        pair = lambda ref, j: jnp.concatenate([ref[2 * j, rows, :], ref[2 * j + 1, rows, :]], axis=1)
        qks = [qk_ref[j, c] for j in heads]
        projs = [jnp.dot(qks[j], jnp.concatenate([s_ref[2 * j], s_ref[2 * j + 1]], axis=1).astype(BF16),
                         preferred_element_type=F32) for j in heads]
        gcs = [pair(gcb_ref, j) for j in heads]
        e_gcs = [jnp.exp(gc) for gc in gcs]
        rhss = [pair(bb_ref, j) * (pair(v_ref, j) - e_gcs[j] * projs[j][CHUNK:]) for j in heads]
        v_news = [jnp.dot(tp_ref[j, c], block_diag2(rhss[j]), preferred_element_type=F32) for j in heads]
        outs = [e_gcs[j] * projs[j][:CHUNK]
                + jnp.dot(pp_ref[j, c], block_diag2(v_news[j]), preferred_element_type=F32) for j in heads]
        g_lasts = [gc[CHUNK - 1:CHUNK, :] for gc in gcs]
        v_decs = [(jnp.exp(g_lasts[j] - gcs[j]) * v_news[j]).astype(BF16) for j in heads]
        d_states = [lax.dot_general(qks[j][CHUNK:], v_decs[j], (((0,), (0,)), ((), ())),
                                    preferred_element_type=F32) for j in heads]
        for j in heads:
            for e in range(2):
                lanes = slice(e * HEAD_DIM, (e + 1) * HEAD_DIM)
                oscr_ref[2 * j + e, rows, :] = outs[j][:, lanes]
                s_ref[2 * j + e] = jnp.exp(g_lasts[j][:, lanes]) * s_ref[2 * j + e] + d_states[j][:, lanes]
        return carry

    lax.fori_loop(0, nchunk, chunk_body, 0)

    for h in range(n_v):
        lo = h * HEAD_DIM
        o = _rms_rows(oscr_ref[h], onorm_ref[...])
        o_ref[:, lo:lo + HEAD_DIM] = (o * _silu(z_ref[:, lo:lo + HEAD_DIM].astype(F32))).astype(o_ref.dtype)

    @pl.when(t_idx == pl.num_programs(1) - 1)
    def _():
        s_out_ref[...] = s_ref[...]


def gdn_prompt(qkvz, gates, conv_w, a_log, dt_bias, o_norm, n_qk, n_v, tb):
    b, t, _ = qkvz.shape
    assert n_v == 2 * n_qk and 2 * CHUNK == LANES
    conv_dim = (2 * n_qk + n_v) * HEAD_DIM
    v_dim = n_v * HEAD_DIM
    tb = min(tb, t)
    nchunk = tb // CHUNK
    pad = lambda p: jnp.zeros((1, LANES), F32).at[0, :n_v].set(p.astype(F32))
    return pl.pallas_call(
        functools.partial(_gdn_prompt_kernel, n_qk=n_qk, n_v=n_v),
        grid=(b, t // tb),
        in_specs=[pl.BlockSpec((None, tb, conv_dim), lambda bi, ti: (bi, ti, 0)),
                  pl.BlockSpec((None, tb, v_dim), lambda bi, ti: (bi, ti, conv_dim // v_dim)),
                  pl.BlockSpec((None, tb, 2 * LANES), lambda bi, ti: (bi, ti, 0)),
                  pl.BlockSpec((CONV_TAPS, conv_dim), lambda bi, ti: (0, 0)),
                  pl.BlockSpec((1, LANES), lambda bi, ti: (0, 0)),
                  pl.BlockSpec((1, LANES), lambda bi, ti: (0, 0)),
                  pl.BlockSpec((1, HEAD_DIM), lambda bi, ti: (0, 0))],
        out_specs=[pl.BlockSpec((None, tb, v_dim), lambda bi, ti: (bi, ti, 0)),
                   pl.BlockSpec((None, n_v, HEAD_DIM, HEAD_DIM), lambda bi, ti: (bi, 0, 0, 0))],
        out_shape=[jax.ShapeDtypeStruct((b, t, v_dim), BF16),
                   jax.ShapeDtypeStruct((b, n_v, HEAD_DIM, HEAD_DIM), F32)],
        scratch_shapes=[pltpu.VMEM((2 * n_qk + n_v, tb + 8, LANES), F32),
                        pltpu.VMEM((n_qk, nchunk, 2 * CHUNK, HEAD_DIM), BF16),
                        pltpu.VMEM((n_v, tb, HEAD_DIM), F32),
                        pltpu.VMEM((2, tb, LANES), F32),
                        pltpu.VMEM((n_v, tb, LANES), F32),
                        pltpu.VMEM((n_v, tb, LANES), F32),
                        pltpu.VMEM((n_qk, nchunk, CHUNK, 2 * CHUNK), BF16),
                        pltpu.VMEM((n_qk, nchunk, CHUNK, 2 * CHUNK), BF16),
                        pltpu.VMEM((n_v, HEAD_DIM, HEAD_DIM), F32),
                        pltpu.VMEM((n_v, tb, HEAD_DIM), F32)],
        compiler_params=_params(2),
        name="gdn_prompt",
    )(qkvz, qkvz, gates, conv_w, pad(a_log), pad(dt_bias), o_norm.reshape(1, HEAD_DIM))


GDN_QK_HEADS = 8
GDN_V_HEADS = 16
GDN_CONV_DIM = (2 * GDN_QK_HEADS + GDN_V_HEADS) * HEAD_DIM
GDN_V_DIM = GDN_V_HEADS * HEAD_DIM


def prep_gdn_weights(w_in, w_out):
    d = w_in.shape[0]
    main = GDN_CONV_DIM + GDN_V_DIM
    w_main = w_in[:, :main].astype(BF16)
    w_gate = jnp.zeros((d, 2 * LANES), F32)
    w_gate = w_gate.at[:, :GDN_V_HEADS].set(w_in[:, main:main + GDN_V_HEADS])
    w_gate = w_gate.at[:, LANES:LANES + GDN_V_HEADS].set(w_in[:, main + GDN_V_HEADS:])
    return w_main, w_gate.astype(BF16), w_out.astype(BF16)


def gdn_layer_prompt(x, gain, weights, conv_w, a_log, dt_bias, o_norm):
    w_main, w_gate, w_out = weights
    b, t, d = x.shape
    x2 = x.reshape(b * t, d)
    qkvz = norm_matmul(x2, gain, w_main, BF16, 1024, 512).reshape(b, t, -1)
    gates = norm_matmul(x2, gain, w_gate, F32, 1024, 2 * LANES).reshape(b, t, -1)
    o, state = gdn_prompt(qkvz, gates, conv_w, a_log, dt_bias, o_norm, GDN_QK_HEADS, GDN_V_HEADS, 256)
    y = matmul_residual(x2, o.reshape(b * t, -1), w_out, 1024).reshape(b, t, d)
    conv_tail = qkvz[:, t - (CONV_TAPS - 1):, :GDN_CONV_DIM].astype(F32)
    return y, conv_tail, state


def _gdn_step_kernel(qkvz_ref, gate_ref, conv_ref, s_ref, cw_ref, alog_ref, dtb_ref, onorm_ref,
                     o_ref, convn_ref, sn_ref, *, n_qk, n_v):
    rep = n_v // n_qk
    conv_dim = (2 * n_qk + n_v) * HEAD_DIM
    x = qkvz_ref[:, :conv_dim]
    y = cw_ref[CONV_TAPS - 1:CONV_TAPS, :] * x
    for j in range(CONV_TAPS - 1):
        y = y + cw_ref[j:j + 1, :] * conv_ref[j:j + 1, :]
    y = _silu(y)
    convn_ref[0:CONV_TAPS - 2, :] = conv_ref[1:CONV_TAPS - 1, :]
    convn_ref[CONV_TAPS - 2:CONV_TAPS - 1, :] = x

    beta = _sigmoid(gate_ref[:, :LANES])
    g = -jnp.exp(alog_ref[...]) * _softplus(gate_ref[:, LANES:] + dtb_ref[...])
    eye = (lax.broadcasted_iota(jnp.int32, (HEAD_DIM, HEAD_DIM), 0)
           == lax.broadcasted_iota(jnp.int32, (HEAD_DIM, HEAD_DIM), 1))

    def column(row):
        return jnp.sum(jnp.where(eye, row, 0.0), axis=1, keepdims=True)

    def l2(row):
        return row * lax.rsqrt(jnp.sum(row * row, axis=-1, keepdims=True) + EPS)

    for j in range(n_qk):
        q_col = column(l2(y[:, j * HEAD_DIM:(j + 1) * HEAD_DIM]) * (HEAD_DIM ** -0.5))
        k_col = column(l2(y[:, (n_qk + j) * HEAD_DIM:(n_qk + j + 1) * HEAD_DIM]))
        for e in range(rep):
            h = rep * j + e
            lo = h * HEAD_DIM
            v = y[:, 2 * n_qk * HEAD_DIM + lo:2 * n_qk * HEAD_DIM + lo + HEAD_DIM]
            s = s_ref[h]
            e_g = jnp.exp(g[:, h:h + 1])
            k_s = jnp.sum(s * k_col, axis=0, keepdims=True)
            v_new = beta[:, h:h + 1] * (v - e_g * k_s)
            s_new = e_g * s + k_col * v_new
            sn_ref[h] = s_new
            o = _rms_rows(jnp.sum(s_new * q_col, axis=0, keepdims=True), onorm_ref[...])
            z = qkvz_ref[:, conv_dim + lo:conv_dim + lo + HEAD_DIM]
            o_ref[:, lo:lo + HEAD_DIM] = (o * _silu(z)).astype(o_ref.dtype)


def gdn_step(qkvz, gates, conv_state, state, conv_w, a_log, dt_bias, o_norm, n_qk, n_v):
    b = qkvz.shape[0]
    conv_dim = (2 * n_qk + n_v) * HEAD_DIM
    v_dim = n_v * HEAD_DIM
    pad = lambda p: jnp.zeros((1, LANES), F32).at[0, :n_v].set(p.astype(F32))
    row = lambda n: pl.BlockSpec((None, 1, n), lambda bi: (bi, 0, 0))
    const = lambda shape: pl.BlockSpec(shape, lambda bi: (0,) * len(shape))
    st = pl.BlockSpec((None, n_v, HEAD_DIM, HEAD_DIM), lambda bi: (bi, 0, 0, 0))
    cv = pl.BlockSpec((None, CONV_TAPS - 1, conv_dim), lambda bi: (bi, 0, 0))
    return pl.pallas_call(
        functools.partial(_gdn_step_kernel, n_qk=n_qk, n_v=n_v),
        grid=(b,),
        in_specs=[row(conv_dim + v_dim), row(2 * LANES), cv, st,
                  const((CONV_TAPS, conv_dim)), const((1, LANES)), const((1, LANES)), const((1, HEAD_DIM))],
        out_specs=[row(v_dim), cv, st],
        out_shape=[jax.ShapeDtypeStruct((b, 1, v_dim), BF16),
                   jax.ShapeDtypeStruct(conv_state.shape, F32),
                   jax.ShapeDtypeStruct(state.shape, F32)],
        compiler_params=_params(1),
        name="gdn_step",
    )(qkvz.reshape(b, 1, -1), gates.reshape(b, 1, -1), conv_state, state,
      conv_w, pad(a_log), pad(dt_bias), o_norm.reshape(1, HEAD_DIM))


def gdn_layer_sample(x, gain, weights, conv_w, a_log, dt_bias, o_norm, conv_state, state):
    w_main, w_gate, w_out = weights
    b = x.shape[0]
    qkvz = norm_matmul(x, gain, w_main, F32, b, 512)
    gates = norm_matmul(x, gain, w_gate, F32, b, 2 * LANES)
    o, conv_new, state_new = gdn_step(qkvz, gates, conv_state, state, conv_w, a_log, dt_bias, o_norm,
                                      GDN_QK_HEADS, GDN_V_HEADS)
    return matmul_residual(x, o.reshape(b, -1), w_out, b), conv_new, state_new


DSW_HEADS = 8
DSW_WIDTH = DSW_HEADS * HEAD_DIM
DSW_TILE = DSW_BLOCK * max(d for _, d in DSW_GROUPS)
DSW_BLOCKS_PER_ITER = 8


def _dsw_attn_kernel(*refs):
    n_groups = len(DSW_GROUPS)
    ins = [refs[5 * g:5 * g + 5] for g in range(n_groups)]
    o_ref = refs[5 * n_groups]
    scratch = refs[5 * n_groups + 1:]
    kf_refs, vf_refs = scratch[:n_groups], scratch[n_groups:2 * n_groups]
    og_ref, lg_ref = scratch[2 * n_groups:]
    first_tile = pl.program_id(2) == 0
    blk = DSW_BLOCK
    qi = lax.broadcasted_iota(jnp.int32, (blk, 2 * blk), 0)
    ki = lax.broadcasted_iota(jnp.int32, (blk, 2 * blk), 1)
    band = (ki >= qi) & (ki <= qi + blk)
    scale = HEAD_DIM ** -0.5

    for g, (_, dil) in enumerate(DSW_GROUPS):
        q_ref, kc_ref, vc_ref, kp_ref, vp_ref = ins[g]
        kf_ref, vf_ref = kf_refs[g], vf_refs[g]
        per_res = q_ref.shape[1] // blk
        kf_ref[:, :blk, :] = kp_ref[...]
        kf_ref[:, blk:, :] = kc_ref[...]
        vf_ref[:, :blk, :] = vp_ref[...]
        vf_ref[:, blk:, :] = vc_ref[...]

        def blocks_body(it, carry, *, g=g, dil=dil, per_res=per_res, q_ref=q_ref, kf_ref=kf_ref, vf_ref=vf_ref):
            ids = [it * DSW_BLOCKS_PER_ITER + i for i in range(DSW_BLOCKS_PER_ITER)]
            rs = [bi // per_res for bi in ids]
            ms = [bi % per_res for bi in ids]
            row0s = [pl.multiple_of(m * blk, blk) for m in ms]
            ss = [lax.dot_general(q_ref[r, pl.ds(row0, blk), :], kf_ref[r, pl.ds(row0, 2 * blk), :],
                                  (((1,), (1,)), ((), ())), preferred_element_type=F32) * scale
                  for r, row0 in zip(rs, row0s)]
            ss = [jnp.where(band & ((ki >= blk) | (m > 0) | jnp.logical_not(first_tile)), s, NEG_INF)
                  for s, m in zip(ss, ms)]
            mxs = [jnp.max(s, axis=-1, keepdims=True) for s in ss]
            ps = [jnp.exp(s - mx) for s, mx in zip(ss, mxs)]
            dens = [jnp.sum(p, axis=-1, keepdims=True) for p in ps]
            os_ = [jnp.dot(p.astype(BF16), vf_ref[r, pl.ds(row0, 2 * blk), :], preferred_element_type=F32)
                   for p, r, row0 in zip(ps, rs, row0s)]
            for o, den, mx, r, row0 in zip(os_, dens, mxs, rs, row0s):
                o = o * (1.0 / den)
                lse = jnp.broadcast_to(mx + jnp.log(den), (blk, LANES))
                start = row0 * dil + r
                rows = pl.ds(start, blk) if dil == 1 else pl.ds(start, blk, stride=dil)
                og_ref[g, rows, :] = o
                lg_ref[g, rows, :] = lse
            return carry

        lax.fori_loop(0, dil * per_res // DSW_BLOCKS_PER_ITER, blocks_body, 0)

    top = lg_ref[0]
    for g in range(1, n_groups):
        top = jnp.maximum(top, lg_ref[g])
    num = jnp.zeros_like(top)
    den = jnp.zeros_like(top)
    for g in range(n_groups):
        w = jnp.exp(lg_ref[g] - top)
        num = num + w * og_ref[g]
        den = den + w
    o_ref[...] = (num * (1.0 / den)).astype(o_ref.dtype)


def dsw_attn(projs):
    b = projs[0].shape[0]
    t = projs[0].shape[1] * projs[0].shape[2]
    tile = DSW_TILE
    heads = DSW_HEADS
    in_specs, args, kv_scratch = [], [], []
    for (_, dil), p in zip(DSW_GROUPS, projs):
        rows = tile // dil
        per_res = rows // DSW_BLOCK
        cur = lambda col: pl.BlockSpec((None, dil, rows, HEAD_DIM),
                                       functools.partial(lambda bi, h, n, col: (bi, 0, n, col * heads + h), col=col))
        prev = lambda col: pl.BlockSpec(
            (None, dil, DSW_BLOCK, HEAD_DIM),
            functools.partial(lambda bi, h, n, col, per_res: (bi, 0, jnp.maximum(n * per_res - 1, 0), col * heads + h),
                              col=col, per_res=per_res))
        in_specs += [cur(0), cur(1), cur(2), prev(1), prev(2)]
        args += [p] * 5
        kv_scratch.append(pltpu.VMEM((dil, DSW_BLOCK + rows, HEAD_DIM), BF16))
    n_groups = len(DSW_GROUPS)
    return pl.pallas_call(
        _dsw_attn_kernel,
        grid=(b, heads, t // tile),
        in_specs=in_specs,
        out_specs=pl.BlockSpec((None, tile, HEAD_DIM), lambda bi, h, n: (bi, n, h)),
        out_shape=jax.ShapeDtypeStruct((b, t, heads * HEAD_DIM), BF16),
        scratch_shapes=kv_scratch + kv_scratch + [pltpu.VMEM((n_groups, tile, HEAD_DIM), F32),
                                                  pltpu.VMEM((n_groups, tile, LANES), F32)],
        compiler_params=_params(3),
        name="dsw_attn",
    )(*args)


def prep_dsw_weights(w_in, w_out):
    d = w_in.shape[0]
    w = w_in.reshape(d, len(DSW_GROUPS), 3, DSW_WIDTH).transpose(1, 2, 0, 3).astype(BF16)
    return [w[g] for g in range(len(DSW_GROUPS))], w_out.astype(BF16)


def dsw_layer_prompt(x, gain, weights, q_norm, k_norm):
    w_groups, w_out = weights
    b, t, d = x.shape
    projs = [dsw_inproj(x, gain, jnp.stack([q_norm[g], k_norm[g]]), w_groups[g], dil, 512)
             for g, (_, dil) in enumerate(DSW_GROUPS)]
    o = dsw_attn(projs)
    y = matmul_residual(x.reshape(b * t, d), o.reshape(b * t, -1), w_out, 1024).reshape(b, t, d)
    return y, projs


def _dsw_cache_kernel(*refs):
    *p_refs, o_ref = refs
    layer = pl.program_id(0)
    for li, p_ref in enumerate(p_refs):
        @pl.when(layer == li)
        def _(p_ref=p_ref):
            o_ref[...] = p_ref[...].astype(o_ref.dtype).reshape(o_ref.shape)


def dsw_prompt_caches(projs_layers):
    n_layers = len(projs_layers)
    outs = []
    for g, (window, dil) in enumerate(DSW_GROUPS):
        ps = [pl_[g] for pl_ in projs_layers]
        b, _, sub, _ = ps[0].shape
        last = sub // DSW_BLOCK - 1

        def in_map(l, bi, r, kv, *, li):
            before, after = l < li, l > li
            pick = lambda lo, x, hi: jnp.where(before, lo, jnp.where(after, hi, x))
            return (pick(0, bi, b - 1), pick(0, r, dil - 1), last, 1 + pick(0, kv, 1))

        out = pl.pallas_call(
            _dsw_cache_kernel,
            grid=(n_layers, b, dil, 2),
            in_specs=[pl.BlockSpec((None, None, DSW_BLOCK, DSW_WIDTH), functools.partial(in_map, li=li))
                      for li in range(n_layers)],
            out_specs=pl.BlockSpec((None, None, DSW_BLOCK, None, DSW_HEADS, HEAD_DIM),
                                   lambda l, bi, r, kv: (l, bi, 0, r * 2 + kv, 0, 0)),
            out_shape=jax.ShapeDtypeStruct((n_layers, b, DSW_BLOCK, dil * 2, DSW_HEADS, HEAD_DIM), F32),
            compiler_params=_params(4),
            name="dsw_prompt_cache",
        )(*ps)
        outs.append(out.reshape(n_layers, b, window, 2, DSW_HEADS, HEAD_DIM))
    return outs


def _dsw_decode_kernel(p_ref, *refs):
    n_groups = len(DSW_GROUPS)
    c_refs = refs[:n_groups]
    o_ref, new_ref = refs[n_groups:]
    scale = HEAD_DIM ** -0.5
    outs, lses = [], []
    for g in range(n_groups):
        q = p_ref[g, 0].astype(F32)
        k_new = p_ref[g, 1].astype(F32)
        v_new = p_ref[g, 2].astype(F32)
        new_ref[g, 0] = k_new
        new_ref[g, 1] = v_new
        s = jnp.sum(c_refs[g][:, 0] * q[None], axis=-1, keepdims=True) * scale
        s_new = jnp.sum(k_new * q, axis=-1, keepdims=True) * scale
        mx = jnp.maximum(jnp.max(s, axis=0), s_new)
        p = jnp.exp(s - mx[None])
        p_new = jnp.exp(s_new - mx)
        den = jnp.sum(p, axis=0) + p_new
        outs.append((jnp.sum(p * c_refs[g][:, 1], axis=0) + p_new * v_new) * (1.0 / den))
        lses.append(mx + jnp.log(den))
    top = functools.reduce(jnp.maximum, lses)
    ws = [jnp.exp(l - top) for l in lses]
    num = sum(w * o for w, o in zip(ws, outs))
    o_ref[...] = (num * (1.0 / sum(ws))).astype(o_ref.dtype)


def dsw_decode(proj, caches, layer):
    b = proj.shape[0]
    n_groups = len(DSW_GROUPS)
    c_specs, c_args = [], []
    for (window, dil), c in zip(DSW_GROUPS, caches):
        c_args.append(c.reshape(c.shape[0], b, window // dil, dil, 2, DSW_HEADS, HEAD_DIM))
        c_specs.append(pl.BlockSpec((None, None, window // dil, None, 2, DSW_HEADS, HEAD_DIM),
                                    lambda bi: (layer, bi, 0, 0, 0, 0, 0)))
    return pl.pallas_call(
        _dsw_decode_kernel,
        grid=(b,),
        in_specs=[pl.BlockSpec((None, n_groups, 3, DSW_HEADS, HEAD_DIM), lambda bi: (bi, 0, 0, 0, 0))] + c_specs,
        out_specs=[pl.BlockSpec((None, DSW_HEADS, HEAD_DIM), lambda bi: (bi, 0, 0)),
                   pl.BlockSpec((None, n_groups, 2, DSW_HEADS, HEAD_DIM), lambda bi: (bi, 0, 0, 0, 0))],
        out_shape=[jax.ShapeDtypeStruct((b, DSW_HEADS, HEAD_DIM), BF16),
                   jax.ShapeDtypeStruct((b, n_groups, 2, DSW_HEADS, HEAD_DIM), F32)],
        compiler_params=_params(1),
        name="dsw_decode",
    )(proj, *c_args)


def dsw_layer_sample(x, gain, weights, q_norm, k_norm, caches, layer):
    w_groups, w_out = weights
    b, d = x.shape
    proj = jnp.stack([dsw_inproj(x.reshape(1, b, d), gain, jnp.stack([q_norm[g], k_norm[g]]), w_groups[g], 1, b)
                      .reshape(b, 3, DSW_HEADS, HEAD_DIM) for g in range(len(DSW_GROUPS))], axis=1)
    o, new_rows = dsw_decode(proj, caches, layer)
    return matmul_residual(x, o.reshape(b, -1), w_out, b), new_rows


SHIFT_ROWS = 256


def _shift_kernel(c_ref, next_ref, new_ref, o_ref):
    rows = c_ref.shape[0]
    is_last = pl.program_id(2) == pl.num_programs(2) - 1
    o_ref[0:rows - 1] = c_ref[1:rows]

    @pl.when(is_last)
    def _():
        o_ref[rows - 1] = new_ref[...]

    @pl.when(jnp.logical_not(is_last))
    def _():
        o_ref[rows - 1] = next_ref[0]


def shift_caches(caches, new_rows_layers):
    new = jnp.stack(new_rows_layers)
    outs = []
    for g, c in enumerate(caches):
        n_layers, b, window = c.shape[:3]
        rows = min(SHIFT_ROWS, window)
        tile = (2, DSW_HEADS, HEAD_DIM)
        blk = pl.BlockSpec((None, None, rows) + tile, lambda l, bi, i: (l, bi, i, 0, 0, 0))
        nxt = pl.BlockSpec((None, None, 1) + tile,
                           lambda l, bi, i: (l, bi, jnp.minimum((i + 1) * rows, window - 1), 0, 0, 0))
        outs.append(pl.pallas_call(
            _shift_kernel,
            grid=(n_layers, b, window // rows),
            in_specs=[blk, nxt,
                      pl.BlockSpec((None, None, None) + tile, functools.partial(lambda l, bi, i, g: (l, bi, g, 0, 0, 0), g=g))],
            out_specs=blk,
            out_shape=jax.ShapeDtypeStruct(c.shape, c.dtype),
            compiler_params=_params(3),
            name="shift_cache",
        )(c, c, new))
    return outs


def kernel(x_prompt, x_sample, state_gdn, state_conv, cache_kv_w128, cache_kv_w512, cache_kv_w2048,
           norm_mix, norm_mlp, gdn_w_in, gdn_conv_w, gdn_a_log, gdn_dt_bias, gdn_o_norm, gdn_w_out,
           dsw_w_in, dsw_q_norm, dsw_k_norm, dsw_w_out, mlp_w_up, mlp_w_down):
    b, t, d = x_prompt.shape
    bs = x_sample.shape[0]
    depth = norm_mix.shape[0]
    caches = [cache_kv_w128, cache_kv_w512, cache_kv_w2048]
    yp = x_prompt
    ys = x_sample.reshape(bs, d)
    p_gdn, p_conv, s_gdn, s_conv, p_projs, s_rows = [], [], [], [], [], []
    for i in range(depth):
        j = i // 2
        if i % 2 == 0:
            wts = prep_gdn_weights(gdn_w_in[j], gdn_w_out[j])
            par = (gdn_conv_w[j], gdn_a_log[j], gdn_dt_bias[j], gdn_o_norm[j])
            yp, conv_p, state_p = gdn_layer_prompt(yp, norm_mix[i], wts, *par)
            ys, conv_s, state_s = gdn_layer_sample(ys, norm_mix[i], wts, *par, state_conv[j], state_gdn[j])
            p_gdn.append(state_p)
            p_conv.append(conv_p)
            s_gdn.append(state_s)
            s_conv.append(conv_s)
        else:
            wts = prep_dsw_weights(dsw_w_in[j], dsw_w_out[j])
            yp, projs = dsw_layer_prompt(yp, norm_mix[i], wts, dsw_q_norm[j], dsw_k_norm[j])
            ys, rows = dsw_layer_sample(ys, norm_mix[i], wts, dsw_q_norm[j], dsw_k_norm[j], caches, j)
            p_projs.append(projs)
            s_rows.append(rows)
        w_up, w_down = mlp_w_up[i].astype(BF16), mlp_w_down[i].astype(BF16)
        yp = mlp(yp.reshape(b * t, d), norm_mlp[i], w_up, w_down, 1024, 512).reshape(b, t, d)
        ys = mlp(ys, norm_mlp[i], w_up, w_down, bs, 512)
    p_kv = dsw_prompt_caches(p_projs)
    s_kv = shift_caches(caches, s_rows)
    return (yp, ys.reshape(x_sample.shape),
            jnp.stack(p_gdn), jnp.stack(p_conv), p_kv[0], p_kv[1], p_kv[2],
            jnp.stack(s_gdn), jnp.stack(s_conv), s_kv[0], s_kv[1], s_kv[2])
```

```python
import functools

import jax
import jax.numpy as jnp
from jax import lax
from jax.experimental import pallas as pl
from jax.experimental.pallas import tpu as pltpu

F32 = jnp.float32
BF16 = jnp.bfloat16

EPS = 1e-6
NEG_INF = -1e30
LANES = 128
CONV_TAPS = 4
CHUNK = 64
HEAD_DIM = 128
DSW_BLOCK = 128
DSW_GROUPS = ((128, 1), (512, 4), (2048, 16))
VMEM_LIMIT = 56 * 1024 * 1024


def _params(n_axes, vmem=VMEM_LIMIT):
    return pltpu.CompilerParams(dimension_semantics=("arbitrary",) * n_axes,
                                vmem_limit_bytes=vmem)


def _sigmoid(x):
    return 0.5 + 0.5 * jnp.tanh(0.5 * x)


def _silu(x):
    h = 0.5 * x
    return h + h * jnp.tanh(h)


def _softplus(x):
    return jnp.maximum(x, 0.0) + jnp.log(1.0 + jnp.exp(-jnp.abs(x)))


def _rms_rows(x, gain_row):
    ms = jnp.mean(x * x, axis=-1, keepdims=True)
    return x * lax.rsqrt(ms + EPS) * gain_row


INPROJ_COLS = 512


def _gdn_inproj_kernel(x_ref, g_ref, w_ref, wg_ref, o_ref, og_ref, xn_ref):
    xn_ref[...] = _rms_rows(x_ref[...], g_ref[...]).astype(BF16)
    for lo in range(0, w_ref.shape[1], INPROJ_COLS):
        o_ref[:, lo:lo + INPROJ_COLS] = jnp.dot(
            xn_ref[...], w_ref[:, lo:lo + INPROJ_COLS], preferred_element_type=F32).astype(o_ref.dtype)
    og_ref[...] = jnp.dot(xn_ref[...], wg_ref[...], preferred_element_type=F32)


def gdn_inproj(x, gain, w_main, w_gate, out_dtype, tm):
    m, k = x.shape
    n, ng = w_main.shape[1], w_gate.shape[1]
    tm = min(tm, m)
    return pl.pallas_call(
        _gdn_inproj_kernel,
        grid=(m // tm,),
        in_specs=[pl.BlockSpec((tm, k), lambda i: (i, 0)),
                  pl.BlockSpec((1, k), lambda i: (0, 0)),
                  pl.BlockSpec((k, n), lambda i: (0, 0)),
                  pl.BlockSpec((k, ng), lambda i: (0, 0))],
        out_specs=[pl.BlockSpec((tm, n), lambda i: (i, 0)),
                   pl.BlockSpec((tm, ng), lambda i: (i, 0))],
        out_shape=[jax.ShapeDtypeStruct((m, n), out_dtype), jax.ShapeDtypeStruct((m, ng), F32)],
        scratch_shapes=[pltpu.VMEM((tm, k), BF16)],
        compiler_params=_params(1),
        name="gdn_inproj",
    )(x, gain.reshape(1, k), w_main, w_gate)


def _mixer_mlp_kernel(x_ref, a_ref, wo_ref, g_ref, wu_ref, wd_ref, o_ref, xn_ref, acc_ref):
    f = pl.program_id(1)

    @pl.when(f == 0)
    def _():
        x1 = x_ref[...] + jnp.dot(a_ref[...], wo_ref[...], preferred_element_type=F32)
        xn_ref[...] = _rms_rows(x1, g_ref[...]).astype(BF16)
        acc_ref[...] = x1

    h = jnp.dot(xn_ref[...], wu_ref[...], preferred_element_type=F32)
    h = jnp.square(jnp.maximum(h, 0.0)).astype(BF16)
    acc_ref[...] += jnp.dot(h, wd_ref[...], preferred_element_type=F32)

    @pl.when(f == pl.num_programs(1) - 1)
    def _():
        o_ref[...] = acc_ref[...]


def mixer_mlp(x, a, w_out, gain, w_up, w_down, tm, tf):
    m, d = x.shape
    ka = a.shape[1]
    ff = w_up.shape[1]
    tm = min(tm, m)
    return pl.pallas_call(
        _mixer_mlp_kernel,
        grid=(m // tm, ff // tf),
        in_specs=[pl.BlockSpec((tm, d), lambda i, f: (i, 0)),
                  pl.BlockSpec((tm, ka), lambda i, f: (i, 0)),
                  pl.BlockSpec((ka, d), lambda i, f: (0, 0)),
                  pl.BlockSpec((1, d), lambda i, f: (0, 0)),
                  pl.BlockSpec((d, tf), lambda i, f: (0, f)),
                  pl.BlockSpec((tf, d), lambda i, f: (f, 0))],
        out_specs=pl.BlockSpec((tm, d), lambda i, f: (i, 0)),
        out_shape=jax.ShapeDtypeStruct((m, d), F32),
        scratch_shapes=[pltpu.VMEM((tm, d), BF16), pltpu.VMEM((tm, d), F32)],
        compiler_params=_params(2),
        name="mixer_mlp",
    )(x, a, w_out, gain.reshape(1, d), w_up, w_down)


def _dsw_inproj_kernel(x_ref, g_ref, hn_ref, w_ref, o_ref, xn_ref, head_scr, *, dil):
    rows = x_ref.shape[0]
    sub = rows // dil
    width = w_ref.shape[2]
    xn_ref[...] = _rms_rows(x_ref[...], g_ref[...]).astype(BF16)
    for t in range(3):
        acc = jnp.dot(xn_ref[...], w_ref[t], preferred_element_type=F32)
        for h in range(width // HEAD_DIM):
            a = acc[:, h * HEAD_DIM:(h + 1) * HEAD_DIM]
            if t < 2:
                a = _rms_rows(a, hn_ref[t:t + 1, :])
            cols = slice(t * width + h * HEAD_DIM, t * width + (h + 1) * HEAD_DIM)
            if dil == 1:
                o_ref[0, :, cols] = a.astype(o_ref.dtype)
            else:
                head_scr[h] = a
                for r in range(dil):
                    o_ref[r, :, cols] = head_scr[h, pl.ds(r, sub, stride=dil), :].astype(o_ref.dtype)


def dsw_inproj(x, gain, head_gains, w3, dil, tile_rows):
    b, t, d = x.shape
    width = w3.shape[2]
    rows = min(tile_rows, t)
    return pl.pallas_call(
        functools.partial(_dsw_inproj_kernel, dil=dil),
        grid=(b, t // rows),
        in_specs=[pl.BlockSpec((None, rows, d), lambda bi, i: (bi, i, 0)),
                  pl.BlockSpec((1, d), lambda bi, i: (0, 0)),
                  pl.BlockSpec((2, HEAD_DIM), lambda bi, i: (0, 0)),
                  pl.BlockSpec((3, d, width), lambda bi, i: (0, 0, 0))],
        out_specs=pl.BlockSpec((None, dil, rows // dil, 3 * width), lambda bi, i: (bi, 0, i, 0)),
        out_shape=jax.ShapeDtypeStruct((b, dil, t // dil, 3 * width), BF16),
        scratch_shapes=[pltpu.VMEM((rows, d), BF16),
                        pltpu.VMEM((width // HEAD_DIM, rows, HEAD_DIM), F32)],
        compiler_params=_params(2),
        name="dsw_inproj",
    )(x, gain.reshape(1, d), head_gains, w3)


TRI_BASE = 8


def _pair_block_diag(m):
    n = m.shape[0]
    mb = m.astype(BF16)
    lane = lax.broadcasted_iota(jnp.int32, mb.shape, 1)
    zero = jnp.zeros_like(mb)
    return jnp.concatenate([jnp.where(lane < n, mb, zero), jnp.where(lane >= n, mb, zero)], axis=0)


def _pair_mm(u, w):
    return jnp.dot(u.astype(BF16), _pair_block_diag(w), preferred_element_type=F32)


def _tri_inverse_pairs(a_list):
    n = a_list[0].shape[0]
    ii = lax.broadcasted_iota(jnp.int32, (n, 2 * n), 0)
    jj = lax.broadcasted_iota(jnp.int32, (n, 2 * n), 1) % n
    same_block = lambda size: (ii // size) == (jj // size)
    ps = [jnp.where(same_block(TRI_BASE), a, 0.0) for a in a_list]
    xs = [jnp.where(ii == jj, 1.0, 0.0) - p for p in ps]
    for _ in range((TRI_BASE - 1).bit_length() - 1):
        ps = [_pair_mm(p, p) for p in ps]
        xs = [x + _pair_mm(x, p) for x, p in zip(xs, ps)]
    size = TRI_BASE
    while size < n:
        off = same_block(2 * size) & jnp.logical_not(same_block(size))
        ys = [_pair_mm(x, jnp.where(off, a, 0.0)) for x, a in zip(xs, a_list)]
        xs = [x - _pair_mm(y, x) for x, y in zip(xs, ys)]
        size *= 2
    return xs


def _gdn_prompt_kernel(qkv_ref, z_ref, gate_ref, cw_ref, alog_ref, dtb_ref, onorm_ref,
                       o_ref, s_out_ref,
                       xe_ref, qk_ref, v_ref, gate_scr, bb_ref, gcb_ref, tp_ref, pp_ref, s_ref, oscr_ref,
                       *, n_qk, n_v):
    tb = qkv_ref.shape[0]
    nchunk = tb // CHUNK
    t_idx = pl.program_id(1)
    halo = 8

    @pl.when(t_idx == 0)
    def _():
        xe_ref[:, halo - (CONV_TAPS - 1):halo, :] = jnp.zeros((xe_ref.shape[0], CONV_TAPS - 1, LANES), F32)
        s_ref[...] = jnp.zeros_like(s_ref)

    gate_scr[0] = _sigmoid(gate_ref[:, :LANES])
    g = -jnp.exp(alog_ref[...]) * _softplus(gate_ref[:, LANES:] + dtb_ref[...])
    pos = lax.broadcasted_iota(jnp.int32, (tb, LANES), 0) % CHUNK
    shift = 1
    while shift < CHUNK:
        g = g + jnp.where(pos >= shift, pltpu.roll(g, shift, axis=0), 0.0)
        shift *= 2
    gate_scr[1] = g
    for h in range(n_v):
        bb_ref[h] = jnp.broadcast_to(gate_scr[0, :, h:h + 1], (tb, LANES))
        gcb_ref[h] = jnp.broadcast_to(gate_scr[1, :, h:h + 1], (tb, LANES))

    for cb in range(n_qk * 2 + n_v):
        lo = cb * LANES
        xe_ref[cb, halo:halo + tb, :] = qkv_ref[:, lo:lo + LANES].astype(F32)
        y = cw_ref[0:1, lo:lo + LANES] * xe_ref[cb, halo - 3:halo - 3 + tb, :]
        for j in range(1, CONV_TAPS):
            y = y + cw_ref[j:j + 1, lo:lo + LANES] * xe_ref[cb, halo - 3 + j:halo - 3 + j + tb, :]
        y = _silu(y)
        xe_ref[cb, halo - 3:halo, :] = xe_ref[cb, halo + tb - 3:halo + tb, :]
        if cb < 2 * n_qk:
            y = y * lax.rsqrt(jnp.sum(y * y, axis=-1, keepdims=True) + EPS)
            if cb < n_qk:
                y = y * (HEAD_DIM ** -0.5)
            yb = y.astype(BF16)
            half = 0 if cb < n_qk else CHUNK
            for c in range(nchunk):
                qk_ref[cb % n_qk, c, half:half + CHUNK, :] = yb[c * CHUNK:(c + 1) * CHUNK, :]
        else:
            v_ref[cb - 2 * n_qk] = y

    ii = lax.broadcasted_iota(jnp.int32, (CHUNK, 2 * CHUNK), 0)
    lane = lax.broadcasted_iota(jnp.int32, (CHUNK, 2 * CHUNK), 1)
    jj = lane % CHUNK
    causal = ii >= jj
    strict = ii > jj
    pairs_per_iter = 4

    def inverse_body(it, carry):
        probs = [(it * pairs_per_iter + dj, c) for dj in range(pairs_per_iter) for c in range(nchunk)]
        grams, a_list, decays = [], [], []
        for j, c in probs:
            qk = qk_ref[j, c]
            k2 = jnp.concatenate([qk[CHUNK:], qk[CHUNK:]], axis=0)
            grams.append(lax.dot_general(qk, k2, (((1,), (1,)), ((), ())), preferred_element_type=F32))
        for (j, c), gram in zip(probs, grams):
            rows = slice(c * CHUNK, (c + 1) * CHUNK)
            gc = jnp.where(lane < CHUNK, gcb_ref[2 * j, rows, :], gcb_ref[2 * j + 1, rows, :])
            bt = jnp.where(lane < CHUNK, bb_ref[2 * j, rows, :], bb_ref[2 * j + 1, rows, :])
            gc_row = jnp.sum(jnp.where(ii == jj, gc, 0.0), axis=0, keepdims=True)
            decay = jnp.where(causal, jnp.exp(jnp.where(causal, gc - gc_row, 0.0)), 0.0)
            a_list.append(jnp.where(strict, bt * gram[CHUNK:] * decay, 0.0))
            pp_ref[j, c] = (gram[:CHUNK] * decay).astype(BF16)
        for (j, c), t_inv in zip(probs, _tri_inverse_pairs(a_list)):
            tp_ref[j, c] = t_inv.astype(BF16)
        return carry

    lax.fori_loop(0, n_qk // pairs_per_iter, inverse_body, 0)

    def block_diag2(m):
        mb = m.astype(BF16)
        zero = jnp.zeros((CHUNK, HEAD_DIM), BF16)
        return jnp.concatenate([jnp.concatenate([mb[:, :HEAD_DIM], zero], axis=1),
                                jnp.concatenate([zero, mb[:, HEAD_DIM:]], axis=1)], axis=0)

    def chunk_body(c, carry):
        rows = pl.ds(pl.multiple_of(c * CHUNK, CHUNK), CHUNK)
        heads = range(n_qk)
        pair = lambda ref, j: jnp.concatenate([ref[2 * j, rows, :], ref[2 * j + 1, rows, :]], axis=1)
        qks = [qk_ref[j, c] for j in heads]
        projs = [jnp.dot(qks[j], jnp.concatenate([s_ref[2 * j], s_ref[2 * j + 1]], axis=1).astype(BF16),
                         preferred_element_type=F32) for j in heads]
        gcs = [pair(gcb_ref, j) for j in heads]
        e_gcs = [jnp.exp(gc) for gc in gcs]
        rhss = [pair(bb_ref, j) * (pair(v_ref, j) - e_gcs[j] * projs[j][CHUNK:]) for j in heads]
        v_news = [jnp.dot(tp_ref[j, c], block_diag2(rhss[j]), preferred_element_type=F32) for j in heads]
        outs = [e_gcs[j] * projs[j][:CHUNK]
                + jnp.dot(pp_ref[j, c], block_diag2(v_news[j]), preferred_element_type=F32) for j in heads]
        g_lasts = [gc[CHUNK - 1:CHUNK, :] for gc in gcs]
        v_decs = [(jnp.exp(g_lasts[j] - gcs[j]) * v_news[j]).astype(BF16) for j in heads]
        d_states = [lax.dot_general(qks[j][CHUNK:], v_decs[j], (((0,), (0,)), ((), ())),
                                    preferred_element_type=F32) for j in heads]
        for j in heads:
            for e in range(2):
                lanes = slice(e * HEAD_DIM, (e + 1) * HEAD_DIM)
                oscr_ref[2 * j + e, rows, :] = outs[j][:, lanes]
                s_ref[2 * j + e] = jnp.exp(g_lasts[j][:, lanes]) * s_ref[2 * j + e] + d_states[j][:, lanes]
        return carry

    lax.fori_loop(0, nchunk, chunk_body, 0)

    for h in range(n_v):
        lo = h * HEAD_DIM
        o = _rms_rows(oscr_ref[h], onorm_ref[...])
        o_ref[:, lo:lo + HEAD_DIM] = (o * _silu(z_ref[:, lo:lo + HEAD_DIM].astype(F32))).astype(o_ref.dtype)

    @pl.when(t_idx == pl.num_programs(1) - 1)
    def _():
        s_out_ref[...] = s_ref[...]


def gdn_prompt(qkvz, gates, conv_w, a_log, dt_bias, o_norm, n_qk, n_v, tb):
    b, t, _ = qkvz.shape
    assert n_v == 2 * n_qk and 2 * CHUNK == LANES
    conv_dim = (2 * n_qk + n_v) * HEAD_DIM
    v_dim = n_v * HEAD_DIM
    tb = min(tb, t)
    nchunk = tb // CHUNK
    pad = lambda p: jnp.zeros((1, LANES), F32).at[0, :n_v].set(p.astype(F32))
    return pl.pallas_call(
        functools.partial(_gdn_prompt_kernel, n_qk=n_qk, n_v=n_v),
        grid=(b, t // tb),
        in_specs=[pl.BlockSpec((None, tb, conv_dim), lambda bi, ti: (bi, ti, 0)),
                  pl.BlockSpec((None, tb, v_dim), lambda bi, ti: (bi, ti, conv_dim // v_dim)),
                  pl.BlockSpec((None, tb, 2 * LANES), lambda bi, ti: (bi, ti, 0)),
                  pl.BlockSpec((CONV_TAPS, conv_dim), lambda bi, ti: (0, 0)),
                  pl.BlockSpec((1, LANES), lambda bi, ti: (0, 0)),
                  pl.BlockSpec((1, LANES), lambda bi, ti: (0, 0)),
                  pl.BlockSpec((1, HEAD_DIM), lambda bi, ti: (0, 0))],
        out_specs=[pl.BlockSpec((None, tb, v_dim), lambda bi, ti: (bi, ti, 0)),
                   pl.BlockSpec((None, n_v, HEAD_DIM, HEAD_DIM), lambda bi, ti: (bi, 0, 0, 0))],
        out_shape=[jax.ShapeDtypeStruct((b, t, v_dim), BF16),
                   jax.ShapeDtypeStruct((b, n_v, HEAD_DIM, HEAD_DIM), F32)],
        scratch_shapes=[pltpu.VMEM((2 * n_qk + n_v, tb + 8, LANES), F32),
                        pltpu.VMEM((n_qk, nchunk, 2 * CHUNK, HEAD_DIM), BF16),
                        pltpu.VMEM((n_v, tb, HEAD_DIM), F32),
                        pltpu.VMEM((2, tb, LANES), F32),
                        pltpu.VMEM((n_v, tb, LANES), F32),
                        pltpu.VMEM((n_v, tb, LANES), F32),
                        pltpu.VMEM((n_qk, nchunk, CHUNK, 2 * CHUNK), BF16),
                        pltpu.VMEM((n_qk, nchunk, CHUNK, 2 * CHUNK), BF16),
                        pltpu.VMEM((n_v, HEAD_DIM, HEAD_DIM), F32),
                        pltpu.VMEM((n_v, tb, HEAD_DIM), F32)],
        compiler_params=_params(2),
        name="gdn_prompt",
    )(qkvz, qkvz, gates, conv_w, pad(a_log), pad(dt_bias), o_norm.reshape(1, HEAD_DIM))


GDN_QK_HEADS = 8
GDN_V_HEADS = 16
GDN_CONV_DIM = (2 * GDN_QK_HEADS + GDN_V_HEADS) * HEAD_DIM
GDN_V_DIM = GDN_V_HEADS * HEAD_DIM


def prep_gdn_weights(w_in, w_out):
    d = w_in.shape[0]
    main = GDN_CONV_DIM + GDN_V_DIM
    w_main = w_in[:, :main].astype(BF16)
    w_gate = jnp.zeros((d, 2 * LANES), F32)
    w_gate = w_gate.at[:, :GDN_V_HEADS].set(w_in[:, main:main + GDN_V_HEADS])
    w_gate = w_gate.at[:, LANES:LANES + GDN_V_HEADS].set(w_in[:, main + GDN_V_HEADS:])
    return w_main, w_gate.astype(BF16), w_out.astype(BF16)


def gdn_layer_prompt(x, gain, weights, conv_w, a_log, dt_bias, o_norm):
    w_main, w_gate, _ = weights
    b, t, d = x.shape
    qkvz, gates = gdn_inproj(x.reshape(b * t, d), gain, w_main, w_gate, BF16, 512)
    qkvz = qkvz.reshape(b, t, -1)
    o, state = gdn_prompt(qkvz, gates.reshape(b, t, -1), conv_w, a_log, dt_bias, o_norm,
                          GDN_QK_HEADS, GDN_V_HEADS, 256)
    conv_tail = qkvz[:, t - (CONV_TAPS - 1):, :GDN_CONV_DIM].astype(F32)
    return o.reshape(b * t, -1), conv_tail, state


def _gdn_step_kernel(qkvz_ref, gate_ref, conv_ref, s_ref, cw_ref, alog_ref, dtb_ref, onorm_ref,
                     o_ref, convn_ref, sn_ref, *, n_qk, n_v):
    rep = n_v // n_qk
    conv_dim = (2 * n_qk + n_v) * HEAD_DIM
    x = qkvz_ref[:, :conv_dim]
    y = cw_ref[CONV_TAPS - 1:CONV_TAPS, :] * x
    for j in range(CONV_TAPS - 1):
        y = y + cw_ref[j:j + 1, :] * conv_ref[j:j + 1, :]
    y = _silu(y)
    convn_ref[0:CONV_TAPS - 2, :] = conv_ref[1:CONV_TAPS - 1, :]
    convn_ref[CONV_TAPS - 2:CONV_TAPS - 1, :] = x

    beta = _sigmoid(gate_ref[:, :LANES])
    g = -jnp.exp(alog_ref[...]) * _softplus(gate_ref[:, LANES:] + dtb_ref[...])
    eye = (lax.broadcasted_iota(jnp.int32, (HEAD_DIM, HEAD_DIM), 0)
           == lax.broadcasted_iota(jnp.int32, (HEAD_DIM, HEAD_DIM), 1))

    def column(row):
        return jnp.sum(jnp.where(eye, row, 0.0), axis=1, keepdims=True)

    def l2(row):
        return row * lax.rsqrt(jnp.sum(row * row, axis=-1, keepdims=True) + EPS)

    for j in range(n_qk):
        q_col = column(l2(y[:, j * HEAD_DIM:(j + 1) * HEAD_DIM]) * (HEAD_DIM ** -0.5))
        k_col = column(l2(y[:, (n_qk + j) * HEAD_DIM:(n_qk + j + 1) * HEAD_DIM]))
        for e in range(rep):
            h = rep * j + e
            lo = h * HEAD_DIM
            v = y[:, 2 * n_qk * HEAD_DIM + lo:2 * n_qk * HEAD_DIM + lo + HEAD_DIM]
            s = s_ref[h]
            e_g = jnp.exp(g[:, h:h + 1])
            k_s = jnp.sum(s * k_col, axis=0, keepdims=True)
            v_new = beta[:, h:h + 1] * (v - e_g * k_s)
            s_new = e_g * s + k_col * v_new
            sn_ref[h] = s_new
            o = _rms_rows(jnp.sum(s_new * q_col, axis=0, keepdims=True), onorm_ref[...])
            z = qkvz_ref[:, conv_dim + lo:conv_dim + lo + HEAD_DIM]
            o_ref[:, lo:lo + HEAD_DIM] = (o * _silu(z)).astype(o_ref.dtype)


def gdn_step(qkvz, gates, conv_state, state, conv_w, a_log, dt_bias, o_norm, n_qk, n_v):
    b = qkvz.shape[0]
    conv_dim = (2 * n_qk + n_v) * HEAD_DIM
    v_dim = n_v * HEAD_DIM
    pad = lambda p: jnp.zeros((1, LANES), F32).at[0, :n_v].set(p.astype(F32))
    row = lambda n: pl.BlockSpec((None, 1, n), lambda bi: (bi, 0, 0))
    const = lambda shape: pl.BlockSpec(shape, lambda bi: (0,) * len(shape))
    st = pl.BlockSpec((None, n_v, HEAD_DIM, HEAD_DIM), lambda bi: (bi, 0, 0, 0))
    cv = pl.BlockSpec((None, CONV_TAPS - 1, conv_dim), lambda bi: (bi, 0, 0))
    return pl.pallas_call(
        functools.partial(_gdn_step_kernel, n_qk=n_qk, n_v=n_v),
        grid=(b,),
        in_specs=[row(conv_dim + v_dim), row(2 * LANES), cv, st,
                  const((CONV_TAPS, conv_dim)), const((1, LANES)), const((1, LANES)), const((1, HEAD_DIM))],
        out_specs=[row(v_dim), cv, st],
        out_shape=[jax.ShapeDtypeStruct((b, 1, v_dim), BF16),
                   jax.ShapeDtypeStruct(conv_state.shape, F32),
                   jax.ShapeDtypeStruct(state.shape, F32)],
        compiler_params=_params(1),
        name="gdn_step",
    )(qkvz.reshape(b, 1, -1), gates.reshape(b, 1, -1), conv_state, state,
      conv_w, pad(a_log), pad(dt_bias), o_norm.reshape(1, HEAD_DIM))


def gdn_layer_sample(x, gain, weights, conv_w, a_log, dt_bias, o_norm, conv_state, state):
    w_main, w_gate, _ = weights
    b = x.shape[0]
    qkvz, gates = gdn_inproj(x, gain, w_main, w_gate, F32, b)
    o, conv_new, state_new = gdn_step(qkvz, gates, conv_state, state, conv_w, a_log, dt_bias, o_norm,
                                      GDN_QK_HEADS, GDN_V_HEADS)
    return o.reshape(b, -1), conv_new, state_new


DSW_HEADS = 8
DSW_WIDTH = DSW_HEADS * HEAD_DIM
DSW_TILE = DSW_BLOCK * max(d for _, d in DSW_GROUPS)
DSW_BLOCKS_PER_ITER = 8


def _dsw_attn_kernel(*refs):
    n_groups = len(DSW_GROUPS)
    ins = [refs[5 * g:5 * g + 5] for g in range(n_groups)]
    o_ref = refs[5 * n_groups]
    scratch = refs[5 * n_groups + 1:]
    kf_refs, vf_refs = scratch[:n_groups], scratch[n_groups:2 * n_groups]
    og_ref, lg_ref = scratch[2 * n_groups:]
    first_tile = pl.program_id(2) == 0
    blk = DSW_BLOCK
    qi = lax.broadcasted_iota(jnp.int32, (blk, 2 * blk), 0)
    ki = lax.broadcasted_iota(jnp.int32, (blk, 2 * blk), 1)
    band = (ki >= qi) & (ki <= qi + blk)
    scale = HEAD_DIM ** -0.5

    for g, (_, dil) in enumerate(DSW_GROUPS):
        q_ref, kc_ref, vc_ref, kp_ref, vp_ref = ins[g]
        kf_ref, vf_ref = kf_refs[g], vf_refs[g]
        per_res = q_ref.shape[1] // blk
        kf_ref[:, :blk, :] = kp_ref[...]
        kf_ref[:, blk:, :] = kc_ref[...]
        vf_ref[:, :blk, :] = vp_ref[...]
        vf_ref[:, blk:, :] = vc_ref[...]

        def blocks_body(it, carry, *, g=g, dil=dil, per_res=per_res, q_ref=q_ref, kf_ref=kf_ref, vf_ref=vf_ref):
            ids = [it * DSW_BLOCKS_PER_ITER + i for i in range(DSW_BLOCKS_PER_ITER)]
            rs = [bi // per_res for bi in ids]
            ms = [bi % per_res for bi in ids]
            row0s = [pl.multiple_of(m * blk, blk) for m in ms]
            ss = [lax.dot_general(q_ref[r, pl.ds(row0, blk), :], kf_ref[r, pl.ds(row0, 2 * blk), :],
                                  (((1,), (1,)), ((), ())), preferred_element_type=F32) * scale
                  for r, row0 in zip(rs, row0s)]
            ss = [jnp.where(band & ((ki >= blk) | (m > 0) | jnp.logical_not(first_tile)), s, NEG_INF)
                  for s, m in zip(ss, ms)]
            mxs = [jnp.max(s, axis=-1, keepdims=True) for s in ss]
            ps = [jnp.exp(s - mx) for s, mx in zip(ss, mxs)]
            dens = [jnp.sum(p, axis=-1, keepdims=True) for p in ps]
            os_ = [jnp.dot(p.astype(BF16), vf_ref[r, pl.ds(row0, 2 * blk), :], preferred_element_type=F32)
                   for p, r, row0 in zip(ps, rs, row0s)]
            for o, den, mx, r, row0 in zip(os_, dens, mxs, rs, row0s):
                o = o * (1.0 / den)
                lse = jnp.broadcast_to(mx + jnp.log(den), (blk, LANES))
                start = row0 * dil + r
                rows = pl.ds(start, blk) if dil == 1 else pl.ds(start, blk, stride=dil)
                og_ref[g, rows, :] = o
                lg_ref[g, rows, :] = lse
            return carry

        lax.fori_loop(0, dil * per_res // DSW_BLOCKS_PER_ITER, blocks_body, 0)

    top = lg_ref[0]
    for g in range(1, n_groups):
        top = jnp.maximum(top, lg_ref[g])
    num = jnp.zeros_like(top)
    den = jnp.zeros_like(top)
    for g in range(n_groups):
        w = jnp.exp(lg_ref[g] - top)
        num = num + w * og_ref[g]
        den = den + w
    o_ref[...] = (num * (1.0 / den)).astype(o_ref.dtype)


def dsw_attn(projs):
    b = projs[0].shape[0]
    t = projs[0].shape[1] * projs[0].shape[2]
    tile = DSW_TILE
    heads = DSW_HEADS
    in_specs, args, kv_scratch = [], [], []
    for (_, dil), p in zip(DSW_GROUPS, projs):
        rows = tile // dil
        per_res = rows // DSW_BLOCK
        cur = lambda col: pl.BlockSpec((None, dil, rows, HEAD_DIM),
                                       functools.partial(lambda bi, h, n, col: (bi, 0, n, col * heads + h), col=col))
        prev = lambda col: pl.BlockSpec(
            (None, dil, DSW_BLOCK, HEAD_DIM),
            functools.partial(lambda bi, h, n, col, per_res: (bi, 0, jnp.maximum(n * per_res - 1, 0), col * heads + h),
                              col=col, per_res=per_res))
        in_specs += [cur(0), cur(1), cur(2), prev(1), prev(2)]
        args += [p] * 5
        kv_scratch.append(pltpu.VMEM((dil, DSW_BLOCK + rows, HEAD_DIM), BF16))
    n_groups = len(DSW_GROUPS)
    return pl.pallas_call(
        _dsw_attn_kernel,
        grid=(b, heads, t // tile),
        in_specs=in_specs,
        out_specs=pl.BlockSpec((None, tile, HEAD_DIM), lambda bi, h, n: (bi, n, h)),
        out_shape=jax.ShapeDtypeStruct((b, t, heads * HEAD_DIM), BF16),
        scratch_shapes=kv_scratch + kv_scratch + [pltpu.VMEM((n_groups, tile, HEAD_DIM), F32),
                                                  pltpu.VMEM((n_groups, tile, LANES), F32)],
        compiler_params=_params(3),
        name="dsw_attn",
    )(*args)


def prep_dsw_weights(w_in, w_out):
    d = w_in.shape[0]
    w = w_in.reshape(d, len(DSW_GROUPS), 3, DSW_WIDTH).transpose(1, 2, 0, 3).astype(BF16)
    return [w[g] for g in range(len(DSW_GROUPS))], w_out.astype(BF16)


def dsw_layer_prompt(x, gain, weights, q_norm, k_norm):
    w_groups, _ = weights
    b, t, d = x.shape
    projs = [dsw_inproj(x, gain, jnp.stack([q_norm[g], k_norm[g]]), w_groups[g], dil, 512)
             for g, (_, dil) in enumerate(DSW_GROUPS)]
    return dsw_attn(projs).reshape(b * t, -1), projs


CACHE_RESIDUES = 4


def _dsw_cache_kernel(*refs):
    *p_refs, o_ref, slab_scr = refs
    layer = pl.program_id(0)
    rows, n_res = o_ref.shape[:2]
    for li in range(len(p_refs) // 2):
        @pl.when(layer == li)
        def _(k_ref=p_refs[2 * li], v_ref=p_refs[2 * li + 1]):
            for rr in range(n_res):
                for kv, p_ref in enumerate((k_ref, v_ref)):
                    slab = slab_scr.at[2 * rr + kv]
                    slab[...] = p_ref[rr].astype(o_ref.dtype).reshape(rows, DSW_HEADS, HEAD_DIM)
                    o_ref[:, rr, kv] = slab[...]


def dsw_prompt_caches(projs_layers):
    n_layers = len(projs_layers)
    outs = []
    for g, (window, dil) in enumerate(DSW_GROUPS):
        ps = [pl_[g] for pl_ in projs_layers]
        b, _, sub, _ = ps[0].shape
        last = sub // DSW_BLOCK - 1
        n_res = min(dil, CACHE_RESIDUES)

        def in_map(l, bi, rb, *, li, col):
            before, after = l < li, l > li
            pick = lambda lo, x, hi: jnp.where(before, lo, jnp.where(after, hi, x))
            return (pick(0, bi, b - 1), pick(0, rb, dil // n_res - 1), last, col)

        out = pl.pallas_call(
            _dsw_cache_kernel,
            grid=(n_layers, b, dil // n_res),
            in_specs=[pl.BlockSpec((None, n_res, DSW_BLOCK, DSW_WIDTH), functools.partial(in_map, li=li, col=col))
                      for li in range(n_layers) for col in (1, 2)],
            out_specs=pl.BlockSpec((None, None, DSW_BLOCK, n_res, 2, DSW_HEADS, HEAD_DIM),
                                   lambda l, bi, rb: (l, bi, 0, rb, 0, 0, 0)),
            out_shape=jax.ShapeDtypeStruct((n_layers, b, DSW_BLOCK, dil, 2, DSW_HEADS, HEAD_DIM), F32),
            scratch_shapes=[pltpu.VMEM((2 * n_res, DSW_BLOCK, DSW_HEADS, HEAD_DIM), F32)],
            compiler_params=_params(3),
            name="dsw_prompt_cache",
        )(*[p for p in ps for _ in (1, 2)])
        outs.append(out.reshape(n_layers, b, window, 2, DSW_HEADS, HEAD_DIM))
    return outs


def _dsw_decode_kernel(p_ref, *refs):
    n_groups = len(DSW_GROUPS)
    c_refs = refs[:n_groups]
    o_ref, new_ref = refs[n_groups:]
    scale = HEAD_DIM ** -0.5
    outs, lses = [], []
    for g in range(n_groups):
        q = p_ref[g, 0].astype(F32)
        k_new = p_ref[g, 1].astype(F32)
        v_new = p_ref[g, 2].astype(F32)
        new_ref[g, 0] = k_new
        new_ref[g, 1] = v_new
        s = jnp.sum(c_refs[g][:, 0] * q[None], axis=-1, keepdims=True) * scale
        s_new = jnp.sum(k_new * q, axis=-1, keepdims=True) * scale
        mx = jnp.maximum(jnp.max(s, axis=0), s_new)
        p = jnp.exp(s - mx[None])
        p_new = jnp.exp(s_new - mx)
        den = jnp.sum(p, axis=0) + p_new
        outs.append((jnp.sum(p * c_refs[g][:, 1], axis=0) + p_new * v_new) * (1.0 / den))
        lses.append(mx + jnp.log(den))
    top = functools.reduce(jnp.maximum, lses)
    ws = [jnp.exp(l - top) for l in lses]
    num = sum(w * o for w, o in zip(ws, outs))
    o_ref[...] = (num * (1.0 / sum(ws))).astype(o_ref.dtype)


def dsw_decode(proj, caches, layer):
    b = proj.shape[0]
    n_groups = len(DSW_GROUPS)
    c_specs, c_args = [], []
    for (window, dil), c in zip(DSW_GROUPS, caches):
        c_args.append(c.reshape(c.shape[0], b, window // dil, dil, 2, DSW_HEADS, HEAD_DIM))
        c_specs.append(pl.BlockSpec((None, None, window // dil, None, 2, DSW_HEADS, HEAD_DIM),
                                    lambda bi: (layer, bi, 0, 0, 0, 0, 0)))
    return pl.pallas_call(
        _dsw_decode_kernel,
        grid=(b,),
        in_specs=[pl.BlockSpec((None, n_groups, 3, DSW_HEADS, HEAD_DIM), lambda bi: (bi, 0, 0, 0, 0))] + c_specs,
        out_specs=[pl.BlockSpec((None, DSW_HEADS, HEAD_DIM), lambda bi: (bi, 0, 0)),
                   pl.BlockSpec((None, n_groups, 2, DSW_HEADS, HEAD_DIM), lambda bi: (bi, 0, 0, 0, 0))],
        out_shape=[jax.ShapeDtypeStruct((b, DSW_HEADS, HEAD_DIM), BF16),
                   jax.ShapeDtypeStruct((b, n_groups, 2, DSW_HEADS, HEAD_DIM), F32)],
        compiler_params=_params(1),
        name="dsw_decode",
    )(proj, *c_args)


def dsw_layer_sample(x, gain, weights, q_norm, k_norm, caches, layer):
    w_groups, _ = weights
    b, d = x.shape
    proj = jnp.stack([dsw_inproj(x.reshape(1, b, d), gain, jnp.stack([q_norm[g], k_norm[g]]), w_groups[g], 1, b)
                      .reshape(b, 3, DSW_HEADS, HEAD_DIM) for g in range(len(DSW_GROUPS))], axis=1)
    o, new_rows = dsw_decode(proj, caches, layer)
    return o.reshape(b, -1), new_rows


SHIFT_ROWS = 256


def _shift_kernel(c_ref, next_ref, new_ref, o_ref):
    rows = c_ref.shape[0]
    is_last = pl.program_id(2) == pl.num_programs(2) - 1
    o_ref[0:rows - 1] = c_ref[1:rows]

    @pl.when(is_last)
    def _():
        o_ref[rows - 1] = new_ref[...]

    @pl.when(jnp.logical_not(is_last))
    def _():
        o_ref[rows - 1] = next_ref[0]


def shift_caches(caches, new_rows_layers):
    new = jnp.stack(new_rows_layers)
    outs = []
    for g, c in enumerate(caches):
        n_layers, b, window = c.shape[:3]
        rows = min(SHIFT_ROWS, window)
        tile = (2, DSW_HEADS, HEAD_DIM)
        blk = pl.BlockSpec((None, None, rows) + tile, lambda l, bi, i: (l, bi, i, 0, 0, 0))
        nxt = pl.BlockSpec((None, None, 1) + tile,
                           lambda l, bi, i: (l, bi, jnp.minimum((i + 1) * rows, window - 1), 0, 0, 0))
        outs.append(pl.pallas_call(
            _shift_kernel,
            grid=(n_layers, b, window // rows),
            in_specs=[blk, nxt,
                      pl.BlockSpec((None, None, None) + tile, functools.partial(lambda l, bi, i, g: (l, bi, g, 0, 0, 0), g=g))],
            out_specs=blk,
            out_shape=jax.ShapeDtypeStruct(c.shape, c.dtype),
            compiler_params=_params(3),
            name="shift_cache",
        )(c, c, new))
    return outs


def kernel(x_prompt, x_sample, state_gdn, state_conv, cache_kv_w128, cache_kv_w512, cache_kv_w2048,
           norm_mix, norm_mlp, gdn_w_in, gdn_conv_w, gdn_a_log, gdn_dt_bias, gdn_o_norm, gdn_w_out,
           dsw_w_in, dsw_q_norm, dsw_k_norm, dsw_w_out, mlp_w_up, mlp_w_down):
    b, t, d = x_prompt.shape
    bs = x_sample.shape[0]
    depth = norm_mix.shape[0]
    caches = [cache_kv_w128, cache_kv_w512, cache_kv_w2048]
    yp = x_prompt.reshape(b * t, d)
    ys = x_sample.reshape(bs, d)
    p_gdn, p_conv, s_gdn, s_conv, p_projs, s_rows = [], [], [], [], [], []
    for i in range(depth):
        j = i // 2
        if i % 2 == 0:
            wts = prep_gdn_weights(gdn_w_in[j], gdn_w_out[j])
            par = (gdn_conv_w[j], gdn_a_log[j], gdn_dt_bias[j], gdn_o_norm[j])
            op, conv_p, state_p = gdn_layer_prompt(yp.reshape(b, t, d), norm_mix[i], wts, *par)
            os_, conv_s, state_s = gdn_layer_sample(ys, norm_mix[i], wts, *par, state_conv[j], state_gdn[j])
            p_gdn.append(state_p)
            p_conv.append(conv_p)
            s_gdn.append(state_s)
            s_conv.append(conv_s)
        else:
            wts = prep_dsw_weights(dsw_w_in[j], dsw_w_out[j])
            op, projs = dsw_layer_prompt(yp.reshape(b, t, d), norm_mix[i], wts, dsw_q_norm[j], dsw_k_norm[j])
            os_, rows = dsw_layer_sample(ys, norm_mix[i], wts, dsw_q_norm[j], dsw_k_norm[j], caches, j)
            p_projs.append(projs)
            s_rows.append(rows)
        w_up, w_down = mlp_w_up[i].astype(BF16), mlp_w_down[i].astype(BF16)
        yp = mixer_mlp(yp, op, wts[-1], norm_mlp[i], w_up, w_down, 1024, 512)
        ys = mixer_mlp(ys, os_, wts[-1], norm_mlp[i], w_up, w_down, bs, 512)
    yp = yp.reshape(b, t, d)
    p_kv = dsw_prompt_caches(p_projs)
    s_kv = shift_caches(caches, s_rows)
    return (yp, ys.reshape(x_sample.shape),
            jnp.stack(p_gdn), jnp.stack(p_conv), p_kv[0], p_kv[1], p_kv[2],
            jnp.stack(s_gdn), jnp.stack(s_conv), s_kv[0], s_kv[1], s_kv[2])
```

```python
import functools

import jax
import jax.numpy as jnp
from jax import lax
from jax.experimental import pallas as pl
from jax.experimental.pallas import tpu as pltpu

F32 = jnp.float32
BF16 = jnp.bfloat16

EPS = 1e-6
NEG_INF = -1e30
LANES = 128
CONV_TAPS = 4
CHUNK = 64
HEAD_DIM = 128
DSW_BLOCK = 128
DSW_GROUPS = ((128, 1), (512, 4), (2048, 16))
VMEM_LIMIT = 56 * 1024 * 1024


def _params(n_axes, vmem=VMEM_LIMIT):
    return pltpu.CompilerParams(dimension_semantics=("arbitrary",) * n_axes,
                                vmem_limit_bytes=vmem)


def _sigmoid(x):
    return 0.5 + 0.5 * jnp.tanh(0.5 * x)


def _silu(x):
    h = 0.5 * x
    return h + h * jnp.tanh(h)


def _softplus(x):
    return jnp.maximum(x, 0.0) + jnp.log(1.0 + jnp.exp(-jnp.abs(x)))


def _rms_rows(x, gain_row):
    ms = jnp.mean(x * x, axis=-1, keepdims=True)
    return x * lax.rsqrt(ms + EPS) * gain_row


INPROJ_COLS = 512


def _gdn_inproj_kernel(x_ref, g_ref, w_ref, wg_ref, o_ref, og_ref, xn_ref):
    xn_ref[...] = _rms_rows(x_ref[...], g_ref[...]).astype(BF16)
    for lo in range(0, w_ref.shape[1], INPROJ_COLS):
        o_ref[:, lo:lo + INPROJ_COLS] = jnp.dot(
            xn_ref[...], w_ref[:, lo:lo + INPROJ_COLS], preferred_element_type=F32).astype(o_ref.dtype)
    og_ref[...] = jnp.dot(xn_ref[...], wg_ref[...], preferred_element_type=F32)


def gdn_inproj(x, gain, w_main, w_gate, out_dtype, tm):
    m, k = x.shape
    n, ng = w_main.shape[1], w_gate.shape[1]
    tm = min(tm, m)
    return pl.pallas_call(
        _gdn_inproj_kernel,
        grid=(m // tm,),
        in_specs=[pl.BlockSpec((tm, k), lambda i: (i, 0)),
                  pl.BlockSpec((1, k), lambda i: (0, 0)),
                  _resident((k, n)), _resident((k, ng))],
        out_specs=[pl.BlockSpec((tm, n), lambda i: (i, 0)),
                   pl.BlockSpec((tm, ng), lambda i: (i, 0))],
        out_shape=[jax.ShapeDtypeStruct((m, n), out_dtype), jax.ShapeDtypeStruct((m, ng), F32)],
        scratch_shapes=[pltpu.VMEM((tm, k), BF16)],
        compiler_params=_params(1),
        name="gdn_inproj",
    )(x, gain.reshape(1, k), w_main, w_gate)


def _mixer_mlp_kernel(x_ref, a_ref, wo_ref, g_ref, wu_ref, wd_ref, o_ref, xn_ref, *, tf):
    x1 = x_ref[...] + jnp.dot(a_ref[...], wo_ref[...], preferred_element_type=F32)
    xn_ref[...] = _rms_rows(x1, g_ref[...]).astype(BF16)
    o_ref[...] = x1
    for lo in range(0, wu_ref.shape[1], tf):
        h = jnp.dot(xn_ref[...], wu_ref[:, lo:lo + tf], preferred_element_type=F32)
        h = jnp.square(jnp.maximum(h, 0.0)).astype(BF16)
        o_ref[...] += jnp.dot(h, wd_ref[lo:lo + tf, :], preferred_element_type=F32)


def _resident(shape):
    return pl.BlockSpec(shape, lambda *_: (0,) * len(shape), pipeline_mode=pl.Buffered(1))


def mixer_mlp(x, a, w_out, gain, w_up, w_down, tm, tf):
    m, d = x.shape
    ka = a.shape[1]
    ff = w_up.shape[1]
    tm = min(tm, m)
    return pl.pallas_call(
        functools.partial(_mixer_mlp_kernel, tf=tf),
        grid=(m // tm,),
        in_specs=[pl.BlockSpec((tm, d), lambda i: (i, 0)),
                  pl.BlockSpec((tm, ka), lambda i: (i, 0)),
                  _resident((ka, d)), _resident((1, d)), _resident((d, ff)), _resident((ff, d))],
        out_specs=pl.BlockSpec((tm, d), lambda i: (i, 0)),
        out_shape=jax.ShapeDtypeStruct((m, d), F32),
        scratch_shapes=[pltpu.VMEM((tm, d), BF16)],
        compiler_params=_params(1),
        name="mixer_mlp",
    )(x, a, w_out, gain.reshape(1, d), w_up, w_down)


def _dsw_inproj_kernel(x_ref, g_ref, hn_ref, w_ref, o_ref, xn_ref, head_scr, *, dil):
    rows = x_ref.shape[0]
    sub = rows // dil
    width = w_ref.shape[2]
    xn_ref[...] = _rms_rows(x_ref[...], g_ref[...]).astype(BF16)
    for t in range(3):
        acc = jnp.dot(xn_ref[...], w_ref[t], preferred_element_type=F32)
        for h in range(width // HEAD_DIM):
            a = acc[:, h * HEAD_DIM:(h + 1) * HEAD_DIM]
            if t < 2:
                a = _rms_rows(a, hn_ref[t:t + 1, :])
            cols = slice(t * width + h * HEAD_DIM, t * width + (h + 1) * HEAD_DIM)
            if dil == 1:
                o_ref[0, :, cols] = a.astype(o_ref.dtype)
            else:
                head_scr[h] = a
                for r in range(dil):
                    o_ref[r, :, cols] = head_scr[h, pl.ds(r, sub, stride=dil), :].astype(o_ref.dtype)


def dsw_inproj(x, gain, head_gains, w3, dil, tile_rows):
    b, t, d = x.shape
    width = w3.shape[2]
    rows = min(tile_rows, t)
    return pl.pallas_call(
        functools.partial(_dsw_inproj_kernel, dil=dil),
        grid=(b, t // rows),
        in_specs=[pl.BlockSpec((None, rows, d), lambda bi, i: (bi, i, 0)),
                  pl.BlockSpec((1, d), lambda bi, i: (0, 0)),
                  pl.BlockSpec((2, HEAD_DIM), lambda bi, i: (0, 0)),
                  _resident((3, d, width))],
        out_specs=pl.BlockSpec((None, dil, rows // dil, 3 * width), lambda bi, i: (bi, 0, i, 0)),
        out_shape=jax.ShapeDtypeStruct((b, dil, t // dil, 3 * width), BF16),
        scratch_shapes=[pltpu.VMEM((rows, d), BF16),
                        pltpu.VMEM((width // HEAD_DIM, rows, HEAD_DIM), F32)],
        compiler_params=_params(2),
        name="dsw_inproj",
    )(x, gain.reshape(1, d), head_gains, w3)


TRI_BASE = 8


def _pair_block_diag(m):
    n = m.shape[0]
    mb = m.astype(BF16)
    lane = lax.broadcasted_iota(jnp.int32, mb.shape, 1)
    zero = jnp.zeros_like(mb)
    return jnp.concatenate([jnp.where(lane < n, mb, zero), jnp.where(lane >= n, mb, zero)], axis=0)


def _pair_mm(u, w):
    return jnp.dot(u.astype(BF16), _pair_block_diag(w), preferred_element_type=F32)


def _tri_inverse_pairs(a_list):
    n = a_list[0].shape[0]
    ii = lax.broadcasted_iota(jnp.int32, (n, 2 * n), 0)
    jj = lax.broadcasted_iota(jnp.int32, (n, 2 * n), 1) % n
    same_block = lambda size: (ii // size) == (jj // size)
    ps = [jnp.where(same_block(TRI_BASE), a, 0.0) for a in a_list]
    xs = [jnp.where(ii == jj, 1.0, 0.0) - p for p in ps]
    for _ in range((TRI_BASE - 1).bit_length() - 1):
        ps = [_pair_mm(p, p) for p in ps]
        xs = [x + _pair_mm(x, p) for x, p in zip(xs, ps)]
    size = TRI_BASE
    while size < n:
        off = same_block(2 * size) & jnp.logical_not(same_block(size))
        ys = [_pair_mm(x, jnp.where(off, a, 0.0)) for x, a in zip(xs, a_list)]
        xs = [x - _pair_mm(y, x) for x, y in zip(xs, ys)]
        size *= 2
    return xs


def _gdn_prompt_kernel(qkv_ref, z_ref, gate_ref, cw_ref, alog_ref, dtb_ref, onorm_ref,
                       o_ref, s_out_ref,
                       xe_ref, qk_ref, v_ref, gate_scr, bb_ref, gcb_ref, tp_ref, pp_ref, s_ref, oscr_ref,
                       *, n_qk, n_v):
    tb = qkv_ref.shape[0]
    nchunk = tb // CHUNK
    t_idx = pl.program_id(1)
    halo = 8

    @pl.when(t_idx == 0)
    def _():
        xe_ref[:, halo - (CONV_TAPS - 1):halo, :] = jnp.zeros((xe_ref.shape[0], CONV_TAPS - 1, LANES), F32)
        s_ref[...] = jnp.zeros_like(s_ref)

    gate_scr[0] = _sigmoid(gate_ref[:, :LANES])
    g = -jnp.exp(alog_ref[...]) * _softplus(gate_ref[:, LANES:] + dtb_ref[...])
    pos = lax.broadcasted_iota(jnp.int32, (tb, LANES), 0) % CHUNK
    shift = 1
    while shift < CHUNK:
        g = g + jnp.where(pos >= shift, pltpu.roll(g, shift, axis=0), 0.0)
        shift *= 2
    gate_scr[1] = g
    for h in range(n_v):
        bb_ref[h] = jnp.broadcast_to(gate_scr[0, :, h:h + 1], (tb, LANES))
        gcb_ref[h] = jnp.broadcast_to(gate_scr[1, :, h:h + 1], (tb, LANES))

    for cb in range(n_qk * 2 + n_v):
        lo = cb * LANES
        xe_ref[cb, halo:halo + tb, :] = qkv_ref[:, lo:lo + LANES].astype(F32)
        y = cw_ref[0:1, lo:lo + LANES] * xe_ref[cb, halo - 3:halo - 3 + tb, :]
        for j in range(1, CONV_TAPS):
            y = y + cw_ref[j:j + 1, lo:lo + LANES] * xe_ref[cb, halo - 3 + j:halo - 3 + j + tb, :]
        y = _silu(y)
        xe_ref[cb, halo - 3:halo, :] = xe_ref[cb, halo + tb - 3:halo + tb, :]
        if cb < 2 * n_qk:
            y = y * lax.rsqrt(jnp.sum(y * y, axis=-1, keepdims=True) + EPS)
            if cb < n_qk:
                y = y * (HEAD_DIM ** -0.5)
            yb = y.astype(BF16)
            half = 0 if cb < n_qk else CHUNK
            for c in range(nchunk):
                qk_ref[cb % n_qk, c, half:half + CHUNK, :] = yb[c * CHUNK:(c + 1) * CHUNK, :]
        else:
            v_ref[cb - 2 * n_qk] = y

    ii = lax.broadcasted_iota(jnp.int32, (CHUNK, 2 * CHUNK), 0)
    lane = lax.broadcasted_iota(jnp.int32, (CHUNK, 2 * CHUNK), 1)
    jj = lane % CHUNK
    causal = ii >= jj
    strict = ii > jj
    pairs_per_iter = 4

    def inverse_body(it, carry):
        probs = [(it * pairs_per_iter + dj, c) for dj in range(pairs_per_iter) for c in range(nchunk)]
        grams, a_list, decays = [], [], []
        for j, c in probs:
            qk = qk_ref[j, c]
            k2 = jnp.concatenate([qk[CHUNK:], qk[CHUNK:]], axis=0)
            grams.append(lax.dot_general(qk, k2, (((1,), (1,)), ((), ())), preferred_element_type=F32))
        for (j, c), gram in zip(probs, grams):
            rows = slice(c * CHUNK, (c + 1) * CHUNK)
            gc = jnp.where(lane < CHUNK, gcb_ref[2 * j, rows, :], gcb_ref[2 * j + 1, rows, :])
            bt = jnp.where(lane < CHUNK, bb_ref[2 * j, rows, :], bb_ref[2 * j + 1, rows, :])
            gc_row = jnp.sum(jnp.where(ii == jj, gc, 0.0), axis=0, keepdims=True)
            decay = jnp.where(causal, jnp.exp(jnp.where(causal, gc - gc_row, 0.0)), 0.0)
            a_list.append(jnp.where(strict, bt * gram[CHUNK:] * decay, 0.0))
            pp_ref[j, c] = (gram[:CHUNK] * decay).astype(BF16)
        for (j, c), t_inv in zip(probs, _tri_inverse_pairs(a_list)):
            tp_ref[j, c] = t_inv.astype(BF16)
        return carry

    lax.fori_loop(0, n_qk // pairs_per_iter, inverse_body, 0)

    def block_diag2(m):
        mb = m.astype(BF16)
        zero = jnp.zeros((CHUNK, HEAD_DIM), BF16)
        return jnp.concatenate([jnp.concatenate([mb[:, :HEAD_DIM], zero], axis=1),
                                jnp.concatenate([zero, mb[:, HEAD_DIM:]], axis=1)], axis=0)

    def chunk_body(c, carry):
        rows = pl.ds(pl.multiple_of(c * CHUNK, CHUNK), CHUNK)
        heads = range(n_qk)
        pair = lambda ref, j: jnp.concatenate([ref[2 * j, rows, :], ref[2 * j + 1, rows, :]], axis=1)
        qks = [qk_ref[j, c] for j in heads]
        projs = [jnp.dot(qks[j], jnp.concatenate([s_ref[2 * j], s_ref[2 * j + 1]], axis=1).astype(BF16),
                         preferred_element_type=F32) for j in heads]
        gcs = [pair(gcb_ref, j) for j in heads]
        e_gcs = [jnp.exp(gc) for gc in gcs]
        rhss = [pair(bb_ref, j) * (pair(v_ref, j) - e_gcs[j] * projs[j][CHUNK:]) for j in heads]
        v_news = [jnp.dot(tp_ref[j, c], block_diag2(rhss[j]), preferred_element_type=F32) for j in heads]
        outs = [e_gcs[j] * projs[j][:CHUNK]
                + jnp.dot(pp_ref[j, c], block_diag2(v_news[j]), preferred_element_type=F32) for j in heads]
        g_lasts = [gc[CHUNK - 1:CHUNK, :] for gc in gcs]
        v_decs = [(jnp.exp(g_lasts[j] - gcs[j]) * v_news[j]).astype(BF16) for j in heads]
        d_states = [lax.dot_general(qks[j][CHUNK:], v_decs[j], (((0,), (0,)), ((), ())),
                                    preferred_element_type=F32) for j in heads]
        for j in heads:
            for e in range(2):
                lanes = slice(e * HEAD_DIM, (e + 1) * HEAD_DIM)
                oscr_ref[2 * j + e, rows, :] = outs[j][:, lanes]
                s_ref[2 * j + e] = jnp.exp(g_lasts[j][:, lanes]) * s_ref[2 * j + e] + d_states[j][:, lanes]
        return carry

    lax.fori_loop(0, nchunk, chunk_body, 0)

    for h in range(n_v):
        lo = h * HEAD_DIM
        o = _rms_rows(oscr_ref[h], onorm_ref[...])
        o_ref[:, lo:lo + HEAD_DIM] = (o * _silu(z_ref[:, lo:lo + HEAD_DIM].astype(F32))).astype(o_ref.dtype)

    @pl.when(t_idx == pl.num_programs(1) - 1)
    def _():
        s_out_ref[...] = s_ref[...]


def gdn_prompt(qkvz, gates, conv_w, a_log, dt_bias, o_norm, n_qk, n_v, tb):
    b, t, _ = qkvz.shape
    assert n_v == 2 * n_qk and 2 * CHUNK == LANES
    conv_dim = (2 * n_qk + n_v) * HEAD_DIM
    v_dim = n_v * HEAD_DIM
    tb = min(tb, t)
    nchunk = tb // CHUNK
    pad = lambda p: jnp.zeros((1, LANES), F32).at[0, :n_v].set(p.astype(F32))
    return pl.pallas_call(
        functools.partial(_gdn_prompt_kernel, n_qk=n_qk, n_v=n_v),
        grid=(b, t // tb),
        in_specs=[pl.BlockSpec((None, tb, conv_dim), lambda bi, ti: (bi, ti, 0)),
                  pl.BlockSpec((None, tb, v_dim), lambda bi, ti: (bi, ti, conv_dim // v_dim)),
                  pl.BlockSpec((None, tb, 2 * LANES), lambda bi, ti: (bi, ti, 0)),
                  pl.BlockSpec((CONV_TAPS, conv_dim), lambda bi, ti: (0, 0)),
                  pl.BlockSpec((1, LANES), lambda bi, ti: (0, 0)),
                  pl.BlockSpec((1, LANES), lambda bi, ti: (0, 0)),
                  pl.BlockSpec((1, HEAD_DIM), lambda bi, ti: (0, 0))],
        out_specs=[pl.BlockSpec((None, tb, v_dim), lambda bi, ti: (bi, ti, 0)),
                   pl.BlockSpec((None, n_v, HEAD_DIM, HEAD_DIM), lambda bi, ti: (bi, 0, 0, 0))],
        out_shape=[jax.ShapeDtypeStruct((b, t, v_dim), BF16),
                   jax.ShapeDtypeStruct((b, n_v, HEAD_DIM, HEAD_DIM), F32)],
        scratch_shapes=[pltpu.VMEM((2 * n_qk + n_v, tb + 8, LANES), F32),
                        pltpu.VMEM((n_qk, nchunk, 2 * CHUNK, HEAD_DIM), BF16),
                        pltpu.VMEM((n_v, tb, HEAD_DIM), F32),
                        pltpu.VMEM((2, tb, LANES), F32),
                        pltpu.VMEM((n_v, tb, LANES), F32),
                        pltpu.VMEM((n_v, tb, LANES), F32),
                        pltpu.VMEM((n_qk, nchunk, CHUNK, 2 * CHUNK), BF16),
                        pltpu.VMEM((n_qk, nchunk, CHUNK, 2 * CHUNK), BF16),
                        pltpu.VMEM((n_v, HEAD_DIM, HEAD_DIM), F32),
                        pltpu.VMEM((n_v, tb, HEAD_DIM), F32)],
        compiler_params=_params(2),
        name="gdn_prompt",
    )(qkvz, qkvz, gates, conv_w, pad(a_log), pad(dt_bias), o_norm.reshape(1, HEAD_DIM))


GDN_QK_HEADS = 8
GDN_V_HEADS = 16
GDN_CONV_DIM = (2 * GDN_QK_HEADS + GDN_V_HEADS) * HEAD_DIM
GDN_V_DIM = GDN_V_HEADS * HEAD_DIM


def prep_gdn_weights(w_in, w_out):
    d = w_in.shape[0]
    main = GDN_CONV_DIM + GDN_V_DIM
    w_main = w_in[:, :main].astype(BF16)
    w_gate = jnp.zeros((d, 2 * LANES), F32)
    w_gate = w_gate.at[:, :GDN_V_HEADS].set(w_in[:, main:main + GDN_V_HEADS])
    w_gate = w_gate.at[:, LANES:LANES + GDN_V_HEADS].set(w_in[:, main + GDN_V_HEADS:])
    return w_main, w_gate.astype(BF16), w_out.astype(BF16)


def gdn_layer_prompt(x, gain, weights, conv_w, a_log, dt_bias, o_norm):
    w_main, w_gate, _ = weights
    b, t, d = x.shape
    qkvz, gates = gdn_inproj(x.reshape(b * t, d), gain, w_main, w_gate, BF16, 512)
    qkvz = qkvz.reshape(b, t, -1)
    o, state = gdn_prompt(qkvz, gates.reshape(b, t, -1), conv_w, a_log, dt_bias, o_norm,
                          GDN_QK_HEADS, GDN_V_HEADS, 256)
    conv_tail = qkvz[:, t - (CONV_TAPS - 1):, :GDN_CONV_DIM].astype(F32)
    return o.reshape(b * t, -1), conv_tail, state


def _gdn_step_kernel(qkvz_ref, gate_ref, conv_ref, s_ref, cw_ref, alog_ref, dtb_ref, onorm_ref,
                     o_ref, convn_ref, sn_ref, *, n_qk, n_v):
    rep = n_v // n_qk
    conv_dim = (2 * n_qk + n_v) * HEAD_DIM
    x = qkvz_ref[:, :conv_dim]
    y = cw_ref[CONV_TAPS - 1:CONV_TAPS, :] * x
    for j in range(CONV_TAPS - 1):
        y = y + cw_ref[j:j + 1, :] * conv_ref[j:j + 1, :]
    y = _silu(y)
    convn_ref[0:CONV_TAPS - 2, :] = conv_ref[1:CONV_TAPS - 1, :]
    convn_ref[CONV_TAPS - 2:CONV_TAPS - 1, :] = x

    beta = _sigmoid(gate_ref[:, :LANES])
    g = -jnp.exp(alog_ref[...]) * _softplus(gate_ref[:, LANES:] + dtb_ref[...])
    eye = (lax.broadcasted_iota(jnp.int32, (HEAD_DIM, HEAD_DIM), 0)
           == lax.broadcasted_iota(jnp.int32, (HEAD_DIM, HEAD_DIM), 1))

    def column(row):
        return jnp.sum(jnp.where(eye, row, 0.0), axis=1, keepdims=True)

    def l2(row):
        return row * lax.rsqrt(jnp.sum(row * row, axis=-1, keepdims=True) + EPS)

    for j in range(n_qk):
        q_col = column(l2(y[:, j * HEAD_DIM:(j + 1) * HEAD_DIM]) * (HEAD_DIM ** -0.5))
        k_col = column(l2(y[:, (n_qk + j) * HEAD_DIM:(n_qk + j + 1) * HEAD_DIM]))
        for e in range(rep):
            h = rep * j + e
            lo = h * HEAD_DIM
            v = y[:, 2 * n_qk * HEAD_DIM + lo:2 * n_qk * HEAD_DIM + lo + HEAD_DIM]
            s = s_ref[h]
            e_g = jnp.exp(g[:, h:h + 1])
            k_s = jnp.sum(s * k_col, axis=0, keepdims=True)
            v_new = beta[:, h:h + 1] * (v - e_g * k_s)
            s_new = e_g * s + k_col * v_new
            sn_ref[h] = s_new
            o = _rms_rows(jnp.sum(s_new * q_col, axis=0, keepdims=True), onorm_ref[...])
            z = qkvz_ref[:, conv_dim + lo:conv_dim + lo + HEAD_DIM]
            o_ref[:, lo:lo + HEAD_DIM] = (o * _silu(z)).astype(o_ref.dtype)


def gdn_step(qkvz, gates, conv_state, state, conv_w, a_log, dt_bias, o_norm, n_qk, n_v):
    b = qkvz.shape[0]
    conv_dim = (2 * n_qk + n_v) * HEAD_DIM
    v_dim = n_v * HEAD_DIM
    pad = lambda p: jnp.zeros((1, LANES), F32).at[0, :n_v].set(p.astype(F32))
    row = lambda n: pl.BlockSpec((None, 1, n), lambda bi: (bi, 0, 0))
    const = lambda shape: pl.BlockSpec(shape, lambda bi: (0,) * len(shape))
    st = pl.BlockSpec((None, n_v, HEAD_DIM, HEAD_DIM), lambda bi: (bi, 0, 0, 0))
    cv = pl.BlockSpec((None, CONV_TAPS - 1, conv_dim), lambda bi: (bi, 0, 0))
    return pl.pallas_call(
        functools.partial(_gdn_step_kernel, n_qk=n_qk, n_v=n_v),
        grid=(b,),
        in_specs=[row(conv_dim + v_dim), row(2 * LANES), cv, st,
                  const((CONV_TAPS, conv_dim)), const((1, LANES)), const((1, LANES)), const((1, HEAD_DIM))],
        out_specs=[row(v_dim), cv, st],
        out_shape=[jax.ShapeDtypeStruct((b, 1, v_dim), BF16),
                   jax.ShapeDtypeStruct(conv_state.shape, F32),
                   jax.ShapeDtypeStruct(state.shape, F32)],
        compiler_params=_params(1),
        name="gdn_step",
    )(qkvz.reshape(b, 1, -1), gates.reshape(b, 1, -1), conv_state, state,
      conv_w, pad(a_log), pad(dt_bias), o_norm.reshape(1, HEAD_DIM))


def gdn_layer_sample(x, gain, weights, conv_w, a_log, dt_bias, o_norm, conv_state, state):
    w_main, w_gate, _ = weights
    b = x.shape[0]
    qkvz, gates = gdn_inproj(x, gain, w_main, w_gate, F32, b)
    o, conv_new, state_new = gdn_step(qkvz, gates, conv_state, state, conv_w, a_log, dt_bias, o_norm,
                                      GDN_QK_HEADS, GDN_V_HEADS)
    return o.reshape(b, -1), conv_new, state_new


DSW_HEADS = 8
DSW_WIDTH = DSW_HEADS * HEAD_DIM
DSW_TILE = DSW_BLOCK * max(d for _, d in DSW_GROUPS)
DSW_BLOCKS_PER_ITER = 8


def _dsw_attn_kernel(*refs):
    n_groups = len(DSW_GROUPS)
    ins = [refs[5 * g:5 * g + 5] for g in range(n_groups)]
    o_ref = refs[5 * n_groups]
    scratch = refs[5 * n_groups + 1:]
    kf_refs, vf_refs = scratch[:n_groups], scratch[n_groups:2 * n_groups]
    og_ref, lg_ref = scratch[2 * n_groups:]
    first_tile = pl.program_id(2) == 0
    blk = DSW_BLOCK
    qi = lax.broadcasted_iota(jnp.int32, (blk, 2 * blk), 0)
    ki = lax.broadcasted_iota(jnp.int32, (blk, 2 * blk), 1)
    band = (ki >= qi) & (ki <= qi + blk)
    scale = HEAD_DIM ** -0.5

    for g, (_, dil) in enumerate(DSW_GROUPS):
        q_ref, kc_ref, vc_ref, kp_ref, vp_ref = ins[g]
        kf_ref, vf_ref = kf_refs[g], vf_refs[g]
        per_res = q_ref.shape[1] // blk
        kf_ref[:, :blk, :] = kp_ref[...]
        kf_ref[:, blk:, :] = kc_ref[...]
        vf_ref[:, :blk, :] = vp_ref[...]
        vf_ref[:, blk:, :] = vc_ref[...]

        def blocks_body(it, carry, *, g=g, dil=dil, per_res=per_res, q_ref=q_ref, kf_ref=kf_ref, vf_ref=vf_ref):
            ids = [it * DSW_BLOCKS_PER_ITER + i for i in range(DSW_BLOCKS_PER_ITER)]
            rs = [bi // per_res for bi in ids]
            ms = [bi % per_res for bi in ids]
            row0s = [pl.multiple_of(m * blk, blk) for m in ms]
            ss = [lax.dot_general(q_ref[r, pl.ds(row0, blk), :], kf_ref[r, pl.ds(row0, 2 * blk), :],
                                  (((1,), (1,)), ((), ())), preferred_element_type=F32) * scale
                  for r, row0 in zip(rs, row0s)]
            ss = [jnp.where(band & ((ki >= blk) | (m > 0) | jnp.logical_not(first_tile)), s, NEG_INF)
                  for s, m in zip(ss, ms)]
            mxs = [jnp.max(s, axis=-1, keepdims=True) for s in ss]
            ps = [jnp.exp(s - mx) for s, mx in zip(ss, mxs)]
            dens = [jnp.sum(p, axis=-1, keepdims=True) for p in ps]
            os_ = [jnp.dot(p.astype(BF16), vf_ref[r, pl.ds(row0, 2 * blk), :], preferred_element_type=F32)
                   for p, r, row0 in zip(ps, rs, row0s)]
            for o, den, mx, r, row0 in zip(os_, dens, mxs, rs, row0s):
                o = o * (1.0 / den)
                lse = jnp.broadcast_to(mx + jnp.log(den), (blk, LANES))
                start = row0 * dil + r
                rows = pl.ds(start, blk) if dil == 1 else pl.ds(start, blk, stride=dil)
                og_ref[g, rows, :] = o
                lg_ref[g, rows, :] = lse
            return carry

        lax.fori_loop(0, dil * per_res // DSW_BLOCKS_PER_ITER, blocks_body, 0)

    top = lg_ref[0]
    for g in range(1, n_groups):
        top = jnp.maximum(top, lg_ref[g])
    num = jnp.zeros_like(top)
    den = jnp.zeros_like(top)
    for g in range(n_groups):
        w = jnp.exp(lg_ref[g] - top)
        num = num + w * og_ref[g]
        den = den + w
    o_ref[...] = (num * (1.0 / den)).astype(o_ref.dtype)


def dsw_attn(projs):
    b = projs[0].shape[0]
    t = projs[0].shape[1] * projs[0].shape[2]
    tile = DSW_TILE
    heads = DSW_HEADS
    in_specs, args, kv_scratch = [], [], []
    for (_, dil), p in zip(DSW_GROUPS, projs):
        rows = tile // dil
        per_res = rows // DSW_BLOCK
        cur = lambda col: pl.BlockSpec((None, dil, rows, HEAD_DIM),
                                       functools.partial(lambda bi, h, n, col: (bi, 0, n, col * heads + h), col=col))
        prev = lambda col: pl.BlockSpec(
            (None, dil, DSW_BLOCK, HEAD_DIM),
            functools.partial(lambda bi, h, n, col, per_res: (bi, 0, jnp.maximum(n * per_res - 1, 0), col * heads + h),
                              col=col, per_res=per_res))
        in_specs += [cur(0), cur(1), cur(2), prev(1), prev(2)]
        args += [p] * 5
        kv_scratch.append(pltpu.VMEM((dil, DSW_BLOCK + rows, HEAD_DIM), BF16))
    n_groups = len(DSW_GROUPS)
    return pl.pallas_call(
        _dsw_attn_kernel,
        grid=(b, heads, t // tile),
        in_specs=in_specs,
        out_specs=pl.BlockSpec((None, tile, HEAD_DIM), lambda bi, h, n: (bi, n, h)),
        out_shape=jax.ShapeDtypeStruct((b, t, heads * HEAD_DIM), BF16),
        scratch_shapes=kv_scratch + kv_scratch + [pltpu.VMEM((n_groups, tile, HEAD_DIM), F32),
                                                  pltpu.VMEM((n_groups, tile, LANES), F32)],
        compiler_params=_params(3),
        name="dsw_attn",
    )(*args)


def prep_dsw_weights(w_in, w_out):
    d = w_in.shape[0]
    w = w_in.reshape(d, len(DSW_GROUPS), 3, DSW_WIDTH).transpose(1, 2, 0, 3).astype(BF16)
    return [w[g] for g in range(len(DSW_GROUPS))], w_out.astype(BF16)


def dsw_layer_prompt(x, gain, weights, q_norm, k_norm):
    w_groups, _ = weights
    b, t, d = x.shape
    projs = [dsw_inproj(x, gain, jnp.stack([q_norm[g], k_norm[g]]), w_groups[g], dil, 512)
             for g, (_, dil) in enumerate(DSW_GROUPS)]
    return dsw_attn(projs).reshape(b * t, -1), projs


CACHE_RESIDUES = 4


def _dsw_cache_kernel(*refs):
    *p_refs, o_ref, slab_scr = refs
    layer = pl.program_id(0)
    rows, n_res = o_ref.shape[:2]
    for li in range(len(p_refs) // 2):
        @pl.when(layer == li)
        def _(k_ref=p_refs[2 * li], v_ref=p_refs[2 * li + 1]):
            for rr in range(n_res):
                for kv, p_ref in enumerate((k_ref, v_ref)):
                    slab = slab_scr.at[2 * rr + kv]
                    slab[...] = p_ref[rr].astype(o_ref.dtype).reshape(rows, DSW_HEADS, HEAD_DIM)
                    o_ref[:, rr, kv] = slab[...]


def dsw_prompt_caches(projs_layers):
    n_layers = len(projs_layers)
    outs = []
    for g, (window, dil) in enumerate(DSW_GROUPS):
        ps = [pl_[g] for pl_ in projs_layers]
        b, _, sub, _ = ps[0].shape
        last = sub // DSW_BLOCK - 1
        n_res = min(dil, CACHE_RESIDUES)

        def in_map(l, bi, rb, *, li, col):
            before, after = l < li, l > li
            pick = lambda lo, x, hi: jnp.where(before, lo, jnp.where(after, hi, x))
            return (pick(0, bi, b - 1), pick(0, rb, dil // n_res - 1), last, col)

        out = pl.pallas_call(
            _dsw_cache_kernel,
            grid=(n_layers, b, dil // n_res),
            in_specs=[pl.BlockSpec((None, n_res, DSW_BLOCK, DSW_WIDTH), functools.partial(in_map, li=li, col=col))
                      for li in range(n_layers) for col in (1, 2)],
            out_specs=pl.BlockSpec((None, None, DSW_BLOCK, n_res, 2, DSW_HEADS, HEAD_DIM),
                                   lambda l, bi, rb: (l, bi, 0, rb, 0, 0, 0)),
            out_shape=jax.ShapeDtypeStruct((n_layers, b, DSW_BLOCK, dil, 2, DSW_HEADS, HEAD_DIM), F32),
            scratch_shapes=[pltpu.VMEM((2 * n_res, DSW_BLOCK, DSW_HEADS, HEAD_DIM), F32)],
            compiler_params=_params(3),
            name="dsw_prompt_cache",
        )(*[p for p in ps for _ in (1, 2)])
        outs.append(out.reshape(n_layers, b, window, 2, DSW_HEADS, HEAD_DIM))
    return outs


def _dsw_decode_kernel(p_ref, *refs):
    n_groups = len(DSW_GROUPS)
    c_refs = refs[:n_groups]
    o_ref, new_ref = refs[n_groups:]
    scale = HEAD_DIM ** -0.5
    outs, lses = [], []
    for g in range(n_groups):
        q = p_ref[g, 0].astype(F32)
        k_new = p_ref[g, 1].astype(F32)
        v_new = p_ref[g, 2].astype(F32)
        new_ref[g, 0] = k_new
        new_ref[g, 1] = v_new
        s = jnp.sum(c_refs[g][:, 0] * q[None], axis=-1, keepdims=True) * scale
        s_new = jnp.sum(k_new * q, axis=-1, keepdims=True) * scale
        mx = jnp.maximum(jnp.max(s, axis=0), s_new)
        p = jnp.exp(s - mx[None])
        p_new = jnp.exp(s_new - mx)
        den = jnp.sum(p, axis=0) + p_new
        outs.append((jnp.sum(p * c_refs[g][:, 1], axis=0) + p_new * v_new) * (1.0 / den))
        lses.append(mx + jnp.log(den))
    top = functools.reduce(jnp.maximum, lses)
    ws = [jnp.exp(l - top) for l in lses]
    num = sum(w * o for w, o in zip(ws, outs))
    o_ref[...] = (num * (1.0 / sum(ws))).astype(o_ref.dtype)


def dsw_decode(proj, caches, layer):
    b = proj.shape[0]
    n_groups = len(DSW_GROUPS)
    c_specs, c_args = [], []
    for (window, dil), c in zip(DSW_GROUPS, caches):
        c_args.append(c.reshape(c.shape[0], b, window // dil, dil, 2, DSW_HEADS, HEAD_DIM))
        c_specs.append(pl.BlockSpec((None, None, window // dil, None, 2, DSW_HEADS, HEAD_DIM),
                                    lambda bi: (layer, bi, 0, 0, 0, 0, 0)))
    return pl.pallas_call(
        _dsw_decode_kernel,
        grid=(b,),
        in_specs=[pl.BlockSpec((None, n_groups, 3, DSW_HEADS, HEAD_DIM), lambda bi: (bi, 0, 0, 0, 0))] + c_specs,
        out_specs=[pl.BlockSpec((None, DSW_HEADS, HEAD_DIM), lambda bi: (bi, 0, 0)),
                   pl.BlockSpec((None, n_groups, 2, DSW_HEADS, HEAD_DIM), lambda bi: (bi, 0, 0, 0, 0))],
        out_shape=[jax.ShapeDtypeStruct((b, DSW_HEADS, HEAD_DIM), BF16),
                   jax.ShapeDtypeStruct((b, n_groups, 2, DSW_HEADS, HEAD_DIM), F32)],
        compiler_params=_params(1),
        name="dsw_decode",
    )(proj, *c_args)


def dsw_layer_sample(x, gain, weights, q_norm, k_norm, caches, layer):
    w_groups, _ = weights
    b, d = x.shape
    proj = jnp.stack([dsw_inproj(x.reshape(1, b, d), gain, jnp.stack([q_norm[g], k_norm[g]]), w_groups[g], 1, b)
                      .reshape(b, 3, DSW_HEADS, HEAD_DIM) for g in range(len(DSW_GROUPS))], axis=1)
    o, new_rows = dsw_decode(proj, caches, layer)
    return o.reshape(b, -1), new_rows


SHIFT_ROWS = 256


def _shift_kernel(c_ref, next_ref, new_ref, o_ref):
    rows = c_ref.shape[0]
    is_last = pl.program_id(2) == pl.num_programs(2) - 1
    o_ref[0:rows - 1] = c_ref[1:rows]

    @pl.when(is_last)
    def _():
        o_ref[rows - 1] = new_ref[...]

    @pl.when(jnp.logical_not(is_last))
    def _():
        o_ref[rows - 1] = next_ref[0]


def shift_caches(caches, new_rows_layers):
    new = jnp.stack(new_rows_layers)
    outs = []
    for g, c in enumerate(caches):
        n_layers, b, window = c.shape[:3]
        rows = min(SHIFT_ROWS, window)
        tile = (2, DSW_HEADS, HEAD_DIM)
        blk = pl.BlockSpec((None, None, rows) + tile, lambda l, bi, i: (l, bi, i, 0, 0, 0))
        nxt = pl.BlockSpec((None, None, 1) + tile,
                           lambda l, bi, i: (l, bi, jnp.minimum((i + 1) * rows, window - 1), 0, 0, 0))
        outs.append(pl.pallas_call(
            _shift_kernel,
            grid=(n_layers, b, window // rows),
            in_specs=[blk, nxt,
                      pl.BlockSpec((None, None, None) + tile, functools.partial(lambda l, bi, i, g: (l, bi, g, 0, 0, 0), g=g))],
            out_specs=blk,
            out_shape=jax.ShapeDtypeStruct(c.shape, c.dtype),
            compiler_params=_params(3),
            name="shift_cache",
        )(c, c, new))
    return outs


def kernel(x_prompt, x_sample, state_gdn, state_conv, cache_kv_w128, cache_kv_w512, cache_kv_w2048,
           norm_mix, norm_mlp, gdn_w_in, gdn_conv_w, gdn_a_log, gdn_dt_bias, gdn_o_norm, gdn_w_out,
           dsw_w_in, dsw_q_norm, dsw_k_norm, dsw_w_out, mlp_w_up, mlp_w_down):
    b, t, d = x_prompt.shape
    bs = x_sample.shape[0]
    depth = norm_mix.shape[0]
    caches = [cache_kv_w128, cache_kv_w512, cache_kv_w2048]
    yp = x_prompt.reshape(b * t, d)
    ys = x_sample.reshape(bs, d)
    p_gdn, p_conv, s_gdn, s_conv, p_projs, s_rows = [], [], [], [], [], []
    for i in range(depth):
        j = i // 2
        if i % 2 == 0:
            wts = prep_gdn_weights(gdn_w_in[j], gdn_w_out[j])
            par = (gdn_conv_w[j], gdn_a_log[j], gdn_dt_bias[j], gdn_o_norm[j])
            op, conv_p, state_p = gdn_layer_prompt(yp.reshape(b, t, d), norm_mix[i], wts, *par)
            os_, conv_s, state_s = gdn_layer_sample(ys, norm_mix[i], wts, *par, state_conv[j], state_gdn[j])
            p_gdn.append(state_p)
            p_conv.append(conv_p)
            s_gdn.append(state_s)
            s_conv.append(conv_s)
        else:
            wts = prep_dsw_weights(dsw_w_in[j], dsw_w_out[j])
            op, projs = dsw_layer_prompt(yp.reshape(b, t, d), norm_mix[i], wts, dsw_q_norm[j], dsw_k_norm[j])
            os_, rows = dsw_layer_sample(ys, norm_mix[i], wts, dsw_q_norm[j], dsw_k_norm[j], caches, j)
            p_projs.append(projs)
            s_rows.append(rows)
        w_up, w_down = mlp_w_up[i].astype(BF16), mlp_w_down[i].astype(BF16)
        yp = mixer_mlp(yp, op, wts[-1], norm_mlp[i], w_up, w_down, 512, 1024)
        ys = mixer_mlp(ys, os_, wts[-1], norm_mlp[i], w_up, w_down, bs, 1024)
    yp = yp.reshape(b, t, d)
    p_kv = dsw_prompt_caches(p_projs)
    s_kv = shift_caches(caches, s_rows)
    return (yp, ys.reshape(x_sample.shape),
            jnp.stack(p_gdn), jnp.stack(p_conv), p_kv[0], p_kv[1], p_kv[2],
            jnp.stack(s_gdn), jnp.stack(s_conv), s_kv[0], s_kv[1], s_kv[2])
```

```python
import functools

import jax
import jax.numpy as jnp
from jax import lax
from jax.experimental import pallas as pl
from jax.experimental.pallas import tpu as pltpu

F32 = jnp.float32
BF16 = jnp.bfloat16

EPS = 1e-6
NEG_INF = -1e30
LANES = 128
CONV_TAPS = 4
CHUNK = 64
HEAD_DIM = 128
DSW_BLOCK = 128
DSW_GROUPS = ((128, 1), (512, 4), (2048, 16))
VMEM_LIMIT = 56 * 1024 * 1024


def _params(n_axes, vmem=VMEM_LIMIT):
    return pltpu.CompilerParams(dimension_semantics=("arbitrary",) * n_axes,
                                vmem_limit_bytes=vmem)


def _sigmoid(x):
    return 0.5 + 0.5 * jnp.tanh(0.5 * x)


def _silu(x):
    h = 0.5 * x
    return h + h * jnp.tanh(h)


def _softplus(x):
    return jnp.maximum(x, 0.0) + jnp.log(1.0 + jnp.exp(-jnp.abs(x)))


def _rms_rows(x, gain_row):
    ms = jnp.mean(x * x, axis=-1, keepdims=True)
    return x * lax.rsqrt(ms + EPS) * gain_row


INPROJ_COLS = 512


def _gdn_inproj_kernel(x_ref, g_ref, w_ref, wg_ref, o_ref, og_ref, xn_ref):
    xn_ref[...] = _rms_rows(x_ref[...], g_ref[...]).astype(BF16)
    for lo in range(0, w_ref.shape[1], INPROJ_COLS):
        o_ref[:, lo:lo + INPROJ_COLS] = jnp.dot(
            xn_ref[...], w_ref[:, lo:lo + INPROJ_COLS], preferred_element_type=F32).astype(o_ref.dtype)
    og_ref[...] = jnp.dot(xn_ref[...], wg_ref[...], preferred_element_type=F32)


def gdn_inproj(x, gain, w_main, w_gate, out_dtype, tm):
    m, k = x.shape
    n, ng = w_main.shape[1], w_gate.shape[1]
    tm = min(tm, m)
    return pl.pallas_call(
        _gdn_inproj_kernel,
        grid=(m // tm,),
        in_specs=[pl.BlockSpec((tm, k), lambda i: (i, 0)),
                  pl.BlockSpec((1, k), lambda i: (0, 0)),
                  _resident((k, n)), _resident((k, ng))],
        out_specs=[pl.BlockSpec((tm, n), lambda i: (i, 0)),
                   pl.BlockSpec((tm, ng), lambda i: (i, 0))],
        out_shape=[jax.ShapeDtypeStruct((m, n), out_dtype), jax.ShapeDtypeStruct((m, ng), F32)],
        scratch_shapes=[pltpu.VMEM((tm, k), BF16)],
        compiler_params=_params(1),
        name="gdn_inproj",
    )(x, gain.reshape(1, k), w_main, w_gate)


def _mixer_mlp_kernel(x_ref, a_ref, wo_ref, g_ref, wu_ref, wd_ref, o_ref, xn_ref, *, tf):
    x1 = x_ref[...] + jnp.dot(a_ref[...], wo_ref[...], preferred_element_type=F32)
    xn_ref[...] = _rms_rows(x1, g_ref[...]).astype(BF16)
    o_ref[...] = x1
    for lo in range(0, wu_ref.shape[1], tf):
        h = jnp.dot(xn_ref[...], wu_ref[:, lo:lo + tf], preferred_element_type=F32)
        h = jnp.square(jnp.maximum(h, 0.0)).astype(BF16)
        o_ref[...] += jnp.dot(h, wd_ref[lo:lo + tf, :], preferred_element_type=F32)


def _resident(shape):
    return pl.BlockSpec(shape, lambda *_: (0,) * len(shape), pipeline_mode=pl.Buffered(1))


def mixer_mlp(x, a, w_out, gain, w_up, w_down, tm, tf):
    m, d = x.shape
    ka = a.shape[1]
    ff = w_up.shape[1]
    tm = min(tm, m)
    return pl.pallas_call(
        functools.partial(_mixer_mlp_kernel, tf=tf),
        grid=(m // tm,),
        in_specs=[pl.BlockSpec((tm, d), lambda i: (i, 0)),
                  pl.BlockSpec((tm, ka), lambda i: (i, 0)),
                  _resident((ka, d)), _resident((1, d)), _resident((d, ff)), _resident((ff, d))],
        out_specs=pl.BlockSpec((tm, d), lambda i: (i, 0)),
        out_shape=jax.ShapeDtypeStruct((m, d), F32),
        scratch_shapes=[pltpu.VMEM((tm, d), BF16)],
        compiler_params=_params(1),
        name="mixer_mlp",
    )(x, a, w_out, gain.reshape(1, d), w_up, w_down)


DEINTERLEAVE_STRIDE = 4
DSW_INPROJ_ROWS = 512
DSW_INPROJ_MIN_SUB = 64


def _dsw_inproj_kernel(x_ref, g_ref, hn_ref, w_ref, o_ref, xn_ref, head_scr, part_scr, *, dil):
    rows = x_ref.shape[0]
    sub = rows // dil
    width = w_ref.shape[2]
    xn_ref[...] = _rms_rows(x_ref[...], g_ref[...]).astype(BF16)
    for t in range(3):
        acc = jnp.dot(xn_ref[...], w_ref[t], preferred_element_type=F32)
        for h in range(width // HEAD_DIM):
            a = acc[:, h * HEAD_DIM:(h + 1) * HEAD_DIM]
            if t < 2:
                a = _rms_rows(a, hn_ref[t:t + 1, :])
            cols = slice(t * width + h * HEAD_DIM, t * width + (h + 1) * HEAD_DIM)
            if dil == 1:
                o_ref[0, :, cols] = a.astype(o_ref.dtype)
            elif dil <= DEINTERLEAVE_STRIDE:
                head_scr[h] = a
                for r in range(dil):
                    o_ref[r, :, cols] = head_scr[h, pl.ds(r, sub, stride=dil), :].astype(o_ref.dtype)
            else:
                st = DEINTERLEAVE_STRIDE
                part = rows // st
                head_scr[h] = a
                for q in range(st):
                    part_scr[h, q * part:(q + 1) * part, :] = head_scr[h, pl.ds(q, part, stride=st), :]
                for q in range(st):
                    for s in range(dil // st):
                        o_ref[st * s + q, :, cols] = part_scr[
                            h, pl.ds(q * part + s, sub, stride=dil // st), :].astype(o_ref.dtype)


def dsw_inproj(x, gain, head_gains, w3, dil, tile_rows):
    b, t, d = x.shape
    width = w3.shape[2]
    rows = min(tile_rows, t)
    return pl.pallas_call(
        functools.partial(_dsw_inproj_kernel, dil=dil),
        grid=(b, t // rows),
        in_specs=[pl.BlockSpec((None, rows, d), lambda bi, i: (bi, i, 0)),
                  pl.BlockSpec((1, d), lambda bi, i: (0, 0)),
                  pl.BlockSpec((2, HEAD_DIM), lambda bi, i: (0, 0)),
                  _resident((3, d, width))],
        out_specs=pl.BlockSpec((None, dil, rows // dil, 3 * width), lambda bi, i: (bi, 0, i, 0)),
        out_shape=jax.ShapeDtypeStruct((b, dil, t // dil, 3 * width), BF16),
        scratch_shapes=[pltpu.VMEM((rows, d), BF16),
                        pltpu.VMEM((width // HEAD_DIM, rows, HEAD_DIM), F32),
                        pltpu.VMEM((width // HEAD_DIM, rows, HEAD_DIM), F32)],
        compiler_params=_params(2),
        name="dsw_inproj",
    )(x, gain.reshape(1, d), head_gains, w3)


TRI_BASE = 8


def _pair_block_diag(m):
    n = m.shape[0]
    mb = m.astype(BF16)
    lane = lax.broadcasted_iota(jnp.int32, mb.shape, 1)
    zero = jnp.zeros_like(mb)
    return jnp.concatenate([jnp.where(lane < n, mb, zero), jnp.where(lane >= n, mb, zero)], axis=0)


def _pair_mm(u, w):
    return jnp.dot(u.astype(BF16), _pair_block_diag(w), preferred_element_type=F32)


def _tri_inverse_pairs(a_list, between=lambda: None):
    n = a_list[0].shape[0]
    ii = lax.broadcasted_iota(jnp.int32, (n, 2 * n), 0)
    jj = lax.broadcasted_iota(jnp.int32, (n, 2 * n), 1) % n
    same_block = lambda size: (ii // size) == (jj // size)
    ps = [jnp.where(same_block(TRI_BASE), a, 0.0) for a in a_list]
    xs = [jnp.where(ii == jj, 1.0, 0.0) - p for p in ps]
    for _ in range((TRI_BASE - 1).bit_length() - 1):
        ps = [_pair_mm(p, p) for p in ps]
        between()
        xs = [x + _pair_mm(x, p) for x, p in zip(xs, ps)]
        between()
    size = TRI_BASE
    while size < n:
        off = same_block(2 * size) & jnp.logical_not(same_block(size))
        ys = [_pair_mm(x, jnp.where(off, a, 0.0)) for x, a in zip(xs, a_list)]
        between()
        xs = [x - _pair_mm(y, x) for x, y in zip(xs, ys)]
        between()
        size *= 2
    return xs


def _gdn_prompt_kernel(qkv_ref, z_ref, gate_ref, cw_ref, alog_ref, dtb_ref, onorm_ref,
                       o_ref, s_out_ref,
                       xe_ref, qk_ref, v_ref, gate_scr, bb_ref, gcb_ref, tp_ref, pp_ref, s_ref, oscr_ref,
                       *, n_qk, n_v):
    tb = qkv_ref.shape[0]
    nchunk = tb // CHUNK
    t_idx = pl.program_id(1)
    halo = 8

    @pl.when(t_idx == 0)
    def _():
        xe_ref[:, halo - (CONV_TAPS - 1):halo, :] = jnp.zeros((xe_ref.shape[0], CONV_TAPS - 1, LANES), F32)
        s_ref[...] = jnp.zeros_like(s_ref)

    gate_scr[0] = _sigmoid(gate_ref[:, :LANES])
    g = -jnp.exp(alog_ref[...]) * _softplus(gate_ref[:, LANES:] + dtb_ref[...])
    pos = lax.broadcasted_iota(jnp.int32, (tb, LANES), 0) % CHUNK
    shift = 1
    while shift < CHUNK:
        g = g + jnp.where(pos >= shift, pltpu.roll(g, shift, axis=0), 0.0)
        shift *= 2
    gate_scr[1] = g
    for h in range(n_v):
        bb_ref[h] = jnp.broadcast_to(gate_scr[0, :, h:h + 1], (tb, LANES))
        gcb_ref[h] = jnp.broadcast_to(gate_scr[1, :, h:h + 1], (tb, LANES))

    def conv_block(cb):
        lo = cb * LANES
        xe_ref[cb, halo:halo + tb, :] = qkv_ref[:, lo:lo + LANES].astype(F32)
        y = cw_ref[0:1, lo:lo + LANES] * xe_ref[cb, halo - 3:halo - 3 + tb, :]
        for j in range(1, CONV_TAPS):
            y = y + cw_ref[j:j + 1, lo:lo + LANES] * xe_ref[cb, halo - 3 + j:halo - 3 + j + tb, :]
        y = _silu(y)
        xe_ref[cb, halo - 3:halo, :] = xe_ref[cb, halo + tb - 3:halo + tb, :]
        if cb < 2 * n_qk:
            y = y * lax.rsqrt(jnp.sum(y * y, axis=-1, keepdims=True) + EPS)
            if cb < n_qk:
                y = y * (HEAD_DIM ** -0.5)
            yb = y.astype(BF16)
            half = 0 if cb < n_qk else CHUNK
            for c in range(nchunk):
                qk_ref[cb % n_qk, c, half:half + CHUNK, :] = yb[c * CHUNK:(c + 1) * CHUNK, :]
        else:
            v_ref[cb - 2 * n_qk] = y

    for cb in range(2 * n_qk):
        conv_block(cb)
    pending = [functools.partial(conv_block, cb) for cb in range(2 * n_qk, 2 * n_qk + n_v)]

    def filler():
        if pending:
            pending.pop(0)()

    ii = lax.broadcasted_iota(jnp.int32, (CHUNK, 2 * CHUNK), 0)
    lane = lax.broadcasted_iota(jnp.int32, (CHUNK, 2 * CHUNK), 1)
    jj = lane % CHUNK
    causal = ii >= jj
    strict = ii > jj
    pairs_per_iter = 4

    for it in range(n_qk // pairs_per_iter):
        probs = [(it * pairs_per_iter + dj, c) for dj in range(pairs_per_iter) for c in range(nchunk)]
        grams, a_list = [], []
        for j, c in probs:
            qk = qk_ref[j, c]
            k2 = jnp.concatenate([qk[CHUNK:], qk[CHUNK:]], axis=0)
            grams.append(lax.dot_general(qk, k2, (((1,), (1,)), ((), ())), preferred_element_type=F32))
        filler()
        for (j, c), gram in zip(probs, grams):
            rows = slice(c * CHUNK, (c + 1) * CHUNK)
            gc = jnp.where(lane < CHUNK, gcb_ref[2 * j, rows, :], gcb_ref[2 * j + 1, rows, :])
            bt = jnp.where(lane < CHUNK, bb_ref[2 * j, rows, :], bb_ref[2 * j + 1, rows, :])
            gc_row = jnp.sum(jnp.where(ii == jj, gc, 0.0), axis=0, keepdims=True)
            decay = jnp.where(causal, jnp.exp(jnp.where(causal, gc - gc_row, 0.0)), 0.0)
            a_list.append(jnp.where(strict, bt * gram[CHUNK:] * decay, 0.0))
            pp_ref[j, c] = (gram[:CHUNK] * decay).astype(BF16)
        for (j, c), t_inv in zip(probs, _tri_inverse_pairs(a_list, filler)):
            tp_ref[j, c] = t_inv.astype(BF16)
    while pending:
        filler()

    def block_diag2(m):
        mb = m.astype(BF16)
        zero = jnp.zeros((CHUNK, HEAD_DIM), BF16)
        return jnp.concatenate([jnp.concatenate([mb[:, :HEAD_DIM], zero], axis=1),
                                jnp.concatenate([zero, mb[:, HEAD_DIM:]], axis=1)], axis=0)

    def finish_chunk(c):
        rows = slice(c * CHUNK, (c + 1) * CHUNK)
        for h in range(n_v):
            lo = h * HEAD_DIM
            o = _rms_rows(oscr_ref[h, rows, :], onorm_ref[...])
            o_ref[rows, lo:lo + HEAD_DIM] = (o * _silu(z_ref[rows, lo:lo + HEAD_DIM].astype(F32))).astype(o_ref.dtype)

    for c in range(nchunk):
        if c > 0:
            finish_chunk(c - 1)
        rows = slice(c * CHUNK, (c + 1) * CHUNK)
        heads = range(n_qk)
        pair = lambda ref, j: jnp.concatenate([ref[2 * j, rows, :], ref[2 * j + 1, rows, :]], axis=1)
        qks = [qk_ref[j, c] for j in heads]
        projs = [jnp.dot(qks[j], jnp.concatenate([s_ref[2 * j], s_ref[2 * j + 1]], axis=1).astype(BF16),
                         preferred_element_type=F32) for j in heads]
        gcs = [pair(gcb_ref, j) for j in heads]
        e_gcs = [jnp.exp(gc) for gc in gcs]
        rhss = [pair(bb_ref, j) * (pair(v_ref, j) - e_gcs[j] * projs[j][CHUNK:]) for j in heads]
        v_news = [jnp.dot(tp_ref[j, c], block_diag2(rhss[j]), preferred_element_type=F32) for j in heads]
        outs = [e_gcs[j] * projs[j][:CHUNK]
                + jnp.dot(pp_ref[j, c], block_diag2(v_news[j]), preferred_element_type=F32) for j in heads]
        g_lasts = [gc[CHUNK - 1:CHUNK, :] for gc in gcs]
        v_decs = [(jnp.exp(g_lasts[j] - gcs[j]) * v_news[j]).astype(BF16) for j in heads]
        d_states = [lax.dot_general(qks[j][CHUNK:], v_decs[j], (((0,), (0,)), ((), ())),
                                    preferred_element_type=F32) for j in heads]
        for j in heads:
            for e in range(2):
                lanes = slice(e * HEAD_DIM, (e + 1) * HEAD_DIM)
                oscr_ref[2 * j + e, rows, :] = outs[j][:, lanes]
                s_ref[2 * j + e] = jnp.exp(g_lasts[j][:, lanes]) * s_ref[2 * j + e] + d_states[j][:, lanes]
    finish_chunk(nchunk - 1)

    @pl.when(t_idx == pl.num_programs(1) - 1)
    def _():
        s_out_ref[...] = s_ref[...]


def gdn_prompt(qkvz, gates, conv_w, a_log, dt_bias, o_norm, n_qk, n_v, tb):
    b, t, _ = qkvz.shape
    assert n_v == 2 * n_qk and 2 * CHUNK == LANES
    conv_dim = (2 * n_qk + n_v) * HEAD_DIM
    v_dim = n_v * HEAD_DIM
    tb = min(tb, t)
    nchunk = tb // CHUNK
    pad = lambda p: jnp.zeros((1, LANES), F32).at[0, :n_v].set(p.astype(F32))
    return pl.pallas_call(
        functools.partial(_gdn_prompt_kernel, n_qk=n_qk, n_v=n_v),
        grid=(b, t // tb),
        in_specs=[pl.BlockSpec((None, tb, conv_dim), lambda bi, ti: (bi, ti, 0)),
                  pl.BlockSpec((None, tb, v_dim), lambda bi, ti: (bi, ti, conv_dim // v_dim)),
                  pl.BlockSpec((None, tb, 2 * LANES), lambda bi, ti: (bi, ti, 0)),
                  pl.BlockSpec((CONV_TAPS, conv_dim), lambda bi, ti: (0, 0)),
                  pl.BlockSpec((1, LANES), lambda bi, ti: (0, 0)),
                  pl.BlockSpec((1, LANES), lambda bi, ti: (0, 0)),
                  pl.BlockSpec((1, HEAD_DIM), lambda bi, ti: (0, 0))],
        out_specs=[pl.BlockSpec((None, tb, v_dim), lambda bi, ti: (bi, ti, 0)),
                   pl.BlockSpec((None, n_v, HEAD_DIM, HEAD_DIM), lambda bi, ti: (bi, 0, 0, 0))],
        out_shape=[jax.ShapeDtypeStruct((b, t, v_dim), BF16),
                   jax.ShapeDtypeStruct((b, n_v, HEAD_DIM, HEAD_DIM), F32)],
        scratch_shapes=[pltpu.VMEM((2 * n_qk + n_v, tb + 8, LANES), F32),
                        pltpu.VMEM((n_qk, nchunk, 2 * CHUNK, HEAD_DIM), BF16),
                        pltpu.VMEM((n_v, tb, HEAD_DIM), F32),
                        pltpu.VMEM((2, tb, LANES), F32),
                        pltpu.VMEM((n_v, tb, LANES), F32),
                        pltpu.VMEM((n_v, tb, LANES), F32),
                        pltpu.VMEM((n_qk, nchunk, CHUNK, 2 * CHUNK), BF16),
                        pltpu.VMEM((n_qk, nchunk, CHUNK, 2 * CHUNK), BF16),
                        pltpu.VMEM((n_v, HEAD_DIM, HEAD_DIM), F32),
                        pltpu.VMEM((n_v, tb, HEAD_DIM), F32)],
        compiler_params=_params(2),
        name="gdn_prompt",
    )(qkvz, qkvz, gates, conv_w, pad(a_log), pad(dt_bias), o_norm.reshape(1, HEAD_DIM))


GDN_QK_HEADS = 8
GDN_V_HEADS = 16
GDN_CONV_DIM = (2 * GDN_QK_HEADS + GDN_V_HEADS) * HEAD_DIM
GDN_V_DIM = GDN_V_HEADS * HEAD_DIM


def prep_gdn_weights(w_in, w_out):
    d = w_in.shape[0]
    main = GDN_CONV_DIM + GDN_V_DIM
    w_main = w_in[:, :main].astype(BF16)
    w_gate = jnp.zeros((d, 2 * LANES), F32)
    w_gate = w_gate.at[:, :GDN_V_HEADS].set(w_in[:, main:main + GDN_V_HEADS])
    w_gate = w_gate.at[:, LANES:LANES + GDN_V_HEADS].set(w_in[:, main + GDN_V_HEADS:])
    return w_main, w_gate.astype(BF16), w_out.astype(BF16)


def gdn_layer_prompt(x, gain, weights, conv_w, a_log, dt_bias, o_norm):
    w_main, w_gate, _ = weights
    b, t, d = x.shape
    qkvz, gates = gdn_inproj(x.reshape(b * t, d), gain, w_main, w_gate, BF16, 512)
    qkvz = qkvz.reshape(b, t, -1)
    o, state = gdn_prompt(qkvz, gates.reshape(b, t, -1), conv_w, a_log, dt_bias, o_norm,
                          GDN_QK_HEADS, GDN_V_HEADS, 256)
    conv_tail = qkvz[:, t - (CONV_TAPS - 1):, :GDN_CONV_DIM].astype(F32)
    return o.reshape(b * t, -1), conv_tail, state


def _gdn_step_kernel(qkvz_ref, gate_ref, conv_ref, s_ref, cw_ref, alog_ref, dtb_ref, onorm_ref,
                     o_ref, convn_ref, sn_ref, *, n_qk, n_v):
    rep = n_v // n_qk
    conv_dim = (2 * n_qk + n_v) * HEAD_DIM
    x = qkvz_ref[:, :conv_dim]
    y = cw_ref[CONV_TAPS - 1:CONV_TAPS, :] * x
    for j in range(CONV_TAPS - 1):
        y = y + cw_ref[j:j + 1, :] * conv_ref[j:j + 1, :]
    y = _silu(y)
    convn_ref[0:CONV_TAPS - 2, :] = conv_ref[1:CONV_TAPS - 1, :]
    convn_ref[CONV_TAPS - 2:CONV_TAPS - 1, :] = x

    beta = _sigmoid(gate_ref[:, :LANES])
    g = -jnp.exp(alog_ref[...]) * _softplus(gate_ref[:, LANES:] + dtb_ref[...])
    eye = (lax.broadcasted_iota(jnp.int32, (HEAD_DIM, HEAD_DIM), 0)
           == lax.broadcasted_iota(jnp.int32, (HEAD_DIM, HEAD_DIM), 1))

    def column(row):
        return jnp.sum(jnp.where(eye, row, 0.0), axis=1, keepdims=True)

    def l2(row):
        return row * lax.rsqrt(jnp.sum(row * row, axis=-1, keepdims=True) + EPS)

    for j in range(n_qk):
        q_col = column(l2(y[:, j * HEAD_DIM:(j + 1) * HEAD_DIM]) * (HEAD_DIM ** -0.5))
        k_col = column(l2(y[:, (n_qk + j) * HEAD_DIM:(n_qk + j + 1) * HEAD_DIM]))
        for e in range(rep):
            h = rep * j + e
            lo = h * HEAD_DIM
            v = y[:, 2 * n_qk * HEAD_DIM + lo:2 * n_qk * HEAD_DIM + lo + HEAD_DIM]
            s = s_ref[h]
            e_g = jnp.exp(g[:, h:h + 1])
            k_s = jnp.sum(s * k_col, axis=0, keepdims=True)
            v_new = beta[:, h:h + 1] * (v - e_g * k_s)
            s_new = e_g * s + k_col * v_new
            sn_ref[h] = s_new
            o = _rms_rows(jnp.sum(s_new * q_col, axis=0, keepdims=True), onorm_ref[...])
            z = qkvz_ref[:, conv_dim + lo:conv_dim + lo + HEAD_DIM]
            o_ref[:, lo:lo + HEAD_DIM] = (o * _silu(z)).astype(o_ref.dtype)


def gdn_step(qkvz, gates, conv_state, state, conv_w, a_log, dt_bias, o_norm, n_qk, n_v):
    b = qkvz.shape[0]
    conv_dim = (2 * n_qk + n_v) * HEAD_DIM
    v_dim = n_v * HEAD_DIM
    pad = lambda p: jnp.zeros((1, LANES), F32).at[0, :n_v].set(p.astype(F32))
    row = lambda n: pl.BlockSpec((None, 1, n), lambda bi: (bi, 0, 0))
    const = lambda shape: pl.BlockSpec(shape, lambda bi: (0,) * len(shape))
    st = pl.BlockSpec((None, n_v, HEAD_DIM, HEAD_DIM), lambda bi: (bi, 0, 0, 0))
    cv = pl.BlockSpec((None, CONV_TAPS - 1, conv_dim), lambda bi: (bi, 0, 0))
    return pl.pallas_call(
        functools.partial(_gdn_step_kernel, n_qk=n_qk, n_v=n_v),
        grid=(b,),
        in_specs=[row(conv_dim + v_dim), row(2 * LANES), cv, st,
                  const((CONV_TAPS, conv_dim)), const((1, LANES)), const((1, LANES)), const((1, HEAD_DIM))],
        out_specs=[row(v_dim), cv, st],
        out_shape=[jax.ShapeDtypeStruct((b, 1, v_dim), BF16),
                   jax.ShapeDtypeStruct(conv_state.shape, F32),
                   jax.ShapeDtypeStruct(state.shape, F32)],
        compiler_params=_params(1),
        name="gdn_step",
    )(qkvz.reshape(b, 1, -1), gates.reshape(b, 1, -1), conv_state, state,
      conv_w, pad(a_log), pad(dt_bias), o_norm.reshape(1, HEAD_DIM))


def gdn_layer_sample(x, gain, weights, conv_w, a_log, dt_bias, o_norm, conv_state, state):
    w_main, w_gate, _ = weights
    b = x.shape[0]
    qkvz, gates = gdn_inproj(x, gain, w_main, w_gate, F32, b)
    o, conv_new, state_new = gdn_step(qkvz, gates, conv_state, state, conv_w, a_log, dt_bias, o_norm,
                                      GDN_QK_HEADS, GDN_V_HEADS)
    return o.reshape(b, -1), conv_new, state_new


DSW_HEADS = 8
DSW_WIDTH = DSW_HEADS * HEAD_DIM
DSW_TILE = DSW_BLOCK * max(d for _, d in DSW_GROUPS)
DSW_BLOCKS_PER_ITER = 8


def _dsw_attn_kernel(*refs):
    n_groups = len(DSW_GROUPS)
    ins = [refs[5 * g:5 * g + 5] for g in range(n_groups)]
    o_ref = refs[5 * n_groups]
    scratch = refs[5 * n_groups + 1:]
    kf_refs, vf_refs = scratch[:n_groups], scratch[n_groups:2 * n_groups]
    og_ref, lg_ref = scratch[2 * n_groups:]
    first_tile = pl.program_id(2) == 0
    blk = DSW_BLOCK
    qi = lax.broadcasted_iota(jnp.int32, (blk, 2 * blk), 0)
    ki = lax.broadcasted_iota(jnp.int32, (blk, 2 * blk), 1)
    band = (ki >= qi) & (ki <= qi + blk)
    scale = HEAD_DIM ** -0.5

    for g, (_, dil) in enumerate(DSW_GROUPS):
        q_ref, kc_ref, vc_ref, kp_ref, vp_ref = ins[g]
        kf_ref, vf_ref = kf_refs[g], vf_refs[g]
        per_res = q_ref.shape[1] // blk
        kf_ref[:, :blk, :] = kp_ref[...]
        kf_ref[:, blk:, :] = kc_ref[...]
        vf_ref[:, :blk, :] = vp_ref[...]
        vf_ref[:, blk:, :] = vc_ref[...]

        def blocks_body(it, carry, *, g=g, dil=dil, per_res=per_res, q_ref=q_ref, kf_ref=kf_ref, vf_ref=vf_ref):
            ids = [it * DSW_BLOCKS_PER_ITER + i for i in range(DSW_BLOCKS_PER_ITER)]
            rs = [bi // per_res for bi in ids]
            ms = [bi % per_res for bi in ids]
            row0s = [pl.multiple_of(m * blk, blk) for m in ms]
            ss = [lax.dot_general(q_ref[r, pl.ds(row0, blk), :], kf_ref[r, pl.ds(row0, 2 * blk), :],
                                  (((1,), (1,)), ((), ())), preferred_element_type=F32) * scale
                  for r, row0 in zip(rs, row0s)]
            ss = [jnp.where(band & ((ki >= blk) | (m > 0) | jnp.logical_not(first_tile)), s, NEG_INF)
                  for s, m in zip(ss, ms)]
            mxs = [jnp.max(s, axis=-1, keepdims=True) for s in ss]
            ps = [jnp.exp(s - mx) for s, mx in zip(ss, mxs)]
            dens = [jnp.sum(p, axis=-1, keepdims=True) for p in ps]
            os_ = [jnp.dot(p.astype(BF16), vf_ref[r, pl.ds(row0, 2 * blk), :], preferred_element_type=F32)
                   for p, r, row0 in zip(ps, rs, row0s)]
            for o, den, mx, r, row0 in zip(os_, dens, mxs, rs, row0s):
                o = o * (1.0 / den)
                lse = jnp.broadcast_to(mx + jnp.log(den), (blk, LANES))
                start = row0 * dil + r
                rows = pl.ds(start, blk) if dil == 1 else pl.ds(start, blk, stride=dil)
                og_ref[g, rows, :] = o
                lg_ref[g, rows, :] = lse
            return carry

        lax.fori_loop(0, dil * per_res // DSW_BLOCKS_PER_ITER, blocks_body, 0)

    top = lg_ref[0]
    for g in range(1, n_groups):
        top = jnp.maximum(top, lg_ref[g])
    num = jnp.zeros_like(top)
    den = jnp.zeros_like(top)
    for g in range(n_groups):
        w = jnp.exp(lg_ref[g] - top)
        num = num + w * og_ref[g]
        den = den + w
    o_ref[...] = (num * (1.0 / den)).astype(o_ref.dtype)


def dsw_attn(projs):
    b = projs[0].shape[0]
    t = projs[0].shape[1] * projs[0].shape[2]
    tile = DSW_TILE
    heads = DSW_HEADS
    in_specs, args, kv_scratch = [], [], []
    for (_, dil), p in zip(DSW_GROUPS, projs):
        rows = tile // dil
        per_res = rows // DSW_BLOCK
        cur = lambda col: pl.BlockSpec((None, dil, rows, HEAD_DIM),
                                       functools.partial(lambda bi, h, n, col: (bi, 0, n, col * heads + h), col=col))
        prev = lambda col: pl.BlockSpec(
            (None, dil, DSW_BLOCK, HEAD_DIM),
            functools.partial(lambda bi, h, n, col, per_res: (bi, 0, jnp.maximum(n * per_res - 1, 0), col * heads + h),
                              col=col, per_res=per_res))
        in_specs += [cur(0), cur(1), cur(2), prev(1), prev(2)]
        args += [p] * 5
        kv_scratch.append(pltpu.VMEM((dil, DSW_BLOCK + rows, HEAD_DIM), BF16))
    n_groups = len(DSW_GROUPS)
    return pl.pallas_call(
        _dsw_attn_kernel,
        grid=(b, heads, t // tile),
        in_specs=in_specs,
        out_specs=pl.BlockSpec((None, tile, HEAD_DIM), lambda bi, h, n: (bi, n, h)),
        out_shape=jax.ShapeDtypeStruct((b, t, heads * HEAD_DIM), BF16),
        scratch_shapes=kv_scratch + kv_scratch + [pltpu.VMEM((n_groups, tile, HEAD_DIM), F32),
                                                  pltpu.VMEM((n_groups, tile, LANES), F32)],
        compiler_params=_params(3),
        name="dsw_attn",
    )(*args)


def prep_dsw_weights(w_in, w_out):
    d = w_in.shape[0]
    w = w_in.reshape(d, len(DSW_GROUPS), 3, DSW_WIDTH).transpose(1, 2, 0, 3).astype(BF16)
    return [w[g] for g in range(len(DSW_GROUPS))], w_out.astype(BF16)


def dsw_layer_prompt(x, gain, weights, q_norm, k_norm):
    w_groups, _ = weights
    b, t, d = x.shape
    projs = [dsw_inproj(x, gain, jnp.stack([q_norm[g], k_norm[g]]), w_groups[g], dil,
                        max(DSW_INPROJ_ROWS, DSW_INPROJ_MIN_SUB * dil))
             for g, (_, dil) in enumerate(DSW_GROUPS)]
    return dsw_attn(projs).reshape(b * t, -1), projs


CACHE_RESIDUES = 4


def _dsw_cache_kernel(*refs):
    *p_refs, o_ref, slab_scr = refs
    layer = pl.program_id(0)
    rows, n_res = o_ref.shape[:2]
    for li in range(len(p_refs) // 2):
        @pl.when(layer == li)
        def _(k_ref=p_refs[2 * li], v_ref=p_refs[2 * li + 1]):
            for rr in range(n_res):
                for kv, p_ref in enumerate((k_ref, v_ref)):
                    slab = slab_scr.at[2 * rr + kv]
                    slab[...] = p_ref[rr].astype(o_ref.dtype).reshape(rows, DSW_HEADS, HEAD_DIM)
                    o_ref[:, rr, kv] = slab[...]


def dsw_prompt_caches(projs_layers):
    n_layers = len(projs_layers)
    outs = []
    for g, (window, dil) in enumerate(DSW_GROUPS):
        ps = [pl_[g] for pl_ in projs_layers]
        b, _, sub, _ = ps[0].shape
        last = sub // DSW_BLOCK - 1
        n_res = min(dil, CACHE_RESIDUES)

        def in_map(l, bi, rb, *, li, col):
            before, after = l < li, l > li
            pick = lambda lo, x, hi: jnp.where(before, lo, jnp.where(after, hi, x))
            return (pick(0, bi, b - 1), pick(0, rb, dil // n_res - 1), last, col)

        out = pl.pallas_call(
            _dsw_cache_kernel,
            grid=(n_layers, b, dil // n_res),
            in_specs=[pl.BlockSpec((None, n_res, DSW_BLOCK, DSW_WIDTH), functools.partial(in_map, li=li, col=col))
                      for li in range(n_layers) for col in (1, 2)],
            out_specs=pl.BlockSpec((None, None, DSW_BLOCK, n_res, 2, DSW_HEADS, HEAD_DIM),
                                   lambda l, bi, rb: (l, bi, 0, rb, 0, 0, 0)),
            out_shape=jax.ShapeDtypeStruct((n_layers, b, DSW_BLOCK, dil, 2, DSW_HEADS, HEAD_DIM), F32),
            scratch_shapes=[pltpu.VMEM((2 * n_res, DSW_BLOCK, DSW_HEADS, HEAD_DIM), F32)],
            compiler_params=_params(3),
            name="dsw_prompt_cache",
        )(*[p for p in ps for _ in (1, 2)])
        outs.append(out.reshape(n_layers, b, window, 2, DSW_HEADS, HEAD_DIM))
    return outs


def _dsw_decode_kernel(p_ref, *refs):
    n_groups = len(DSW_GROUPS)
    c_refs = refs[:n_groups]
    o_ref, new_ref = refs[n_groups:]
    scale = HEAD_DIM ** -0.5
    outs, lses = [], []
    for g in range(n_groups):
        q = p_ref[g, 0].astype(F32)
        k_new = p_ref[g, 1].astype(F32)
        v_new = p_ref[g, 2].astype(F32)
        new_ref[g, 0] = k_new
        new_ref[g, 1] = v_new
        s = jnp.sum(c_refs[g][:, 0] * q[None], axis=-1, keepdims=True) * scale
        s_new = jnp.sum(k_new * q, axis=-1, keepdims=True) * scale
        mx = jnp.maximum(jnp.max(s, axis=0), s_new)
        p = jnp.exp(s - mx[None])
        p_new = jnp.exp(s_new - mx)
        den = jnp.sum(p, axis=0) + p_new
        outs.append((jnp.sum(p * c_refs[g][:, 1], axis=0) + p_new * v_new) * (1.0 / den))
        lses.append(mx + jnp.log(den))
    top = functools.reduce(jnp.maximum, lses)
    ws = [jnp.exp(l - top) for l in lses]
    num = sum(w * o for w, o in zip(ws, outs))
    o_ref[...] = (num * (1.0 / sum(ws))).astype(o_ref.dtype)


def dsw_decode(proj, caches, layer):
    b = proj.shape[0]
    n_groups = len(DSW_GROUPS)
    c_specs, c_args = [], []
    for (window, dil), c in zip(DSW_GROUPS, caches):
        c_args.append(c.reshape(c.shape[0], b, window // dil, dil, 2, DSW_HEADS, HEAD_DIM))
        c_specs.append(pl.BlockSpec((None, None, window // dil, None, 2, DSW_HEADS, HEAD_DIM),
                                    lambda bi: (layer, bi, 0, 0, 0, 0, 0)))
    return pl.pallas_call(
        _dsw_decode_kernel,
        grid=(b,),
        in_specs=[pl.BlockSpec((None, n_groups, 3, DSW_HEADS, HEAD_DIM), lambda bi: (bi, 0, 0, 0, 0))] + c_specs,
        out_specs=[pl.BlockSpec((None, DSW_HEADS, HEAD_DIM), lambda bi: (bi, 0, 0)),
                   pl.BlockSpec((None, n_groups, 2, DSW_HEADS, HEAD_DIM), lambda bi: (bi, 0, 0, 0, 0))],
        out_shape=[jax.ShapeDtypeStruct((b, DSW_HEADS, HEAD_DIM), BF16),
                   jax.ShapeDtypeStruct((b, n_groups, 2, DSW_HEADS, HEAD_DIM), F32)],
        compiler_params=_params(1),
        name="dsw_decode",
    )(proj, *c_args)


def dsw_layer_sample(x, gain, weights, q_norm, k_norm, caches, layer):
    w_groups, _ = weights
    b, d = x.shape
    proj = jnp.stack([dsw_inproj(x.reshape(1, b, d), gain, jnp.stack([q_norm[g], k_norm[g]]), w_groups[g], 1, b)
                      .reshape(b, 3, DSW_HEADS, HEAD_DIM) for g in range(len(DSW_GROUPS))], axis=1)
    o, new_rows = dsw_decode(proj, caches, layer)
    return o.reshape(b, -1), new_rows


SHIFT_ROWS = 256


def _shift_kernel(c_ref, next_ref, new_ref, o_ref):
    rows = c_ref.shape[0]
    is_last = pl.program_id(2) == pl.num_programs(2) - 1
    o_ref[0:rows - 1] = c_ref[1:rows]

    @pl.when(is_last)
    def _():
        o_ref[rows - 1] = new_ref[...]

    @pl.when(jnp.logical_not(is_last))
    def _():
        o_ref[rows - 1] = next_ref[0]


def shift_caches(caches, new_rows_layers):
    new = jnp.stack(new_rows_layers)
    outs = []
    for g, c in enumerate(caches):
        n_layers, b, window = c.shape[:3]
        rows = min(SHIFT_ROWS, window)
        tile = (2, DSW_HEADS, HEAD_DIM)
        blk = pl.BlockSpec((None, None, rows) + tile, lambda l, bi, i: (l, bi, i, 0, 0, 0))
        nxt = pl.BlockSpec((None, None, 1) + tile,
                           lambda l, bi, i: (l, bi, jnp.minimum((i + 1) * rows, window - 1), 0, 0, 0))
        outs.append(pl.pallas_call(
            _shift_kernel,
            grid=(n_layers, b, window // rows),
            in_specs=[blk, nxt,
                      pl.BlockSpec((None, None, None) + tile, functools.partial(lambda l, bi, i, g: (l, bi, g, 0, 0, 0), g=g))],
            out_specs=blk,
            out_shape=jax.ShapeDtypeStruct(c.shape, c.dtype),
            compiler_params=_params(3),
            name="shift_cache",
        )(c, c, new))
    return outs


def kernel(x_prompt, x_sample, state_gdn, state_conv, cache_kv_w128, cache_kv_w512, cache_kv_w2048,
           norm_mix, norm_mlp, gdn_w_in, gdn_conv_w, gdn_a_log, gdn_dt_bias, gdn_o_norm, gdn_w_out,
           dsw_w_in, dsw_q_norm, dsw_k_norm, dsw_w_out, mlp_w_up, mlp_w_down):
    b, t, d = x_prompt.shape
    bs = x_sample.shape[0]
    depth = norm_mix.shape[0]
    caches = [cache_kv_w128, cache_kv_w512, cache_kv_w2048]
    yp = x_prompt.reshape(b * t, d)
    ys = x_sample.reshape(bs, d)
    p_gdn, p_conv, s_gdn, s_conv, p_projs, s_rows = [], [], [], [], [], []
    for i in range(depth):
        j = i // 2
        if i % 2 == 0:
            wts = prep_gdn_weights(gdn_w_in[j], gdn_w_out[j])
            par = (gdn_conv_w[j], gdn_a_log[j], gdn_dt_bias[j], gdn_o_norm[j])
            op, conv_p, state_p = gdn_layer_prompt(yp.reshape(b, t, d), norm_mix[i], wts, *par)
            os_, conv_s, state_s = gdn_layer_sample(ys, norm_mix[i], wts, *par, state_conv[j], state_gdn[j])
            p_gdn.append(state_p)
            p_conv.append(conv_p)
            s_gdn.append(state_s)
            s_conv.append(conv_s)
        else:
            wts = prep_dsw_weights(dsw_w_in[j], dsw_w_out[j])
            op, projs = dsw_layer_prompt(yp.reshape(b, t, d), norm_mix[i], wts, dsw_q_norm[j], dsw_k_norm[j])
            os_, rows = dsw_layer_sample(ys, norm_mix[i], wts, dsw_q_norm[j], dsw_k_norm[j], caches, j)
            p_projs.append(projs)
            s_rows.append(rows)
        w_up, w_down = mlp_w_up[i].astype(BF16), mlp_w_down[i].astype(BF16)
        yp = mixer_mlp(yp, op, wts[-1], norm_mlp[i], w_up, w_down, 512, 1024)
        ys = mixer_mlp(ys, os_, wts[-1], norm_mlp[i], w_up, w_down, bs, 1024)
    yp = yp.reshape(b, t, d)
    p_kv = dsw_prompt_caches(p_projs)
    s_kv = shift_caches(caches, s_rows)
    return (yp, ys.reshape(x_sample.shape),
            jnp.stack(p_gdn), jnp.stack(p_conv), p_kv[0], p_kv[1], p_kv[2],
            jnp.stack(s_gdn), jnp.stack(s_conv), s_kv[0], s_kv[1], s_kv[2])
```

```python
import functools

import jax
import jax.numpy as jnp
from jax import lax
from jax.experimental import pallas as pl
from jax.experimental.pallas import tpu as pltpu

F32 = jnp.float32
BF16 = jnp.bfloat16

EPS = 1e-6
NEG_INF = -1e30
LANES = 128
CONV_TAPS = 4
CHUNK = 64
HEAD_DIM = 128
DSW_BLOCK = 128
DSW_GROUPS = ((128, 1), (512, 4), (2048, 16))
VMEM_LIMIT = 56 * 1024 * 1024


def _params(n_axes, vmem=VMEM_LIMIT):
    return pltpu.CompilerParams(dimension_semantics=("arbitrary",) * n_axes,
                                vmem_limit_bytes=vmem)


def _sigmoid(x):
    return 0.5 + 0.5 * jnp.tanh(0.5 * x)


def _silu(x):
    h = 0.5 * x
    return h + h * jnp.tanh(h)


def _softplus(x):
    return jnp.maximum(x, 0.0) + jnp.log(1.0 + jnp.exp(-jnp.abs(x)))


def _rms_rows(x, gain_row):
    ms = jnp.mean(x * x, axis=-1, keepdims=True)
    return x * lax.rsqrt(ms + EPS) * gain_row


INPROJ_COLS = 512


def _gdn_inproj_kernel(x_ref, g_ref, w_ref, wg_ref, o_ref, og_ref, xn_ref):
    xn_ref[...] = _rms_rows(x_ref[...], g_ref[...]).astype(BF16)
    for lo in range(0, w_ref.shape[1], INPROJ_COLS):
        o_ref[:, lo:lo + INPROJ_COLS] = jnp.dot(
            xn_ref[...], w_ref[:, lo:lo + INPROJ_COLS], preferred_element_type=F32).astype(o_ref.dtype)
    og_ref[...] = jnp.dot(xn_ref[...], wg_ref[...], preferred_element_type=F32)


def gdn_inproj(x, gain, w_main, w_gate, out_dtype, tm):
    m, k = x.shape
    n, ng = w_main.shape[1], w_gate.shape[1]
    tm = min(tm, m)
    return pl.pallas_call(
        _gdn_inproj_kernel,
        grid=(m // tm,),
        in_specs=[pl.BlockSpec((tm, k), lambda i: (i, 0)),
                  pl.BlockSpec((1, k), lambda i: (0, 0)),
                  _resident((k, n)), _resident((k, ng))],
        out_specs=[pl.BlockSpec((tm, n), lambda i: (i, 0)),
                   pl.BlockSpec((tm, ng), lambda i: (i, 0))],
        out_shape=[jax.ShapeDtypeStruct((m, n), out_dtype), jax.ShapeDtypeStruct((m, ng), F32)],
        scratch_shapes=[pltpu.VMEM((tm, k), BF16)],
        compiler_params=_params(1),
        name="gdn_inproj",
    )(x, gain.reshape(1, k), w_main, w_gate)


def _mixer_mlp_kernel(x_ref, a_ref, wo_ref, g_ref, wu_ref, wd_ref, o_ref, xn_ref, *, tf):
    x1 = x_ref[...] + jnp.dot(a_ref[...], wo_ref[...], preferred_element_type=F32)
    xn_ref[...] = _rms_rows(x1, g_ref[...]).astype(BF16)
    o_ref[...] = x1
    for lo in range(0, wu_ref.shape[1], tf):
        h = jnp.dot(xn_ref[...], wu_ref[:, lo:lo + tf], preferred_element_type=F32)
        h = jnp.square(jnp.maximum(h, 0.0)).astype(BF16)
        o_ref[...] += jnp.dot(h, wd_ref[lo:lo + tf, :], preferred_element_type=F32)


def _resident(shape):
    return pl.BlockSpec(shape, lambda *_: (0,) * len(shape), pipeline_mode=pl.Buffered(1))


def mixer_mlp(x, a, w_out, gain, w_up, w_down, tm, tf):
    m, d = x.shape
    ka = a.shape[1]
    ff = w_up.shape[1]
    tm = min(tm, m)
    return pl.pallas_call(
        functools.partial(_mixer_mlp_kernel, tf=tf),
        grid=(m // tm,),
        in_specs=[pl.BlockSpec((tm, d), lambda i: (i, 0)),
                  pl.BlockSpec((tm, ka), lambda i: (i, 0)),
                  _resident((ka, d)), _resident((1, d)), _resident((d, ff)), _resident((ff, d))],
        out_specs=pl.BlockSpec((tm, d), lambda i: (i, 0)),
        out_shape=jax.ShapeDtypeStruct((m, d), F32),
        scratch_shapes=[pltpu.VMEM((tm, d), BF16)],
        compiler_params=_params(1),
        name="mixer_mlp",
    )(x, a, w_out, gain.reshape(1, d), w_up, w_down)


DEINTERLEAVE_STRIDE = 4
DSW_INPROJ_ROWS = 512
DSW_INPROJ_MIN_SUB = 64


def _dsw_inproj_kernel(x_ref, g_ref, hn_ref, w_ref, o_ref, xn_ref, head_scr, part_scr, *, dil):
    rows = x_ref.shape[0]
    sub = rows // dil
    width = w_ref.shape[2]
    xn_ref[...] = _rms_rows(x_ref[...], g_ref[...]).astype(BF16)
    for t in range(3):
        acc = jnp.dot(xn_ref[...], w_ref[t], preferred_element_type=F32)
        for h in range(width // HEAD_DIM):
            a = acc[:, h * HEAD_DIM:(h + 1) * HEAD_DIM]
            if t < 2:
                a = _rms_rows(a, hn_ref[t:t + 1, :])
            cols = slice(t * width + h * HEAD_DIM, t * width + (h + 1) * HEAD_DIM)
            if dil == 1:
                o_ref[0, :, cols] = a.astype(o_ref.dtype)
            elif dil <= DEINTERLEAVE_STRIDE:
                head_scr[h] = a
                for r in range(dil):
                    o_ref[r, :, cols] = head_scr[h, pl.ds(r, sub, stride=dil), :].astype(o_ref.dtype)
            else:
                st = DEINTERLEAVE_STRIDE
                part = rows // st
                head_scr[h] = a
                for q in range(st):
                    part_scr[h, q * part:(q + 1) * part, :] = head_scr[h, pl.ds(q, part, stride=st), :]
                for q in range(st):
                    for s in range(dil // st):
                        o_ref[st * s + q, :, cols] = part_scr[
                            h, pl.ds(q * part + s, sub, stride=dil // st), :].astype(o_ref.dtype)


def dsw_inproj(x, gain, head_gains, w3, dil, tile_rows):
    b, t, d = x.shape
    width = w3.shape[2]
    rows = min(tile_rows, t)
    return pl.pallas_call(
        functools.partial(_dsw_inproj_kernel, dil=dil),
        grid=(b, t // rows),
        in_specs=[pl.BlockSpec((None, rows, d), lambda bi, i: (bi, i, 0)),
                  pl.BlockSpec((1, d), lambda bi, i: (0, 0)),
                  pl.BlockSpec((2, HEAD_DIM), lambda bi, i: (0, 0)),
                  _resident((3, d, width))],
        out_specs=pl.BlockSpec((None, dil, rows // dil, 3 * width), lambda bi, i: (bi, 0, i, 0)),
        out_shape=jax.ShapeDtypeStruct((b, dil, t // dil, 3 * width), BF16),
        scratch_shapes=[pltpu.VMEM((rows, d), BF16),
                        pltpu.VMEM((width // HEAD_DIM, rows, HEAD_DIM), F32),
                        pltpu.VMEM((width // HEAD_DIM, rows, HEAD_DIM), F32)],
        compiler_params=_params(2),
        name="dsw_inproj",
    )(x, gain.reshape(1, d), head_gains, w3)


TRI_BASE = 8


def _pair_block_diag(m):
    n = m.shape[0]
    mb = m.astype(BF16)
    lane = lax.broadcasted_iota(jnp.int32, mb.shape, 1)
    zero = jnp.zeros_like(mb)
    return jnp.concatenate([jnp.where(lane < n, mb, zero), jnp.where(lane >= n, mb, zero)], axis=0)


def _pair_mm(u, w):
    return jnp.dot(u.astype(BF16), _pair_block_diag(w), preferred_element_type=F32)


def _tri_inverse_pairs(a_list, between=lambda: None):
    n = a_list[0].shape[0]
    ii = lax.broadcasted_iota(jnp.int32, (n, 2 * n), 0)
    jj = lax.broadcasted_iota(jnp.int32, (n, 2 * n), 1) % n
    same_block = lambda size: (ii // size) == (jj // size)
    ps = [jnp.where(same_block(TRI_BASE), a, 0.0) for a in a_list]
    xs = [jnp.where(ii == jj, 1.0, 0.0) - p for p in ps]
    for _ in range((TRI_BASE - 1).bit_length() - 1):
        ps = [_pair_mm(p, p) for p in ps]
        between()
        xs = [x + _pair_mm(x, p) for x, p in zip(xs, ps)]
        between()
    size = TRI_BASE
    while size < n:
        off = same_block(2 * size) & jnp.logical_not(same_block(size))
        ys = [_pair_mm(x, jnp.where(off, a, 0.0)) for x, a in zip(xs, a_list)]
        between()
        xs = [x - _pair_mm(y, x) for x, y in zip(xs, ys)]
        between()
        size *= 2
    return xs


RING_POINTS = 4
RING_SLOTS = 3


class _ShiftRing:
    def __init__(self, step, n_steps, cache_ref, out_ref, stage_ref, zero_ref, sem_in, sem_out, sem_row):
        self.step, self.n_steps = step, n_steps
        self.cache, self.out, self.stage, self.zero = cache_ref, out_ref, stage_ref, zero_ref
        self.sem_in, self.sem_out, self.sem_row = sem_in, sem_out, sem_row
        self.batch, self.rows = cache_ref.shape[1], cache_ref.shape[2]
        self.chunk = self.rows // RING_POINTS

    def _n_rows(self, q):
        return self.chunk if q < RING_POINTS - 1 else self.chunk - 1

    def _in(self, step, q, slot):
        n = self._n_rows(q)
        return pltpu.make_async_copy(
            self.cache.at[step // self.batch, step % self.batch, pl.ds(1 + self.chunk * q, n)],
            self.stage.at[slot, pl.ds(0, n)], self.sem_in.at[slot])

    def _out(self, step, q, slot):
        n = self._n_rows(q)
        return pltpu.make_async_copy(
            self.stage.at[slot, pl.ds(0, n)],
            self.out.at[step // self.batch, step % self.batch, pl.ds(self.chunk * q, n)], self.sem_out.at[slot])

    def _row(self):
        return pltpu.make_async_copy(
            self.zero, self.out.at[self.step // self.batch, self.step % self.batch, self.rows - 1], self.sem_row)

    def begin(self):
        step = self.step
        self.zero[...] = jnp.zeros_like(self.zero)

        @pl.when(step == 0)
        def _():
            self._in(step, 0, 0).start()
            self._in(step, 1, 1).start()

        self._row().start()

    def point(self, q):
        step = self.step
        n = step * RING_POINTS + q
        slot = n % RING_SLOTS
        self._in(step, q, slot).wait()
        self._out(step, q, slot).start()
        prev_step, prev_q = (step, q - 1) if q > 0 else (step - 1, RING_POINTS - 1)
        free_slot = (n + RING_SLOTS - 1) % RING_SLOTS

        @pl.when(n > 0)
        def _():
            self._out(prev_step, prev_q, free_slot).wait()

        next_step, next_q = step + (q + 2) // RING_POINTS, (q + 2) % RING_POINTS

        @pl.when(n + 2 < self.n_steps * RING_POINTS)
        def _():
            self._in(next_step, next_q, free_slot).start()

    def end(self):
        step = self.step
        self._row().wait()

        @pl.when(step == self.n_steps - 1)
        def _():
            last = self.n_steps * RING_POINTS - 1
            self._out(step, RING_POINTS - 1, last % RING_SLOTS).wait()


def _gdn_prompt_kernel(*refs, n_qk, n_v, with_shift):
    if with_shift:
        (qkv_ref, z_ref, gate_ref, cw_ref, alog_ref, dtb_ref, onorm_ref, cache_ref,
         o_ref, s_out_ref, shift_ref,
         xe_ref, qk_ref, v_ref, gate_scr, bb_ref, gcb_ref, tp_ref, pp_ref, s_ref, oscr_ref,
         stage_ref, zero_ref, sem_in, sem_out, sem_row) = refs
    else:
        (qkv_ref, z_ref, gate_ref, cw_ref, alog_ref, dtb_ref, onorm_ref,
         o_ref, s_out_ref,
         xe_ref, qk_ref, v_ref, gate_scr, bb_ref, gcb_ref, tp_ref, pp_ref, s_ref, oscr_ref) = refs
    tb = qkv_ref.shape[0]
    nchunk = tb // CHUNK
    t_idx = pl.program_id(1)
    halo = 8
    ring_point = lambda q: None
    if with_shift:
        ring = _ShiftRing(pl.program_id(0) * pl.num_programs(1) + t_idx, pl.num_programs(0) * pl.num_programs(1),
                          cache_ref, shift_ref, stage_ref, zero_ref, sem_in, sem_out, sem_row)
        ring.begin()
        ring_point = ring.point

    @pl.when(t_idx == 0)
    def _():
        xe_ref[:, halo - (CONV_TAPS - 1):halo, :] = jnp.zeros((xe_ref.shape[0], CONV_TAPS - 1, LANES), F32)
        s_ref[...] = jnp.zeros_like(s_ref)

    ring_point(0)

    gate_scr[0] = _sigmoid(gate_ref[:, :LANES])
    g = -jnp.exp(alog_ref[...]) * _softplus(gate_ref[:, LANES:] + dtb_ref[...])
    pos = lax.broadcasted_iota(jnp.int32, (tb, LANES), 0) % CHUNK
    shift = 1
    while shift < CHUNK:
        g = g + jnp.where(pos >= shift, pltpu.roll(g, shift, axis=0), 0.0)
        shift *= 2
    gate_scr[1] = g
    for h in range(n_v):
        bb_ref[h] = jnp.broadcast_to(gate_scr[0, :, h:h + 1], (tb, LANES))
        gcb_ref[h] = jnp.broadcast_to(gate_scr[1, :, h:h + 1], (tb, LANES))

    def conv_block(cb):
        lo = cb * LANES
        xe_ref[cb, halo:halo + tb, :] = qkv_ref[:, lo:lo + LANES].astype(F32)
        y = cw_ref[0:1, lo:lo + LANES] * xe_ref[cb, halo - 3:halo - 3 + tb, :]
        for j in range(1, CONV_TAPS):
            y = y + cw_ref[j:j + 1, lo:lo + LANES] * xe_ref[cb, halo - 3 + j:halo - 3 + j + tb, :]
        y = _silu(y)
        xe_ref[cb, halo - 3:halo, :] = xe_ref[cb, halo + tb - 3:halo + tb, :]
        if cb < 2 * n_qk:
            y = y * lax.rsqrt(jnp.sum(y * y, axis=-1, keepdims=True) + EPS)
            if cb < n_qk:
                y = y * (HEAD_DIM ** -0.5)
            yb = y.astype(BF16)
            half = 0 if cb < n_qk else CHUNK
            for c in range(nchunk):
                qk_ref[cb % n_qk, c, half:half + CHUNK, :] = yb[c * CHUNK:(c + 1) * CHUNK, :]
        else:
            v_ref[cb - 2 * n_qk] = y

    for cb in range(2 * n_qk):
        conv_block(cb)
    ring_point(1)
    pending = [functools.partial(conv_block, cb) for cb in range(2 * n_qk, 2 * n_qk + n_v)]

    def filler():
        if pending:
            pending.pop(0)()

    ii = lax.broadcasted_iota(jnp.int32, (CHUNK, 2 * CHUNK), 0)
    lane = lax.broadcasted_iota(jnp.int32, (CHUNK, 2 * CHUNK), 1)
    jj = lane % CHUNK
    causal = ii >= jj
    strict = ii > jj
    pairs_per_iter = 4

    for it in range(n_qk // pairs_per_iter):
        probs = [(it * pairs_per_iter + dj, c) for dj in range(pairs_per_iter) for c in range(nchunk)]
        grams, a_list = [], []
        for j, c in probs:
            qk = qk_ref[j, c]
            k2 = jnp.concatenate([qk[CHUNK:], qk[CHUNK:]], axis=0)
            grams.append(lax.dot_general(qk, k2, (((1,), (1,)), ((), ())), preferred_element_type=F32))
        filler()
        for (j, c), gram in zip(probs, grams):
            rows = slice(c * CHUNK, (c + 1) * CHUNK)
            gc = jnp.where(lane < CHUNK, gcb_ref[2 * j, rows, :], gcb_ref[2 * j + 1, rows, :])
            bt = jnp.where(lane < CHUNK, bb_ref[2 * j, rows, :], bb_ref[2 * j + 1, rows, :])
            gc_row = jnp.sum(jnp.where(ii == jj, gc, 0.0), axis=0, keepdims=True)
            decay = jnp.where(causal, jnp.exp(jnp.where(causal, gc - gc_row, 0.0)), 0.0)
            a_list.append(jnp.where(strict, bt * gram[CHUNK:] * decay, 0.0))
            pp_ref[j, c] = (gram[:CHUNK] * decay).astype(BF16)
        for (j, c), t_inv in zip(probs, _tri_inverse_pairs(a_list, filler)):
            tp_ref[j, c] = t_inv.astype(BF16)
    while pending:
        filler()
    ring_point(2)

    def block_diag2(m):
        mb = m.astype(BF16)
        zero = jnp.zeros((CHUNK, HEAD_DIM), BF16)
        return jnp.concatenate([jnp.concatenate([mb[:, :HEAD_DIM], zero], axis=1),
                                jnp.concatenate([zero, mb[:, HEAD_DIM:]], axis=1)], axis=0)

    def finish_chunk(c):
        rows = slice(c * CHUNK, (c + 1) * CHUNK)
        for h in range(n_v):
            lo = h * HEAD_DIM
            o = _rms_rows(oscr_ref[h, rows, :], onorm_ref[...])
            o_ref[rows, lo:lo + HEAD_DIM] = (o * _silu(z_ref[rows, lo:lo + HEAD_DIM].astype(F32))).astype(o_ref.dtype)

    for c in range(nchunk):
        if c > 0:
            finish_chunk(c - 1)
        rows = slice(c * CHUNK, (c + 1) * CHUNK)
        heads = range(n_qk)
        pair = lambda ref, j: jnp.concatenate([ref[2 * j, rows, :], ref[2 * j + 1, rows, :]], axis=1)
        qks = [qk_ref[j, c] for j in heads]
        projs = [jnp.dot(qks[j], jnp.concatenate([s_ref[2 * j], s_ref[2 * j + 1]], axis=1).astype(BF16),
                         preferred_element_type=F32) for j in heads]
        gcs = [pair(gcb_ref, j) for j in heads]
        e_gcs = [jnp.exp(gc) for gc in gcs]
        rhss = [pair(bb_ref, j) * (pair(v_ref, j) - e_gcs[j] * projs[j][CHUNK:]) for j in heads]
        v_news = [jnp.dot(tp_ref[j, c], block_diag2(rhss[j]), preferred_element_type=F32) for j in heads]
        outs = [e_gcs[j] * projs[j][:CHUNK]
                + jnp.dot(pp_ref[j, c], block_diag2(v_news[j]), preferred_element_type=F32) for j in heads]
        g_lasts = [gc[CHUNK - 1:CHUNK, :] for gc in gcs]
        v_decs = [(jnp.exp(g_lasts[j] - gcs[j]) * v_news[j]).astype(BF16) for j in heads]
        d_states = [lax.dot_general(qks[j][CHUNK:], v_decs[j], (((0,), (0,)), ((), ())),
                                    preferred_element_type=F32) for j in heads]
        for j in heads:
            for e in range(2):
                lanes = slice(e * HEAD_DIM, (e + 1) * HEAD_DIM)
                oscr_ref[2 * j + e, rows, :] = outs[j][:, lanes]
                s_ref[2 * j + e] = jnp.exp(g_lasts[j][:, lanes]) * s_ref[2 * j + e] + d_states[j][:, lanes]
    ring_point(3)
    finish_chunk(nchunk - 1)

    @pl.when(t_idx == pl.num_programs(1) - 1)
    def _():
        s_out_ref[...] = s_ref[...]

    if with_shift:
        ring.end()


def gdn_prompt(qkvz, gates, conv_w, a_log, dt_bias, o_norm, n_qk, n_v, tb, shift_cache=None):
    b, t, _ = qkvz.shape
    assert n_v == 2 * n_qk and 2 * CHUNK == LANES
    conv_dim = (2 * n_qk + n_v) * HEAD_DIM
    v_dim = n_v * HEAD_DIM
    tb = min(tb, t)
    nchunk = tb // CHUNK
    pad = lambda p: jnp.zeros((1, LANES), F32).at[0, :n_v].set(p.astype(F32))
    extra_in, extra_in_specs, extra_out_specs, extra_out_shape, extra_scratch = [], [], [], [], []
    if shift_cache is not None:
        n_l, n_b, rows = shift_cache.shape[:3]
        assert n_l * n_b == b * (t // tb) and rows % RING_POINTS == 0
        extra_in = [shift_cache]
        extra_in_specs = [pl.BlockSpec(memory_space=pl.ANY)]
        extra_out_specs = [pl.BlockSpec(memory_space=pl.ANY)]
        extra_out_shape = [jax.ShapeDtypeStruct(shift_cache.shape, shift_cache.dtype)]
        extra_scratch = [pltpu.VMEM((RING_SLOTS, rows // RING_POINTS) + shift_cache.shape[3:], shift_cache.dtype),
                         pltpu.VMEM(shift_cache.shape[3:], shift_cache.dtype),
                         pltpu.SemaphoreType.DMA((RING_SLOTS,)), pltpu.SemaphoreType.DMA((RING_SLOTS,)),
                         pltpu.SemaphoreType.DMA(())]
    return pl.pallas_call(
        functools.partial(_gdn_prompt_kernel, n_qk=n_qk, n_v=n_v, with_shift=shift_cache is not None),
        grid=(b, t // tb),
        in_specs=[pl.BlockSpec((None, tb, conv_dim), lambda bi, ti: (bi, ti, 0)),
                  pl.BlockSpec((None, tb, v_dim), lambda bi, ti: (bi, ti, conv_dim // v_dim)),
                  pl.BlockSpec((None, tb, 2 * LANES), lambda bi, ti: (bi, ti, 0)),
                  pl.BlockSpec((CONV_TAPS, conv_dim), lambda bi, ti: (0, 0)),
                  pl.BlockSpec((1, LANES), lambda bi, ti: (0, 0)),
                  pl.BlockSpec((1, LANES), lambda bi, ti: (0, 0)),
                  pl.BlockSpec((1, HEAD_DIM), lambda bi, ti: (0, 0))] + extra_in_specs,
        out_specs=[pl.BlockSpec((None, tb, v_dim), lambda bi, ti: (bi, ti, 0)),
                   pl.BlockSpec((None, n_v, HEAD_DIM, HEAD_DIM), lambda bi, ti: (bi, 0, 0, 0))] + extra_out_specs,
        out_shape=[jax.ShapeDtypeStruct((b, t, v_dim), BF16),
                   jax.ShapeDtypeStruct((b, n_v, HEAD_DIM, HEAD_DIM), F32)] + extra_out_shape,
        scratch_shapes=[pltpu.VMEM((2 * n_qk + n_v, tb + 8, LANES), F32),
                        pltpu.VMEM((n_qk, nchunk, 2 * CHUNK, HEAD_DIM), BF16),
                        pltpu.VMEM((n_v, tb, HEAD_DIM), F32),
                        pltpu.VMEM((2, tb, LANES), F32),
                        pltpu.VMEM((n_v, tb, LANES), F32),
                        pltpu.VMEM((n_v, tb, LANES), F32),
                        pltpu.VMEM((n_qk, nchunk, CHUNK, 2 * CHUNK), BF16),
                        pltpu.VMEM((n_qk, nchunk, CHUNK, 2 * CHUNK), BF16),
                        pltpu.VMEM((n_v, HEAD_DIM, HEAD_DIM), F32),
                        pltpu.VMEM((n_v, tb, HEAD_DIM), F32)] + extra_scratch,
        compiler_params=_params(2),
        name="gdn_prompt",
    )(qkvz, qkvz, gates, conv_w, pad(a_log), pad(dt_bias), o_norm.reshape(1, HEAD_DIM), *extra_in)


GDN_QK_HEADS = 8
GDN_V_HEADS = 16
GDN_CONV_DIM = (2 * GDN_QK_HEADS + GDN_V_HEADS) * HEAD_DIM
GDN_V_DIM = GDN_V_HEADS * HEAD_DIM


def prep_gdn_weights(w_in, w_out):
    d = w_in.shape[0]
    main = GDN_CONV_DIM + GDN_V_DIM
    w_main = w_in[:, :main].astype(BF16)
    w_gate = jnp.zeros((d, 2 * LANES), F32)
    w_gate = w_gate.at[:, :GDN_V_HEADS].set(w_in[:, main:main + GDN_V_HEADS])
    w_gate = w_gate.at[:, LANES:LANES + GDN_V_HEADS].set(w_in[:, main + GDN_V_HEADS:])
    return w_main, w_gate.astype(BF16), w_out.astype(BF16)


def gdn_layer_prompt(x, gain, weights, conv_w, a_log, dt_bias, o_norm, shift_cache=None):
    w_main, w_gate, _ = weights
    b, t, d = x.shape
    qkvz, gates = gdn_inproj(x.reshape(b * t, d), gain, w_main, w_gate, BF16, 512)
    qkvz = qkvz.reshape(b, t, -1)
    o, state, *shifted = gdn_prompt(qkvz, gates.reshape(b, t, -1), conv_w, a_log, dt_bias, o_norm,
                                    GDN_QK_HEADS, GDN_V_HEADS, 256, shift_cache)
    conv_tail = qkvz[:, t - (CONV_TAPS - 1):, :GDN_CONV_DIM].astype(F32)
    return (o.reshape(b * t, -1), conv_tail, state, *shifted)


def _gdn_step_kernel(qkvz_ref, gate_ref, conv_ref, s_ref, cw_ref, alog_ref, dtb_ref, onorm_ref,
                     o_ref, convn_ref, sn_ref, *, n_qk, n_v):
    rep = n_v // n_qk
    conv_dim = (2 * n_qk + n_v) * HEAD_DIM
    x = qkvz_ref[:, :conv_dim]
    y = cw_ref[CONV_TAPS - 1:CONV_TAPS, :] * x
    for j in range(CONV_TAPS - 1):
        y = y + cw_ref[j:j + 1, :] * conv_ref[j:j + 1, :]
    y = _silu(y)
    convn_ref[0:CONV_TAPS - 2, :] = conv_ref[1:CONV_TAPS - 1, :]
    convn_ref[CONV_TAPS - 2:CONV_TAPS - 1, :] = x

    beta = _sigmoid(gate_ref[:, :LANES])
    g = -jnp.exp(alog_ref[...]) * _softplus(gate_ref[:, LANES:] + dtb_ref[...])
    eye = (lax.broadcasted_iota(jnp.int32, (HEAD_DIM, HEAD_DIM), 0)
           == lax.broadcasted_iota(jnp.int32, (HEAD_DIM, HEAD_DIM), 1))

    def column(row):
        return jnp.sum(jnp.where(eye, row, 0.0), axis=1, keepdims=True)

    def l2(row):
        return row * lax.rsqrt(jnp.sum(row * row, axis=-1, keepdims=True) + EPS)

    for j in range(n_qk):
        q_col = column(l2(y[:, j * HEAD_DIM:(j + 1) * HEAD_DIM]) * (HEAD_DIM ** -0.5))
        k_col = column(l2(y[:, (n_qk + j) * HEAD_DIM:(n_qk + j + 1) * HEAD_DIM]))
        for e in range(rep):
            h = rep * j + e
            lo = h * HEAD_DIM
            v = y[:, 2 * n_qk * HEAD_DIM + lo:2 * n_qk * HEAD_DIM + lo + HEAD_DIM]
            s = s_ref[h]
            e_g = jnp.exp(g[:, h:h + 1])
            k_s = jnp.sum(s * k_col, axis=0, keepdims=True)
            v_new = beta[:, h:h + 1] * (v - e_g * k_s)
            s_new = e_g * s + k_col * v_new
            sn_ref[h] = s_new
            o = _rms_rows(jnp.sum(s_new * q_col, axis=0, keepdims=True), onorm_ref[...])
            z = qkvz_ref[:, conv_dim + lo:conv_dim + lo + HEAD_DIM]
            o_ref[:, lo:lo + HEAD_DIM] = (o * _silu(z)).astype(o_ref.dtype)


def gdn_step(qkvz, gates, conv_state, state, conv_w, a_log, dt_bias, o_norm, n_qk, n_v):
    b = qkvz.shape[0]
    conv_dim = (2 * n_qk + n_v) * HEAD_DIM
    v_dim = n_v * HEAD_DIM
    pad = lambda p: jnp.zeros((1, LANES), F32).at[0, :n_v].set(p.astype(F32))
    row = lambda n: pl.BlockSpec((None, 1, n), lambda bi: (bi, 0, 0))
    const = lambda shape: pl.BlockSpec(shape, lambda bi: (0,) * len(shape))
    st = pl.BlockSpec((None, n_v, HEAD_DIM, HEAD_DIM), lambda bi: (bi, 0, 0, 0))
    cv = pl.BlockSpec((None, CONV_TAPS - 1, conv_dim), lambda bi: (bi, 0, 0))
    return pl.pallas_call(
        functools.partial(_gdn_step_kernel, n_qk=n_qk, n_v=n_v),
        grid=(b,),
        in_specs=[row(conv_dim + v_dim), row(2 * LANES), cv, st,
                  const((CONV_TAPS, conv_dim)), const((1, LANES)), const((1, LANES)), const((1, HEAD_DIM))],
        out_specs=[row(v_dim), cv, st],
        out_shape=[jax.ShapeDtypeStruct((b, 1, v_dim), BF16),
                   jax.ShapeDtypeStruct(conv_state.shape, F32),
                   jax.ShapeDtypeStruct(state.shape, F32)],
        compiler_params=_params(1),
        name="gdn_step",
    )(qkvz.reshape(b, 1, -1), gates.reshape(b, 1, -1), conv_state, state,
      conv_w, pad(a_log), pad(dt_bias), o_norm.reshape(1, HEAD_DIM))


def gdn_layer_sample(x, gain, weights, conv_w, a_log, dt_bias, o_norm, conv_state, state):
    w_main, w_gate, _ = weights
    b = x.shape[0]
    qkvz, gates = gdn_inproj(x, gain, w_main, w_gate, F32, b)
    o, conv_new, state_new = gdn_step(qkvz, gates, conv_state, state, conv_w, a_log, dt_bias, o_norm,
                                      GDN_QK_HEADS, GDN_V_HEADS)
    return o.reshape(b, -1), conv_new, state_new


DSW_HEADS = 8
DSW_WIDTH = DSW_HEADS * HEAD_DIM
DSW_TILE = DSW_BLOCK * max(d for _, d in DSW_GROUPS)
DSW_BLOCKS_PER_ITER = 8


def _dsw_attn_kernel(*refs):
    n_groups = len(DSW_GROUPS)
    ins = [refs[5 * g:5 * g + 5] for g in range(n_groups)]
    o_ref = refs[5 * n_groups]
    scratch = refs[5 * n_groups + 1:]
    kf_refs, vf_refs = scratch[:n_groups], scratch[n_groups:2 * n_groups]
    og_ref, lg_ref = scratch[2 * n_groups:]
    first_tile = pl.program_id(2) == 0
    blk = DSW_BLOCK
    qi = lax.broadcasted_iota(jnp.int32, (blk, 2 * blk), 0)
    ki = lax.broadcasted_iota(jnp.int32, (blk, 2 * blk), 1)
    band = (ki >= qi) & (ki <= qi + blk)
    scale = HEAD_DIM ** -0.5

    for g, (_, dil) in enumerate(DSW_GROUPS):
        q_ref, kc_ref, vc_ref, kp_ref, vp_ref = ins[g]
        kf_ref, vf_ref = kf_refs[g], vf_refs[g]
        per_res = q_ref.shape[1] // blk
        kf_ref[:, :blk, :] = kp_ref[...]
        kf_ref[:, blk:, :] = kc_ref[...]
        vf_ref[:, :blk, :] = vp_ref[...]
        vf_ref[:, blk:, :] = vc_ref[...]

        def blocks_body(it, carry, *, g=g, dil=dil, per_res=per_res, q_ref=q_ref, kf_ref=kf_ref, vf_ref=vf_ref):
            ids = [it * DSW_BLOCKS_PER_ITER + i for i in range(DSW_BLOCKS_PER_ITER)]
            rs = [bi // per_res for bi in ids]
            ms = [bi % per_res for bi in ids]
            row0s = [pl.multiple_of(m * blk, blk) for m in ms]
            ss = [lax.dot_general(q_ref[r, pl.ds(row0, blk), :], kf_ref[r, pl.ds(row0, 2 * blk), :],
                                  (((1,), (1,)), ((), ())), preferred_element_type=F32) * scale
                  for r, row0 in zip(rs, row0s)]
            ss = [jnp.where(band & ((ki >= blk) | (m > 0) | jnp.logical_not(first_tile)), s, NEG_INF)
                  for s, m in zip(ss, ms)]
            mxs = [jnp.max(s, axis=-1, keepdims=True) for s in ss]
            ps = [jnp.exp(s - mx) for s, mx in zip(ss, mxs)]
            dens = [jnp.sum(p, axis=-1, keepdims=True) for p in ps]
            os_ = [jnp.dot(p.astype(BF16), vf_ref[r, pl.ds(row0, 2 * blk), :], preferred_element_type=F32)
                   for p, r, row0 in zip(ps, rs, row0s)]
            for o, den, mx, r, row0 in zip(os_, dens, mxs, rs, row0s):
                o = o * (1.0 / den)
                lse = jnp.broadcast_to(mx + jnp.log(den), (blk, LANES))
                start = row0 * dil + r
                rows = pl.ds(start, blk) if dil == 1 else pl.ds(start, blk, stride=dil)
                og_ref[g, rows, :] = o
                lg_ref[g, rows, :] = lse
            return carry

        lax.fori_loop(0, dil * per_res // DSW_BLOCKS_PER_ITER, blocks_body, 0)

    top = lg_ref[0]
    for g in range(1, n_groups):
        top = jnp.maximum(top, lg_ref[g])
    num = jnp.zeros_like(top)
    den = jnp.zeros_like(top)
    for g in range(n_groups):
        w = jnp.exp(lg_ref[g] - top)
        num = num + w * og_ref[g]
        den = den + w
    o_ref[...] = (num * (1.0 / den)).astype(o_ref.dtype)


def dsw_attn(projs):
    b = projs[0].shape[0]
    t = projs[0].shape[1] * projs[0].shape[2]
    tile = DSW_TILE
    heads = DSW_HEADS
    in_specs, args, kv_scratch = [], [], []
    for (_, dil), p in zip(DSW_GROUPS, projs):
        rows = tile // dil
        per_res = rows // DSW_BLOCK
        cur = lambda col: pl.BlockSpec((None, dil, rows, HEAD_DIM),
                                       functools.partial(lambda bi, h, n, col: (bi, 0, n, col * heads + h), col=col))
        prev = lambda col: pl.BlockSpec(
            (None, dil, DSW_BLOCK, HEAD_DIM),
            functools.partial(lambda bi, h, n, col, per_res: (bi, 0, jnp.maximum(n * per_res - 1, 0), col * heads + h),
                              col=col, per_res=per_res))
        in_specs += [cur(0), cur(1), cur(2), prev(1), prev(2)]
        args += [p] * 5
        kv_scratch.append(pltpu.VMEM((dil, DSW_BLOCK + rows, HEAD_DIM), BF16))
    n_groups = len(DSW_GROUPS)
    return pl.pallas_call(
        _dsw_attn_kernel,
        grid=(b, heads, t // tile),
        in_specs=in_specs,
        out_specs=pl.BlockSpec((None, tile, HEAD_DIM), lambda bi, h, n: (bi, n, h)),
        out_shape=jax.ShapeDtypeStruct((b, t, heads * HEAD_DIM), BF16),
        scratch_shapes=kv_scratch + kv_scratch + [pltpu.VMEM((n_groups, tile, HEAD_DIM), F32),
                                                  pltpu.VMEM((n_groups, tile, LANES), F32)],
        compiler_params=_params(3),
        name="dsw_attn",
    )(*args)


def prep_dsw_weights(w_in, w_out):
    d = w_in.shape[0]
    w = w_in.reshape(d, len(DSW_GROUPS), 3, DSW_WIDTH).transpose(1, 2, 0, 3).astype(BF16)
    return [w[g] for g in range(len(DSW_GROUPS))], w_out.astype(BF16)


def dsw_layer_prompt(x, gain, weights, q_norm, k_norm):
    w_groups, _ = weights
    b, t, d = x.shape
    projs = [dsw_inproj(x, gain, jnp.stack([q_norm[g], k_norm[g]]), w_groups[g], dil,
                        max(DSW_INPROJ_ROWS, DSW_INPROJ_MIN_SUB * dil))
             for g, (_, dil) in enumerate(DSW_GROUPS)]
    return dsw_attn(projs).reshape(b * t, -1), projs


CACHE_RESIDUES = 4


def _dsw_cache_kernel(*refs):
    *p_refs, o_ref, slab_scr = refs
    layer = pl.program_id(0)
    rows, n_res = o_ref.shape[:2]
    for li in range(len(p_refs) // 2):
        @pl.when(layer == li)
        def _(k_ref=p_refs[2 * li], v_ref=p_refs[2 * li + 1]):
            for rr in range(n_res):
                for kv, p_ref in enumerate((k_ref, v_ref)):
                    slab = slab_scr.at[2 * rr + kv]
                    slab[...] = p_ref[rr].astype(o_ref.dtype).reshape(rows, DSW_HEADS, HEAD_DIM)
                    o_ref[:, rr, kv] = slab[...]


def dsw_prompt_caches(projs_layers):
    n_layers = len(projs_layers)
    outs = []
    for g, (window, dil) in enumerate(DSW_GROUPS):
        ps = [pl_[g] for pl_ in projs_layers]
        b, _, sub, _ = ps[0].shape
        last = sub // DSW_BLOCK - 1
        n_res = min(dil, CACHE_RESIDUES)

        def in_map(l, bi, rb, *, li, col):
            before, after = l < li, l > li
            pick = lambda lo, x, hi: jnp.where(before, lo, jnp.where(after, hi, x))
            return (pick(0, bi, b - 1), pick(0, rb, dil // n_res - 1), last, col)

        out = pl.pallas_call(
            _dsw_cache_kernel,
            grid=(n_layers, b, dil // n_res),
            in_specs=[pl.BlockSpec((None, n_res, DSW_BLOCK, DSW_WIDTH), functools.partial(in_map, li=li, col=col))
                      for li in range(n_layers) for col in (1, 2)],
            out_specs=pl.BlockSpec((None, None, DSW_BLOCK, n_res, 2, DSW_HEADS, HEAD_DIM),
                                   lambda l, bi, rb: (l, bi, 0, rb, 0, 0, 0)),
            out_shape=jax.ShapeDtypeStruct((n_layers, b, DSW_BLOCK, dil, 2, DSW_HEADS, HEAD_DIM), F32),
            scratch_shapes=[pltpu.VMEM((2 * n_res, DSW_BLOCK, DSW_HEADS, HEAD_DIM), F32)],
            compiler_params=_params(3),
            name="dsw_prompt_cache",
        )(*[p for p in ps for _ in (1, 2)])
        outs.append(out.reshape(n_layers, b, window, 2, DSW_HEADS, HEAD_DIM))
    return outs


def _dsw_decode_kernel(p_ref, *refs):
    n_groups = len(DSW_GROUPS)
    c_refs = refs[:n_groups]
    o_ref, new_ref = refs[n_groups:]
    scale = HEAD_DIM ** -0.5
    outs, lses = [], []
    for g in range(n_groups):
        q = p_ref[g, 0].astype(F32)
        k_new = p_ref[g, 1].astype(F32)
        v_new = p_ref[g, 2].astype(F32)
        new_ref[g, 0] = k_new
        new_ref[g, 1] = v_new
        s = jnp.sum(c_refs[g][:, 0] * q[None], axis=-1, keepdims=True) * scale
        s_new = jnp.sum(k_new * q, axis=-1, keepdims=True) * scale
        mx = jnp.maximum(jnp.max(s, axis=0), s_new)
        p = jnp.exp(s - mx[None])
        p_new = jnp.exp(s_new - mx)
        den = jnp.sum(p, axis=0) + p_new
        outs.append((jnp.sum(p * c_refs[g][:, 1], axis=0) + p_new * v_new) * (1.0 / den))
        lses.append(mx + jnp.log(den))
    top = functools.reduce(jnp.maximum, lses)
    ws = [jnp.exp(l - top) for l in lses]
    num = sum(w * o for w, o in zip(ws, outs))
    o_ref[...] = (num * (1.0 / sum(ws))).astype(o_ref.dtype)


def dsw_decode(proj, caches, layer):
    b = proj.shape[0]
    n_groups = len(DSW_GROUPS)
    c_specs, c_args = [], []
    for (window, dil), c in zip(DSW_GROUPS, caches):
        c_args.append(c.reshape(c.shape[0], b, window // dil, dil, 2, DSW_HEADS, HEAD_DIM))
        c_specs.append(pl.BlockSpec((None, None, window // dil, None, 2, DSW_HEADS, HEAD_DIM),
                                    lambda bi: (layer, bi, 0, 0, 0, 0, 0)))
    return pl.pallas_call(
        _dsw_decode_kernel,
        grid=(b,),
        in_specs=[pl.BlockSpec((None, n_groups, 3, DSW_HEADS, HEAD_DIM), lambda bi: (bi, 0, 0, 0, 0))] + c_specs,
        out_specs=[pl.BlockSpec((None, DSW_HEADS, HEAD_DIM), lambda bi: (bi, 0, 0)),
                   pl.BlockSpec((None, n_groups, 2, DSW_HEADS, HEAD_DIM), lambda bi: (bi, 0, 0, 0, 0))],
        out_shape=[jax.ShapeDtypeStruct((b, DSW_HEADS, HEAD_DIM), BF16),
                   jax.ShapeDtypeStruct((b, n_groups, 2, DSW_HEADS, HEAD_DIM), F32)],
        compiler_params=_params(1),
        name="dsw_decode",
    )(proj, *c_args)


def dsw_layer_sample(x, gain, weights, q_norm, k_norm, caches, layer):
    w_groups, _ = weights
    b, d = x.shape
    proj = jnp.stack([dsw_inproj(x.reshape(1, b, d), gain, jnp.stack([q_norm[g], k_norm[g]]), w_groups[g], 1, b)
                      .reshape(b, 3, DSW_HEADS, HEAD_DIM) for g in range(len(DSW_GROUPS))], axis=1)
    o, new_rows = dsw_decode(proj, caches, layer)
    return o.reshape(b, -1), new_rows


SHIFT_ROWS = 256


def _shift_kernel(c_ref, next_ref, new_ref, o_ref):
    rows = c_ref.shape[0]
    is_last = pl.program_id(2) == pl.num_programs(2) - 1
    o_ref[0:rows - 1] = c_ref[1:rows]

    @pl.when(is_last)
    def _():
        o_ref[rows - 1] = new_ref[...]

    @pl.when(jnp.logical_not(is_last))
    def _():
        o_ref[rows - 1] = next_ref[0]


def _last_row_kernel(shifted_ref, new_ref, o_ref):
    del shifted_ref
    o_ref[0] = new_ref[...]


def write_last_rows(shifted, new_rows_layers, g):
    new = jnp.stack(new_rows_layers)
    n_layers, b, window = shifted.shape[:3]
    tile = shifted.shape[3:]
    return pl.pallas_call(
        _last_row_kernel,
        grid=(n_layers, b),
        in_specs=[pl.BlockSpec(memory_space=pl.ANY),
                  pl.BlockSpec((None, None, None) + tile, lambda l, bi: (l, bi, g, 0, 0, 0))],
        out_specs=pl.BlockSpec((None, None, 1) + tile, lambda l, bi: (l, bi, window - 1, 0, 0, 0)),
        out_shape=jax.ShapeDtypeStruct(shifted.shape, shifted.dtype),
        input_output_aliases={0: 0},
        compiler_params=_params(2),
        name="cache_last_row",
    )(shifted, new)


def shift_caches(caches, new_rows_layers):
    new = jnp.stack(new_rows_layers)
    outs = []
    for g, c in enumerate(caches):
        if c is None:
            outs.append(None)
            continue
        n_layers, b, window = c.shape[:3]
        rows = min(SHIFT_ROWS, window)
        tile = (2, DSW_HEADS, HEAD_DIM)
        blk = pl.BlockSpec((None, None, rows) + tile, lambda l, bi, i: (l, bi, i, 0, 0, 0))
        nxt = pl.BlockSpec((None, None, 1) + tile,
                           lambda l, bi, i: (l, bi, jnp.minimum((i + 1) * rows, window - 1), 0, 0, 0))
        outs.append(pl.pallas_call(
            _shift_kernel,
            grid=(n_layers, b, window // rows),
            in_specs=[blk, nxt,
                      pl.BlockSpec((None, None, None) + tile, functools.partial(lambda l, bi, i, g: (l, bi, g, 0, 0, 0), g=g))],
            out_specs=blk,
            out_shape=jax.ShapeDtypeStruct(c.shape, c.dtype),
            compiler_params=_params(3),
            name="shift_cache",
        )(c, c, new))
    return outs


def kernel(x_prompt, x_sample, state_gdn, state_conv, cache_kv_w128, cache_kv_w512, cache_kv_w2048,
           norm_mix, norm_mlp, gdn_w_in, gdn_conv_w, gdn_a_log, gdn_dt_bias, gdn_o_norm, gdn_w_out,
           dsw_w_in, dsw_q_norm, dsw_k_norm, dsw_w_out, mlp_w_up, mlp_w_down):
    b, t, d = x_prompt.shape
    bs = x_sample.shape[0]
    depth = norm_mix.shape[0]
    caches = [cache_kv_w128, cache_kv_w512, cache_kv_w2048]
    yp = x_prompt.reshape(b * t, d)
    ys = x_sample.reshape(bs, d)
    p_gdn, p_conv, s_gdn, s_conv, p_projs, s_rows = [], [], [], [], [], []
    for i in range(depth):
        j = i // 2
        if i % 2 == 0:
            wts = prep_gdn_weights(gdn_w_in[j], gdn_w_out[j])
            par = (gdn_conv_w[j], gdn_a_log[j], gdn_dt_bias[j], gdn_o_norm[j])
            host = i == 2 * ((depth - 1) // 2)
            op, conv_p, state_p, *shifted = gdn_layer_prompt(yp.reshape(b, t, d), norm_mix[i], wts, *par,
                                                             caches[-1] if host else None)
            if host:
                shifted_big = shifted[0]
            os_, conv_s, state_s = gdn_layer_sample(ys, norm_mix[i], wts, *par, state_conv[j], state_gdn[j])
            p_gdn.append(state_p)
            p_conv.append(conv_p)
            s_gdn.append(state_s)
            s_conv.append(conv_s)
        else:
            wts = prep_dsw_weights(dsw_w_in[j], dsw_w_out[j])
            op, projs = dsw_layer_prompt(yp.reshape(b, t, d), norm_mix[i], wts, dsw_q_norm[j], dsw_k_norm[j])
            os_, rows = dsw_layer_sample(ys, norm_mix[i], wts, dsw_q_norm[j], dsw_k_norm[j], caches, j)
            p_projs.append(projs)
            s_rows.append(rows)
        w_up, w_down = mlp_w_up[i].astype(BF16), mlp_w_down[i].astype(BF16)
        yp = mixer_mlp(yp, op, wts[-1], norm_mlp[i], w_up, w_down, 512, 1024)
        ys = mixer_mlp(ys, os_, wts[-1], norm_mlp[i], w_up, w_down, bs, 1024)
    yp = yp.reshape(b, t, d)
    p_kv = dsw_prompt_caches(p_projs)
    s_kv = shift_caches(caches[:-1] + [None], s_rows)
    s_kv[-1] = write_last_rows(shifted_big, s_rows, len(caches) - 1)
    return (yp, ys.reshape(x_sample.shape),
            jnp.stack(p_gdn), jnp.stack(p_conv), p_kv[0], p_kv[1], p_kv[2],
            jnp.stack(s_gdn), jnp.stack(s_conv), s_kv[0], s_kv[1], s_kv[2])
```

```python
import functools

import jax
import jax.numpy as jnp
from jax import lax
from jax.experimental import pallas as pl
from jax.experimental.pallas import tpu as pltpu

F32 = jnp.float32
BF16 = jnp.bfloat16

EPS = 1e-6
NEG_INF = -1e30
LANES = 128
CONV_TAPS = 4
CHUNK = 64
HEAD_DIM = 128
DSW_BLOCK = 128
DSW_GROUPS = ((128, 1), (512, 4), (2048, 16))
VMEM_LIMIT = 56 * 1024 * 1024


def _params(n_axes, vmem=VMEM_LIMIT):
    return pltpu.CompilerParams(dimension_semantics=("arbitrary",) * n_axes,
                                vmem_limit_bytes=vmem)


def _sigmoid(x):
    return 0.5 + 0.5 * jnp.tanh(0.5 * x)


def _silu(x):
    h = 0.5 * x
    return h + h * jnp.tanh(h)


def _softplus(x):
    return jnp.maximum(x, 0.0) + jnp.log(1.0 + jnp.exp(-jnp.abs(x)))


def _rms_rows(x, gain_row):
    ms = jnp.mean(x * x, axis=-1, keepdims=True)
    return x * lax.rsqrt(ms + EPS) * gain_row


INPROJ_COLS = 512


def _gdn_inproj_kernel(x_ref, g_ref, w_ref, wg_ref, o_ref, og_ref, xn_ref):
    xn_ref[...] = _rms_rows(x_ref[...], g_ref[...]).astype(BF16)
    for lo in range(0, w_ref.shape[1], INPROJ_COLS):
        o_ref[:, lo:lo + INPROJ_COLS] = jnp.dot(
            xn_ref[...], w_ref[:, lo:lo + INPROJ_COLS], preferred_element_type=F32).astype(o_ref.dtype)
    og_ref[...] = jnp.dot(xn_ref[...], wg_ref[...], preferred_element_type=F32)


def gdn_inproj(x, gain, w_main, w_gate, out_dtype, tm):
    m, k = x.shape
    n, ng = w_main.shape[1], w_gate.shape[1]
    tm = min(tm, m)
    return pl.pallas_call(
        _gdn_inproj_kernel,
        grid=(m // tm,),
        in_specs=[pl.BlockSpec((tm, k), lambda i: (i, 0)),
                  pl.BlockSpec((1, k), lambda i: (0, 0)),
                  _resident((k, n)), _resident((k, ng))],
        out_specs=[pl.BlockSpec((tm, n), lambda i: (i, 0)),
                   pl.BlockSpec((tm, ng), lambda i: (i, 0))],
        out_shape=[jax.ShapeDtypeStruct((m, n), out_dtype), jax.ShapeDtypeStruct((m, ng), F32)],
        scratch_shapes=[pltpu.VMEM((tm, k), BF16)],
        compiler_params=_params(1),
        name="gdn_inproj",
    )(x, gain.reshape(1, k), w_main, w_gate)


def _mixer_mlp_kernel(x_ref, a_ref, wo_ref, g_ref, wu_ref, wd_ref, o_ref, xn_ref, *, tf):
    x1 = x_ref[...] + jnp.dot(a_ref[...], wo_ref[...], preferred_element_type=F32)
    xn_ref[...] = _rms_rows(x1, g_ref[...]).astype(BF16)
    o_ref[...] = x1
    for lo in range(0, wu_ref.shape[1], tf):
        h = jnp.dot(xn_ref[...], wu_ref[:, lo:lo + tf], preferred_element_type=F32)
        h = jnp.square(jnp.maximum(h, 0.0)).astype(BF16)
        o_ref[...] += jnp.dot(h, wd_ref[lo:lo + tf, :], preferred_element_type=F32)


def _resident(shape):
    return pl.BlockSpec(shape, lambda *_: (0,) * len(shape), pipeline_mode=pl.Buffered(1))


def mixer_mlp(x, a, w_out, gain, w_up, w_down, tm, tf):
    m, d = x.shape
    ka = a.shape[1]
    ff = w_up.shape[1]
    tm = min(tm, m)
    return pl.pallas_call(
        functools.partial(_mixer_mlp_kernel, tf=tf),
        grid=(m // tm,),
        in_specs=[pl.BlockSpec((tm, d), lambda i: (i, 0)),
                  pl.BlockSpec((tm, ka), lambda i: (i, 0)),
                  _resident((ka, d)), _resident((1, d)), _resident((d, ff)), _resident((ff, d))],
        out_specs=pl.BlockSpec((tm, d), lambda i: (i, 0)),
        out_shape=jax.ShapeDtypeStruct((m, d), F32),
        scratch_shapes=[pltpu.VMEM((tm, d), BF16)],
        compiler_params=_params(1),
        name="mixer_mlp",
    )(x, a, w_out, gain.reshape(1, d), w_up, w_down)


DEINTERLEAVE_STRIDE = 4
DSW_INPROJ_ROWS = 512
DSW_INPROJ_MIN_SUB = 64


def _dsw_inproj_kernel(x_ref, g_ref, hn_ref, w_ref, o_ref, xn_ref, head_scr, part_scr, *, dil):
    rows = x_ref.shape[0]
    sub = rows // dil
    width = w_ref.shape[2]
    xn_ref[...] = _rms_rows(x_ref[...], g_ref[...]).astype(BF16)
    for t in range(3):
        acc = jnp.dot(xn_ref[...], w_ref[t], preferred_element_type=F32)
        for h in range(width // HEAD_DIM):
            a = acc[:, h * HEAD_DIM:(h + 1) * HEAD_DIM]
            if t < 2:
                a = _rms_rows(a, hn_ref[t:t + 1, :])
            cols = slice(t * width + h * HEAD_DIM, t * width + (h + 1) * HEAD_DIM)
            if dil == 1:
                o_ref[0, :, cols] = a.astype(o_ref.dtype)
            elif dil <= DEINTERLEAVE_STRIDE:
                head_scr[h] = a
                for r in range(dil):
                    o_ref[r, :, cols] = head_scr[h, pl.ds(r, sub, stride=dil), :].astype(o_ref.dtype)
            else:
                st = DEINTERLEAVE_STRIDE
                part = rows // st
                head_scr[h] = a
                for q in range(st):
                    part_scr[h, q * part:(q + 1) * part, :] = head_scr[h, pl.ds(q, part, stride=st), :]
                for q in range(st):
                    for s in range(dil // st):
                        o_ref[st * s + q, :, cols] = part_scr[
                            h, pl.ds(q * part + s, sub, stride=dil // st), :].astype(o_ref.dtype)


def dsw_inproj(x, gain, head_gains, w3, dil, tile_rows):
    b, t, d = x.shape
    width = w3.shape[2]
    rows = min(tile_rows, t)
    return pl.pallas_call(
        functools.partial(_dsw_inproj_kernel, dil=dil),
        grid=(b, t // rows),
        in_specs=[pl.BlockSpec((None, rows, d), lambda bi, i: (bi, i, 0)),
                  pl.BlockSpec((1, d), lambda bi, i: (0, 0)),
                  pl.BlockSpec((2, HEAD_DIM), lambda bi, i: (0, 0)),
                  _resident((3, d, width))],
        out_specs=pl.BlockSpec((None, dil, rows // dil, 3 * width), lambda bi, i: (bi, 0, i, 0)),
        out_shape=jax.ShapeDtypeStruct((b, dil, t // dil, 3 * width), BF16),
        scratch_shapes=[pltpu.VMEM((rows, d), BF16),
                        pltpu.VMEM((width // HEAD_DIM, rows, HEAD_DIM), F32),
                        pltpu.VMEM((width // HEAD_DIM, rows, HEAD_DIM), F32)],
        compiler_params=_params(2),
        name="dsw_inproj",
    )(x, gain.reshape(1, d), head_gains, w3)


TRI_BASE = 8


def _pair_block_diag(m):
    n = m.shape[0]
    mb = m.astype(BF16)
    lane = lax.broadcasted_iota(jnp.int32, mb.shape, 1)
    zero = jnp.zeros_like(mb)
    return jnp.concatenate([jnp.where(lane < n, mb, zero), jnp.where(lane >= n, mb, zero)], axis=0)


def _pair_mm(u, w):
    return jnp.dot(u.astype(BF16), _pair_block_diag(w), preferred_element_type=F32)


def _tri_inverse_pairs(a_list, between=lambda: None):
    n = a_list[0].shape[0]
    ii = lax.broadcasted_iota(jnp.int32, (n, 2 * n), 0)
    jj = lax.broadcasted_iota(jnp.int32, (n, 2 * n), 1) % n
    same_block = lambda size: (ii // size) == (jj // size)
    ps = [jnp.where(same_block(TRI_BASE), a, 0.0) for a in a_list]
    xs = [jnp.where(ii == jj, 1.0, 0.0) - p for p in ps]
    for _ in range((TRI_BASE - 1).bit_length() - 1):
        ps = [_pair_mm(p, p) for p in ps]
        between()
        xs = [x + _pair_mm(x, p) for x, p in zip(xs, ps)]
        between()
    size = TRI_BASE
    while size < n:
        off = same_block(2 * size) & jnp.logical_not(same_block(size))
        ys = [_pair_mm(x, jnp.where(off, a, 0.0)) for x, a in zip(xs, a_list)]
        between()
        xs = [x - _pair_mm(y, x) for x, y in zip(xs, ys)]
        between()
        size *= 2
    return xs


RING_POINTS = 4
RING_SLOTS = 3


class _ShiftRing:
    def __init__(self, step, n_steps, cache_ref, out_ref, stage_ref, sem_in, sem_out, sem_row,
                 zero_ref=None, new_ref=None):
        self.step, self.n_steps = step, n_steps
        self.cache, self.out, self.stage, self.zero, self.new = cache_ref, out_ref, stage_ref, zero_ref, new_ref
        self.sem_in, self.sem_out, self.sem_row = sem_in, sem_out, sem_row
        self.batch, self.rows = cache_ref.shape[1], cache_ref.shape[2]
        self.chunk = self.rows // RING_POINTS

    def _n_rows(self, q):
        return self.chunk if q < RING_POINTS - 1 else self.chunk - 1

    def _in(self, step, q, slot):
        n = self._n_rows(q)
        return pltpu.make_async_copy(
            self.cache.at[step // self.batch, step % self.batch, pl.ds(1 + self.chunk * q, n)],
            self.stage.at[slot, pl.ds(0, n)], self.sem_in.at[slot])

    def _out(self, step, q, slot):
        n = self._n_rows(q)
        return pltpu.make_async_copy(
            self.stage.at[slot, pl.ds(0, n)],
            self.out.at[step // self.batch, step % self.batch, pl.ds(self.chunk * q, n)], self.sem_out.at[slot])

    def _row(self):
        l, b = self.step // self.batch, self.step % self.batch
        src = self.zero if self.new is None else self.new.at[l, b]
        return pltpu.make_async_copy(src, self.out.at[l, b, self.rows - 1], self.sem_row)

    def begin(self):
        step = self.step
        if self.new is None:
            self.zero[...] = jnp.zeros_like(self.zero)

        @pl.when(step == 0)
        def _():
            self._in(step, 0, 0).start()
            self._in(step, 1, 1).start()

        self._row().start()

    def point(self, q):
        step = self.step
        n = step * RING_POINTS + q
        slot = n % RING_SLOTS
        self._in(step, q, slot).wait()
        self._out(step, q, slot).start()
        prev_step, prev_q = (step, q - 1) if q > 0 else (step - 1, RING_POINTS - 1)
        free_slot = (n + RING_SLOTS - 1) % RING_SLOTS

        @pl.when(n > 0)
        def _():
            self._out(prev_step, prev_q, free_slot).wait()

        next_step, next_q = step + (q + 2) // RING_POINTS, (q + 2) % RING_POINTS

        @pl.when(n + 2 < self.n_steps * RING_POINTS)
        def _():
            self._in(next_step, next_q, free_slot).start()

    def end(self):
        step = self.step
        self._row().wait()

        @pl.when(step == self.n_steps - 1)
        def _():
            last = self.n_steps * RING_POINTS - 1
            self._out(step, RING_POINTS - 1, last % RING_SLOTS).wait()


def _gdn_prompt_kernel(*refs, n_qk, n_v, with_shift):
    if with_shift:
        (qkv_ref, z_ref, gate_ref, cw_ref, alog_ref, dtb_ref, onorm_ref, cache_ref,
         o_ref, s_out_ref, shift_ref,
         xe_ref, qk_ref, v_ref, gate_scr, bb_ref, gcb_ref, tp_ref, pp_ref, s_ref, oscr_ref,
         stage_ref, zero_ref, sem_in, sem_out, sem_row) = refs
    else:
        (qkv_ref, z_ref, gate_ref, cw_ref, alog_ref, dtb_ref, onorm_ref,
         o_ref, s_out_ref,
         xe_ref, qk_ref, v_ref, gate_scr, bb_ref, gcb_ref, tp_ref, pp_ref, s_ref, oscr_ref) = refs
    tb = qkv_ref.shape[0]
    nchunk = tb // CHUNK
    t_idx = pl.program_id(1)
    halo = 8
    ring_point = lambda q: None
    if with_shift:
        ring = _ShiftRing(pl.program_id(0) * pl.num_programs(1) + t_idx, pl.num_programs(0) * pl.num_programs(1),
                          cache_ref, shift_ref, stage_ref, sem_in, sem_out, sem_row, zero_ref=zero_ref)
        ring.begin()
        ring_point = ring.point

    @pl.when(t_idx == 0)
    def _():
        xe_ref[:, halo - (CONV_TAPS - 1):halo, :] = jnp.zeros((xe_ref.shape[0], CONV_TAPS - 1, LANES), F32)
        s_ref[...] = jnp.zeros_like(s_ref)

    ring_point(0)

    gate_scr[0] = _sigmoid(gate_ref[:, :LANES])
    g = -jnp.exp(alog_ref[...]) * _softplus(gate_ref[:, LANES:] + dtb_ref[...])
    pos = lax.broadcasted_iota(jnp.int32, (tb, LANES), 0) % CHUNK
    shift = 1
    while shift < CHUNK:
        g = g + jnp.where(pos >= shift, pltpu.roll(g, shift, axis=0), 0.0)
        shift *= 2
    gate_scr[1] = g
    for h in range(n_v):
        bb_ref[h] = jnp.broadcast_to(gate_scr[0, :, h:h + 1], (tb, LANES))
        gcb_ref[h] = jnp.broadcast_to(gate_scr[1, :, h:h + 1], (tb, LANES))

    def conv_block(cb):
        lo = cb * LANES
        xe_ref[cb, halo:halo + tb, :] = qkv_ref[:, lo:lo + LANES].astype(F32)
        y = cw_ref[0:1, lo:lo + LANES] * xe_ref[cb, halo - 3:halo - 3 + tb, :]
        for j in range(1, CONV_TAPS):
            y = y + cw_ref[j:j + 1, lo:lo + LANES] * xe_ref[cb, halo - 3 + j:halo - 3 + j + tb, :]
        y = _silu(y)
        xe_ref[cb, halo - 3:halo, :] = xe_ref[cb, halo + tb - 3:halo + tb, :]
        if cb < 2 * n_qk:
            y = y * lax.rsqrt(jnp.sum(y * y, axis=-1, keepdims=True) + EPS)
            if cb < n_qk:
                y = y * (HEAD_DIM ** -0.5)
            yb = y.astype(BF16)
            half = 0 if cb < n_qk else CHUNK
            for c in range(nchunk):
                qk_ref[cb % n_qk, c, half:half + CHUNK, :] = yb[c * CHUNK:(c + 1) * CHUNK, :]
        else:
            v_ref[cb - 2 * n_qk] = y

    for cb in range(2 * n_qk):
        conv_block(cb)
    ring_point(1)
    pending = [functools.partial(conv_block, cb) for cb in range(2 * n_qk, 2 * n_qk + n_v)]

    def filler():
        if pending:
            pending.pop(0)()

    ii = lax.broadcasted_iota(jnp.int32, (CHUNK, 2 * CHUNK), 0)
    lane = lax.broadcasted_iota(jnp.int32, (CHUNK, 2 * CHUNK), 1)
    jj = lane % CHUNK
    causal = ii >= jj
    strict = ii > jj
    pairs_per_iter = 4

    for it in range(n_qk // pairs_per_iter):
        probs = [(it * pairs_per_iter + dj, c) for dj in range(pairs_per_iter) for c in range(nchunk)]
        grams, a_list = [], []
        for j, c in probs:
            qk = qk_ref[j, c]
            k2 = jnp.concatenate([qk[CHUNK:], qk[CHUNK:]], axis=0)
            grams.append(lax.dot_general(qk, k2, (((1,), (1,)), ((), ())), preferred_element_type=F32))
        filler()
        for (j, c), gram in zip(probs, grams):
            rows = slice(c * CHUNK, (c + 1) * CHUNK)
            gc = jnp.where(lane < CHUNK, gcb_ref[2 * j, rows, :], gcb_ref[2 * j + 1, rows, :])
            bt = jnp.where(lane < CHUNK, bb_ref[2 * j, rows, :], bb_ref[2 * j + 1, rows, :])
            gc_row = jnp.sum(jnp.where(ii == jj, gc, 0.0), axis=0, keepdims=True)
            decay = jnp.where(causal, jnp.exp(jnp.where(causal, gc - gc_row, 0.0)), 0.0)
            a_list.append(jnp.where(strict, bt * gram[CHUNK:] * decay, 0.0))
            pp_ref[j, c] = (gram[:CHUNK] * decay).astype(BF16)
        for (j, c), t_inv in zip(probs, _tri_inverse_pairs(a_list, filler)):
            tp_ref[j, c] = t_inv.astype(BF16)
    while pending:
        filler()
    ring_point(2)

    def block_diag2(m):
        mb = m.astype(BF16)
        zero = jnp.zeros((CHUNK, HEAD_DIM), BF16)
        return jnp.concatenate([jnp.concatenate([mb[:, :HEAD_DIM], zero], axis=1),
                                jnp.concatenate([zero, mb[:, HEAD_DIM:]], axis=1)], axis=0)

    def finish_chunk(c):
        rows = slice(c * CHUNK, (c + 1) * CHUNK)
        for h in range(n_v):
            lo = h * HEAD_DIM
            o = _rms_rows(oscr_ref[h, rows, :], onorm_ref[...])
            o_ref[rows, lo:lo + HEAD_DIM] = (o * _silu(z_ref[rows, lo:lo + HEAD_DIM].astype(F32))).astype(o_ref.dtype)

    for c in range(nchunk):
        if c > 0:
            finish_chunk(c - 1)
        rows = slice(c * CHUNK, (c + 1) * CHUNK)
        heads = range(n_qk)
        pair = lambda ref, j: jnp.concatenate([ref[2 * j, rows, :], ref[2 * j + 1, rows, :]], axis=1)
        qks = [qk_ref[j, c] for j in heads]
        projs = [jnp.dot(qks[j], jnp.concatenate([s_ref[2 * j], s_ref[2 * j + 1]], axis=1).astype(BF16),
                         preferred_element_type=F32) for j in heads]
        gcs = [pair(gcb_ref, j) for j in heads]
        e_gcs = [jnp.exp(gc) for gc in gcs]
        rhss = [pair(bb_ref, j) * (pair(v_ref, j) - e_gcs[j] * projs[j][CHUNK:]) for j in heads]
        v_news = [jnp.dot(tp_ref[j, c], block_diag2(rhss[j]), preferred_element_type=F32) for j in heads]
        outs = [e_gcs[j] * projs[j][:CHUNK]
                + jnp.dot(pp_ref[j, c], block_diag2(v_news[j]), preferred_element_type=F32) for j in heads]
        g_lasts = [gc[CHUNK - 1:CHUNK, :] for gc in gcs]
        v_decs = [(jnp.exp(g_lasts[j] - gcs[j]) * v_news[j]).astype(BF16) for j in heads]
        d_states = [lax.dot_general(qks[j][CHUNK:], v_decs[j], (((0,), (0,)), ((), ())),
                                    preferred_element_type=F32) for j in heads]
        for j in heads:
            for e in range(2):
                lanes = slice(e * HEAD_DIM, (e + 1) * HEAD_DIM)
                oscr_ref[2 * j + e, rows, :] = outs[j][:, lanes]
                s_ref[2 * j + e] = jnp.exp(g_lasts[j][:, lanes]) * s_ref[2 * j + e] + d_states[j][:, lanes]
    ring_point(3)
    finish_chunk(nchunk - 1)

    @pl.when(t_idx == pl.num_programs(1) - 1)
    def _():
        s_out_ref[...] = s_ref[...]

    if with_shift:
        ring.end()


def gdn_prompt(qkvz, gates, conv_w, a_log, dt_bias, o_norm, n_qk, n_v, tb, shift_cache=None):
    b, t, _ = qkvz.shape
    assert n_v == 2 * n_qk and 2 * CHUNK == LANES
    conv_dim = (2 * n_qk + n_v) * HEAD_DIM
    v_dim = n_v * HEAD_DIM
    tb = min(tb, t)
    nchunk = tb // CHUNK
    pad = lambda p: jnp.zeros((1, LANES), F32).at[0, :n_v].set(p.astype(F32))
    extra_in, extra_in_specs, extra_out_specs, extra_out_shape, extra_scratch = [], [], [], [], []
    if shift_cache is not None:
        n_l, n_b, rows = shift_cache.shape[:3]
        assert n_l * n_b == b * (t // tb) and rows % RING_POINTS == 0
        extra_in = [shift_cache]
        extra_in_specs = [pl.BlockSpec(memory_space=pl.ANY)]
        extra_out_specs = [pl.BlockSpec(memory_space=pl.ANY)]
        extra_out_shape = [jax.ShapeDtypeStruct(shift_cache.shape, shift_cache.dtype)]
        extra_scratch = [pltpu.VMEM((RING_SLOTS, rows // RING_POINTS) + shift_cache.shape[3:], shift_cache.dtype),
                         pltpu.VMEM(shift_cache.shape[3:], shift_cache.dtype),
                         pltpu.SemaphoreType.DMA((RING_SLOTS,)), pltpu.SemaphoreType.DMA((RING_SLOTS,)),
                         pltpu.SemaphoreType.DMA(())]
    return pl.pallas_call(
        functools.partial(_gdn_prompt_kernel, n_qk=n_qk, n_v=n_v, with_shift=shift_cache is not None),
        grid=(b, t // tb),
        in_specs=[pl.BlockSpec((None, tb, conv_dim), lambda bi, ti: (bi, ti, 0)),
                  pl.BlockSpec((None, tb, v_dim), lambda bi, ti: (bi, ti, conv_dim // v_dim)),
                  pl.BlockSpec((None, tb, 2 * LANES), lambda bi, ti: (bi, ti, 0)),
                  pl.BlockSpec((CONV_TAPS, conv_dim), lambda bi, ti: (0, 0)),
                  pl.BlockSpec((1, LANES), lambda bi, ti: (0, 0)),
                  pl.BlockSpec((1, LANES), lambda bi, ti: (0, 0)),
                  pl.BlockSpec((1, HEAD_DIM), lambda bi, ti: (0, 0))] + extra_in_specs,
        out_specs=[pl.BlockSpec((None, tb, v_dim), lambda bi, ti: (bi, ti, 0)),
                   pl.BlockSpec((None, n_v, HEAD_DIM, HEAD_DIM), lambda bi, ti: (bi, 0, 0, 0))] + extra_out_specs,
        out_shape=[jax.ShapeDtypeStruct((b, t, v_dim), BF16),
                   jax.ShapeDtypeStruct((b, n_v, HEAD_DIM, HEAD_DIM), F32)] + extra_out_shape,
        scratch_shapes=[pltpu.VMEM((2 * n_qk + n_v, tb + 8, LANES), F32),
                        pltpu.VMEM((n_qk, nchunk, 2 * CHUNK, HEAD_DIM), BF16),
                        pltpu.VMEM((n_v, tb, HEAD_DIM), F32),
                        pltpu.VMEM((2, tb, LANES), F32),
                        pltpu.VMEM((n_v, tb, LANES), F32),
                        pltpu.VMEM((n_v, tb, LANES), F32),
                        pltpu.VMEM((n_qk, nchunk, CHUNK, 2 * CHUNK), BF16),
                        pltpu.VMEM((n_qk, nchunk, CHUNK, 2 * CHUNK), BF16),
                        pltpu.VMEM((n_v, HEAD_DIM, HEAD_DIM), F32),
                        pltpu.VMEM((n_v, tb, HEAD_DIM), F32)] + extra_scratch,
        compiler_params=_params(2),
        name="gdn_prompt",
    )(qkvz, qkvz, gates, conv_w, pad(a_log), pad(dt_bias), o_norm.reshape(1, HEAD_DIM), *extra_in)


GDN_QK_HEADS = 8
GDN_V_HEADS = 16
GDN_CONV_DIM = (2 * GDN_QK_HEADS + GDN_V_HEADS) * HEAD_DIM
GDN_V_DIM = GDN_V_HEADS * HEAD_DIM


def prep_gdn_weights(w_in, w_out):
    d = w_in.shape[0]
    main = GDN_CONV_DIM + GDN_V_DIM
    w_main = w_in[:, :main].astype(BF16)
    w_gate = jnp.zeros((d, 2 * LANES), F32)
    w_gate = w_gate.at[:, :GDN_V_HEADS].set(w_in[:, main:main + GDN_V_HEADS])
    w_gate = w_gate.at[:, LANES:LANES + GDN_V_HEADS].set(w_in[:, main + GDN_V_HEADS:])
    return w_main, w_gate.astype(BF16), w_out.astype(BF16)


def gdn_layer_prompt(x, gain, weights, conv_w, a_log, dt_bias, o_norm, shift_cache=None):
    w_main, w_gate, _ = weights
    b, t, d = x.shape
    qkvz, gates = gdn_inproj(x.reshape(b * t, d), gain, w_main, w_gate, BF16, 512)
    qkvz = qkvz.reshape(b, t, -1)
    o, state, *shifted = gdn_prompt(qkvz, gates.reshape(b, t, -1), conv_w, a_log, dt_bias, o_norm,
                                    GDN_QK_HEADS, GDN_V_HEADS, 256, shift_cache)
    conv_tail = qkvz[:, t - (CONV_TAPS - 1):, :GDN_CONV_DIM].astype(F32)
    return (o.reshape(b * t, -1), conv_tail, state, *shifted)


def _gdn_step_kernel(qkvz_ref, gate_ref, conv_ref, s_ref, cw_ref, alog_ref, dtb_ref, onorm_ref,
                     o_ref, convn_ref, sn_ref, *, n_qk, n_v):
    rep = n_v // n_qk
    conv_dim = (2 * n_qk + n_v) * HEAD_DIM
    x = qkvz_ref[:, :conv_dim]
    y = cw_ref[CONV_TAPS - 1:CONV_TAPS, :] * x
    for j in range(CONV_TAPS - 1):
        y = y + cw_ref[j:j + 1, :] * conv_ref[j:j + 1, :]
    y = _silu(y)
    convn_ref[0:CONV_TAPS - 2, :] = conv_ref[1:CONV_TAPS - 1, :]
    convn_ref[CONV_TAPS - 2:CONV_TAPS - 1, :] = x

    beta = _sigmoid(gate_ref[:, :LANES])
    g = -jnp.exp(alog_ref[...]) * _softplus(gate_ref[:, LANES:] + dtb_ref[...])
    eye = (lax.broadcasted_iota(jnp.int32, (HEAD_DIM, HEAD_DIM), 0)
           == lax.broadcasted_iota(jnp.int32, (HEAD_DIM, HEAD_DIM), 1))

    def column(row):
        return jnp.sum(jnp.where(eye, row, 0.0), axis=1, keepdims=True)

    def l2(row):
        return row * lax.rsqrt(jnp.sum(row * row, axis=-1, keepdims=True) + EPS)

    for j in range(n_qk):
        q_col = column(l2(y[:, j * HEAD_DIM:(j + 1) * HEAD_DIM]) * (HEAD_DIM ** -0.5))
        k_col = column(l2(y[:, (n_qk + j) * HEAD_DIM:(n_qk + j + 1) * HEAD_DIM]))
        for e in range(rep):
            h = rep * j + e
            lo = h * HEAD_DIM
            v = y[:, 2 * n_qk * HEAD_DIM + lo:2 * n_qk * HEAD_DIM + lo + HEAD_DIM]
            s = s_ref[h]
            e_g = jnp.exp(g[:, h:h + 1])
            k_s = jnp.sum(s * k_col, axis=0, keepdims=True)
            v_new = beta[:, h:h + 1] * (v - e_g * k_s)
            s_new = e_g * s + k_col * v_new
            sn_ref[h] = s_new
            o = _rms_rows(jnp.sum(s_new * q_col, axis=0, keepdims=True), onorm_ref[...])
            z = qkvz_ref[:, conv_dim + lo:conv_dim + lo + HEAD_DIM]
            o_ref[:, lo:lo + HEAD_DIM] = (o * _silu(z)).astype(o_ref.dtype)


def gdn_step(qkvz, gates, conv_state, state, conv_w, a_log, dt_bias, o_norm, n_qk, n_v):
    b = qkvz.shape[0]
    conv_dim = (2 * n_qk + n_v) * HEAD_DIM
    v_dim = n_v * HEAD_DIM
    pad = lambda p: jnp.zeros((1, LANES), F32).at[0, :n_v].set(p.astype(F32))
    row = lambda n: pl.BlockSpec((None, 1, n), lambda bi: (bi, 0, 0))
    const = lambda shape: pl.BlockSpec(shape, lambda bi: (0,) * len(shape))
    st = pl.BlockSpec((None, n_v, HEAD_DIM, HEAD_DIM), lambda bi: (bi, 0, 0, 0))
    cv = pl.BlockSpec((None, CONV_TAPS - 1, conv_dim), lambda bi: (bi, 0, 0))
    return pl.pallas_call(
        functools.partial(_gdn_step_kernel, n_qk=n_qk, n_v=n_v),
        grid=(b,),
        in_specs=[row(conv_dim + v_dim), row(2 * LANES), cv, st,
                  const((CONV_TAPS, conv_dim)), const((1, LANES)), const((1, LANES)), const((1, HEAD_DIM))],
        out_specs=[row(v_dim), cv, st],
        out_shape=[jax.ShapeDtypeStruct((b, 1, v_dim), BF16),
                   jax.ShapeDtypeStruct(conv_state.shape, F32),
                   jax.ShapeDtypeStruct(state.shape, F32)],
        compiler_params=_params(1),
        name="gdn_step",
    )(qkvz.reshape(b, 1, -1), gates.reshape(b, 1, -1), conv_state, state,
      conv_w, pad(a_log), pad(dt_bias), o_norm.reshape(1, HEAD_DIM))


def gdn_layer_sample(x, gain, weights, conv_w, a_log, dt_bias, o_norm, conv_state, state):
    w_main, w_gate, _ = weights
    b = x.shape[0]
    qkvz, gates = gdn_inproj(x, gain, w_main, w_gate, F32, b)
    o, conv_new, state_new = gdn_step(qkvz, gates, conv_state, state, conv_w, a_log, dt_bias, o_norm,
                                      GDN_QK_HEADS, GDN_V_HEADS)
    return o.reshape(b, -1), conv_new, state_new


DSW_HEADS = 8
DSW_WIDTH = DSW_HEADS * HEAD_DIM
DSW_TILE = DSW_BLOCK * max(d for _, d in DSW_GROUPS)
DSW_BLOCKS_PER_ITER = 8


def _dsw_attn_kernel(*refs, n_shift):
    n_groups = len(DSW_GROUPS)
    ins = [refs[5 * g:5 * g + 5] for g in range(n_groups)]
    refs = refs[5 * n_groups:]
    shift_ins, refs = refs[:2 * n_shift], refs[2 * n_shift:]
    o_ref, shift_outs, scratch = refs[0], refs[1:1 + n_shift], refs[1 + n_shift:]
    kf_refs, vf_refs = scratch[:n_groups], scratch[n_groups:2 * n_groups]
    og_ref, lg_ref = scratch[2 * n_groups:2 * n_groups + 2]
    ring_scr = scratch[2 * n_groups + 2:]
    step = (pl.program_id(0) * pl.num_programs(1) + pl.program_id(1)) * pl.num_programs(2) + pl.program_id(2)
    n_steps = pl.num_programs(0) * pl.num_programs(1) * pl.num_programs(2)
    rings = [_ShiftRing(step, n_steps, shift_ins[2 * i], shift_outs[i], *ring_scr[4 * i:4 * i + 4],
                        new_ref=shift_ins[2 * i + 1]) for i in range(n_shift)]
    for ring in rings:
        ring.begin()
    first_tile = pl.program_id(2) == 0
    blk = DSW_BLOCK
    qi = lax.broadcasted_iota(jnp.int32, (blk, 2 * blk), 0)
    ki = lax.broadcasted_iota(jnp.int32, (blk, 2 * blk), 1)
    band = (ki >= qi) & (ki <= qi + blk)
    scale = HEAD_DIM ** -0.5

    for g, (_, dil) in enumerate(DSW_GROUPS):
        for ring in rings:
            ring.point(g)
        q_ref, kc_ref, vc_ref, kp_ref, vp_ref = ins[g]
        kf_ref, vf_ref = kf_refs[g], vf_refs[g]
        per_res = q_ref.shape[1] // blk
        kf_ref[:, :blk, :] = kp_ref[...]
        kf_ref[:, blk:, :] = kc_ref[...]
        vf_ref[:, :blk, :] = vp_ref[...]
        vf_ref[:, blk:, :] = vc_ref[...]

        def blocks_body(it, carry, *, g=g, dil=dil, per_res=per_res, q_ref=q_ref, kf_ref=kf_ref, vf_ref=vf_ref):
            ids = [it * DSW_BLOCKS_PER_ITER + i for i in range(DSW_BLOCKS_PER_ITER)]
            rs = [bi // per_res for bi in ids]
            ms = [bi % per_res for bi in ids]
            row0s = [pl.multiple_of(m * blk, blk) for m in ms]
            ss = [lax.dot_general(q_ref[r, pl.ds(row0, blk), :], kf_ref[r, pl.ds(row0, 2 * blk), :],
                                  (((1,), (1,)), ((), ())), preferred_element_type=F32) * scale
                  for r, row0 in zip(rs, row0s)]
            ss = [jnp.where(band & ((ki >= blk) | (m > 0) | jnp.logical_not(first_tile)), s, NEG_INF)
                  for s, m in zip(ss, ms)]
            mxs = [jnp.max(s, axis=-1, keepdims=True) for s in ss]
            ps = [jnp.exp(s - mx) for s, mx in zip(ss, mxs)]
            dens = [jnp.sum(p, axis=-1, keepdims=True) for p in ps]
            os_ = [jnp.dot(p.astype(BF16), vf_ref[r, pl.ds(row0, 2 * blk), :], preferred_element_type=F32)
                   for p, r, row0 in zip(ps, rs, row0s)]
            for o, den, mx, r, row0 in zip(os_, dens, mxs, rs, row0s):
                o = o * (1.0 / den)
                lse = jnp.broadcast_to(mx + jnp.log(den), (blk, LANES))
                start = row0 * dil + r
                rows = pl.ds(start, blk) if dil == 1 else pl.ds(start, blk, stride=dil)
                og_ref[g, rows, :] = o
                lg_ref[g, rows, :] = lse
            return carry

        lax.fori_loop(0, dil * per_res // DSW_BLOCKS_PER_ITER, blocks_body, 0)

    for ring in rings:
        ring.point(n_groups)

    top = lg_ref[0]
    for g in range(1, n_groups):
        top = jnp.maximum(top, lg_ref[g])
    num = jnp.zeros_like(top)
    den = jnp.zeros_like(top)
    for g in range(n_groups):
        w = jnp.exp(lg_ref[g] - top)
        num = num + w * og_ref[g]
        den = den + w
    o_ref[...] = (num * (1.0 / den)).astype(o_ref.dtype)
    for ring in rings:
        ring.end()


def dsw_attn(projs, shifts=()):
    assert RING_POINTS == len(DSW_GROUPS) + 1
    b = projs[0].shape[0]
    t = projs[0].shape[1] * projs[0].shape[2]
    tile = DSW_TILE
    heads = DSW_HEADS
    in_specs, args, kv_scratch = [], [], []
    for (_, dil), p in zip(DSW_GROUPS, projs):
        rows = tile // dil
        per_res = rows // DSW_BLOCK
        cur = lambda col: pl.BlockSpec((None, dil, rows, HEAD_DIM),
                                       functools.partial(lambda bi, h, n, col: (bi, 0, n, col * heads + h), col=col))
        prev = lambda col: pl.BlockSpec(
            (None, dil, DSW_BLOCK, HEAD_DIM),
            functools.partial(lambda bi, h, n, col, per_res: (bi, 0, jnp.maximum(n * per_res - 1, 0), col * heads + h),
                              col=col, per_res=per_res))
        in_specs += [cur(0), cur(1), cur(2), prev(1), prev(2)]
        args += [p] * 5
        kv_scratch.append(pltpu.VMEM((dil, DSW_BLOCK + rows, HEAD_DIM), BF16))
    n_groups = len(DSW_GROUPS)
    shift_out_specs, shift_out_shape, ring_scratch = [], [], []
    for cache, new in shifts:
        n_l, n_b, rows = cache.shape[:3]
        assert n_l * n_b == b * heads * (t // tile) and rows % RING_POINTS == 0
        in_specs += [pl.BlockSpec(memory_space=pl.ANY), _resident(new.shape)]
        args += [cache, new]
        shift_out_specs.append(pl.BlockSpec(memory_space=pl.ANY))
        shift_out_shape.append(jax.ShapeDtypeStruct(cache.shape, cache.dtype))
        ring_scratch += [pltpu.VMEM((RING_SLOTS, rows // RING_POINTS) + cache.shape[3:], cache.dtype),
                         pltpu.SemaphoreType.DMA((RING_SLOTS,)), pltpu.SemaphoreType.DMA((RING_SLOTS,)),
                         pltpu.SemaphoreType.DMA(())]
    outs = pl.pallas_call(
        functools.partial(_dsw_attn_kernel, n_shift=len(shifts)),
        grid=(b, heads, t // tile),
        in_specs=in_specs,
        out_specs=[pl.BlockSpec((None, tile, HEAD_DIM), lambda bi, h, n: (bi, n, h))] + shift_out_specs,
        out_shape=[jax.ShapeDtypeStruct((b, t, heads * HEAD_DIM), BF16)] + shift_out_shape,
        scratch_shapes=kv_scratch + kv_scratch + [pltpu.VMEM((n_groups, tile, HEAD_DIM), F32),
                                                  pltpu.VMEM((n_groups, tile, LANES), F32)] + ring_scratch,
        compiler_params=_params(3),
        name="dsw_attn",
    )(*args)
    return outs[0], list(outs[1:])


def prep_dsw_weights(w_in, w_out):
    d = w_in.shape[0]
    w = w_in.reshape(d, len(DSW_GROUPS), 3, DSW_WIDTH).transpose(1, 2, 0, 3).astype(BF16)
    return [w[g] for g in range(len(DSW_GROUPS))], w_out.astype(BF16)


def dsw_layer_prompt(x, gain, weights, q_norm, k_norm, shifts=()):
    w_groups, _ = weights
    b, t, d = x.shape
    projs = [dsw_inproj(x, gain, jnp.stack([q_norm[g], k_norm[g]]), w_groups[g], dil,
                        max(DSW_INPROJ_ROWS, DSW_INPROJ_MIN_SUB * dil))
             for g, (_, dil) in enumerate(DSW_GROUPS)]
    o, shifted = dsw_attn(projs, shifts)
    return o.reshape(b * t, -1), projs, shifted


CACHE_RESIDUES = 4


def _dsw_cache_kernel(*refs):
    *p_refs, o_ref, slab_scr = refs
    layer = pl.program_id(0)
    rows, n_res = o_ref.shape[:2]
    for li in range(len(p_refs) // 2):
        @pl.when(layer == li)
        def _(k_ref=p_refs[2 * li], v_ref=p_refs[2 * li + 1]):
            for rr in range(n_res):
                for kv, p_ref in enumerate((k_ref, v_ref)):
                    slab = slab_scr.at[2 * rr + kv]
                    slab[...] = p_ref[rr].astype(o_ref.dtype).reshape(rows, DSW_HEADS, HEAD_DIM)
                    o_ref[:, rr, kv] = slab[...]


def dsw_prompt_caches(projs_layers):
    n_layers = len(projs_layers)
    outs = []
    for g, (window, dil) in enumerate(DSW_GROUPS):
        ps = [pl_[g] for pl_ in projs_layers]
        b, _, sub, _ = ps[0].shape
        last = sub // DSW_BLOCK - 1
        n_res = min(dil, CACHE_RESIDUES)

        def in_map(l, bi, rb, *, li, col):
            before, after = l < li, l > li
            pick = lambda lo, x, hi: jnp.where(before, lo, jnp.where(after, hi, x))
            return (pick(0, bi, b - 1), pick(0, rb, dil // n_res - 1), last, col)

        out = pl.pallas_call(
            _dsw_cache_kernel,
            grid=(n_layers, b, dil // n_res),
            in_specs=[pl.BlockSpec((None, n_res, DSW_BLOCK, DSW_WIDTH), functools.partial(in_map, li=li, col=col))
                      for li in range(n_layers) for col in (1, 2)],
            out_specs=pl.BlockSpec((None, None, DSW_BLOCK, n_res, 2, DSW_HEADS, HEAD_DIM),
                                   lambda l, bi, rb: (l, bi, 0, rb, 0, 0, 0)),
            out_shape=jax.ShapeDtypeStruct((n_layers, b, DSW_BLOCK, dil, 2, DSW_HEADS, HEAD_DIM), F32),
            scratch_shapes=[pltpu.VMEM((2 * n_res, DSW_BLOCK, DSW_HEADS, HEAD_DIM), F32)],
            compiler_params=_params(3),
            name="dsw_prompt_cache",
        )(*[p for p in ps for _ in (1, 2)])
        outs.append(out.reshape(n_layers, b, window, 2, DSW_HEADS, HEAD_DIM))
    return outs


def _dsw_decode_kernel(p_ref, *refs):
    n_groups = len(DSW_GROUPS)
    c_refs = refs[:n_groups]
    o_ref, new_ref = refs[n_groups:]
    scale = HEAD_DIM ** -0.5
    outs, lses = [], []
    for g in range(n_groups):
        q = p_ref[g, 0].astype(F32)
        k_new = p_ref[g, 1].astype(F32)
        v_new = p_ref[g, 2].astype(F32)
        new_ref[g, 0] = k_new
        new_ref[g, 1] = v_new
        s = jnp.sum(c_refs[g][:, 0] * q[None], axis=-1, keepdims=True) * scale
        s_new = jnp.sum(k_new * q, axis=-1, keepdims=True) * scale
        mx = jnp.maximum(jnp.max(s, axis=0), s_new)
        p = jnp.exp(s - mx[None])
        p_new = jnp.exp(s_new - mx)
        den = jnp.sum(p, axis=0) + p_new
        outs.append((jnp.sum(p * c_refs[g][:, 1], axis=0) + p_new * v_new) * (1.0 / den))
        lses.append(mx + jnp.log(den))
    top = functools.reduce(jnp.maximum, lses)
    ws = [jnp.exp(l - top) for l in lses]
    num = sum(w * o for w, o in zip(ws, outs))
    o_ref[...] = (num * (1.0 / sum(ws))).astype(o_ref.dtype)


def dsw_decode(proj, caches, layer):
    b = proj.shape[0]
    n_groups = len(DSW_GROUPS)
    c_specs, c_args = [], []
    for (window, dil), c in zip(DSW_GROUPS, caches):
        c_args.append(c.reshape(c.shape[0], b, window // dil, dil, 2, DSW_HEADS, HEAD_DIM))
        c_specs.append(pl.BlockSpec((None, None, window // dil, None, 2, DSW_HEADS, HEAD_DIM),
                                    lambda bi: (layer, bi, 0, 0, 0, 0, 0)))
    return pl.pallas_call(
        _dsw_decode_kernel,
        grid=(b,),
        in_specs=[pl.BlockSpec((None, n_groups, 3, DSW_HEADS, HEAD_DIM), lambda bi: (bi, 0, 0, 0, 0))] + c_specs,
        out_specs=[pl.BlockSpec((None, DSW_HEADS, HEAD_DIM), lambda bi: (bi, 0, 0)),
                   pl.BlockSpec((None, n_groups, 2, DSW_HEADS, HEAD_DIM), lambda bi: (bi, 0, 0, 0, 0))],
        out_shape=[jax.ShapeDtypeStruct((b, DSW_HEADS, HEAD_DIM), BF16),
                   jax.ShapeDtypeStruct((b, n_groups, 2, DSW_HEADS, HEAD_DIM), F32)],
        compiler_params=_params(1),
        name="dsw_decode",
    )(proj, *c_args)


def dsw_layer_sample(x, gain, weights, q_norm, k_norm, caches, layer):
    w_groups, _ = weights
    b, d = x.shape
    proj = jnp.stack([dsw_inproj(x.reshape(1, b, d), gain, jnp.stack([q_norm[g], k_norm[g]]), w_groups[g], 1, b)
                      .reshape(b, 3, DSW_HEADS, HEAD_DIM) for g in range(len(DSW_GROUPS))], axis=1)
    o, new_rows = dsw_decode(proj, caches, layer)
    return o.reshape(b, -1), new_rows


def _last_row_kernel(shifted_ref, new_ref, o_ref):
    del shifted_ref
    o_ref[0] = new_ref[...]


def write_last_rows(shifted, new_rows_layers, g):
    new = jnp.stack(new_rows_layers)
    n_layers, b, window = shifted.shape[:3]
    tile = shifted.shape[3:]
    return pl.pallas_call(
        _last_row_kernel,
        grid=(n_layers, b),
        in_specs=[pl.BlockSpec(memory_space=pl.ANY),
                  pl.BlockSpec((None, None, None) + tile, lambda l, bi: (l, bi, g, 0, 0, 0))],
        out_specs=pl.BlockSpec((None, None, 1) + tile, lambda l, bi: (l, bi, window - 1, 0, 0, 0)),
        out_shape=jax.ShapeDtypeStruct(shifted.shape, shifted.dtype),
        input_output_aliases={0: 0},
        compiler_params=_params(2),
        name="cache_last_row",
    )(shifted, new)


def kernel(x_prompt, x_sample, state_gdn, state_conv, cache_kv_w128, cache_kv_w512, cache_kv_w2048,
           norm_mix, norm_mlp, gdn_w_in, gdn_conv_w, gdn_a_log, gdn_dt_bias, gdn_o_norm, gdn_w_out,
           dsw_w_in, dsw_q_norm, dsw_k_norm, dsw_w_out, mlp_w_up, mlp_w_down):
    b, t, d = x_prompt.shape
    bs = x_sample.shape[0]
    depth = norm_mix.shape[0]
    caches = [cache_kv_w128, cache_kv_w512, cache_kv_w2048]
    yp = x_prompt.reshape(b * t, d)
    ys = x_sample.reshape(bs, d)
    p_gdn, p_conv, s_gdn, s_conv, p_projs, s_rows = [], [], [], [], [], []
    for i in range(depth):
        j = i // 2
        if i % 2 == 0:
            wts = prep_gdn_weights(gdn_w_in[j], gdn_w_out[j])
            par = (gdn_conv_w[j], gdn_a_log[j], gdn_dt_bias[j], gdn_o_norm[j])
            host = i == 2 * ((depth - 1) // 2)
            op, conv_p, state_p, *shifted = gdn_layer_prompt(yp.reshape(b, t, d), norm_mix[i], wts, *par,
                                                             caches[-1] if host else None)
            if host:
                shifted_big = shifted[0]
            os_, conv_s, state_s = gdn_layer_sample(ys, norm_mix[i], wts, *par, state_conv[j], state_gdn[j])
            p_gdn.append(state_p)
            p_conv.append(conv_p)
            s_gdn.append(state_s)
            s_conv.append(conv_s)
        else:
            wts = prep_dsw_weights(dsw_w_in[j], dsw_w_out[j])
            os_, rows = dsw_layer_sample(ys, norm_mix[i], wts, dsw_q_norm[j], dsw_k_norm[j], caches, j)
            s_rows.append(rows)
            shifts = ()
            if len(s_rows) == caches[0].shape[0]:
                new = jnp.stack(s_rows)
                shifts = tuple((caches[g], new[:, :, g]) for g in range(len(caches) - 1))
            op, projs, shifted = dsw_layer_prompt(yp.reshape(b, t, d), norm_mix[i], wts, dsw_q_norm[j], dsw_k_norm[j],
                                                  shifts)
            if shifts:
                shifted_small = shifted
            p_projs.append(projs)
        w_up, w_down = mlp_w_up[i].astype(BF16), mlp_w_down[i].astype(BF16)
        yp = mixer_mlp(yp, op, wts[-1], norm_mlp[i], w_up, w_down, 512, 1024)
        ys = mixer_mlp(ys, os_, wts[-1], norm_mlp[i], w_up, w_down, bs, 1024)
    yp = yp.reshape(b, t, d)
    p_kv = dsw_prompt_caches(p_projs)
    s_kv = shifted_small + [write_last_rows(shifted_big, s_rows, len(caches) - 1)]
    return (yp, ys.reshape(x_sample.shape),
            jnp.stack(p_gdn), jnp.stack(p_conv), p_kv[0], p_kv[1], p_kv[2],
            jnp.stack(s_gdn), jnp.stack(s_conv), s_kv[0], s_kv[1], s_kv[2])
```

```python
import functools

import jax
import jax.numpy as jnp
from jax import lax
from jax.experimental import pallas as pl
from jax.experimental.pallas import tpu as pltpu

F32 = jnp.float32
BF16 = jnp.bfloat16

EPS = 1e-6
NEG_INF = -1e30
LANES = 128
CONV_TAPS = 4
CHUNK = 64
HEAD_DIM = 128
DSW_BLOCK = 128
DSW_GROUPS = ((128, 1), (512, 4), (2048, 16))
VMEM_LIMIT = 56 * 1024 * 1024


def _params(n_axes, vmem=VMEM_LIMIT):
    return pltpu.CompilerParams(dimension_semantics=("arbitrary",) * n_axes,
                                vmem_limit_bytes=vmem)


def _sigmoid(x):
    return 0.5 + 0.5 * jnp.tanh(0.5 * x)


def _silu(x):
    h = 0.5 * x
    return h + h * jnp.tanh(h)


def _softplus(x):
    return jnp.maximum(x, 0.0) + jnp.log(1.0 + jnp.exp(-jnp.abs(x)))


def _rms_rows(x, gain_row):
    ms = jnp.mean(x * x, axis=-1, keepdims=True)
    return x * lax.rsqrt(ms + EPS) * gain_row


INPROJ_COLS = 512
MLP_ROWS = 512
MLP_COLS = 1024


def _gdn_inproj_kernel(x_ref, g_ref, w_ref, wg_ref, o_ref, og_ref, xn_ref):
    xn_ref[...] = _rms_rows(x_ref[...], g_ref[...]).astype(BF16)
    for lo in range(0, w_ref.shape[1], INPROJ_COLS):
        o_ref[:, lo:lo + INPROJ_COLS] = jnp.dot(
            xn_ref[...], w_ref[:, lo:lo + INPROJ_COLS], preferred_element_type=F32).astype(o_ref.dtype)
    og_ref[...] = jnp.dot(xn_ref[...], wg_ref[...], preferred_element_type=F32)


def gdn_inproj(x, gain, w_main, w_gate, out_dtype, tm):
    m, k = x.shape
    n, ng = w_main.shape[1], w_gate.shape[1]
    tm = min(tm, m)
    return pl.pallas_call(
        _gdn_inproj_kernel,
        grid=(m // tm,),
        in_specs=[pl.BlockSpec((tm, k), lambda i: (i, 0)),
                  pl.BlockSpec((1, k), lambda i: (0, 0)),
                  _resident((k, n)), _resident((k, ng))],
        out_specs=[pl.BlockSpec((tm, n), lambda i: (i, 0)),
                   pl.BlockSpec((tm, ng), lambda i: (i, 0))],
        out_shape=[jax.ShapeDtypeStruct((m, n), out_dtype), jax.ShapeDtypeStruct((m, ng), F32)],
        scratch_shapes=[pltpu.VMEM((tm, k), BF16)],
        compiler_params=_params(1),
        name="gdn_inproj",
    )(x, gain.reshape(1, k), w_main, w_gate)


def _mixer_mlp_kernel(x_ref, a_ref, wo_ref, g_ref, wu_ref, wd_ref, o_ref, xn_ref, *, tf):
    x1 = x_ref[...] + jnp.dot(a_ref[...], wo_ref[...], preferred_element_type=F32)
    xn_ref[...] = _rms_rows(x1, g_ref[...]).astype(BF16)
    o_ref[...] = x1
    for lo in range(0, wu_ref.shape[1], tf):
        h = jnp.dot(xn_ref[...], wu_ref[:, lo:lo + tf], preferred_element_type=F32)
        h = jnp.square(jnp.maximum(h, 0.0)).astype(BF16)
        o_ref[...] += jnp.dot(h, wd_ref[lo:lo + tf, :], preferred_element_type=F32)


def _resident(shape):
    return pl.BlockSpec(shape, lambda *_: (0,) * len(shape), pipeline_mode=pl.Buffered(1))


def mixer_mlp(x, a, w_out, gain, w_up, w_down, tm, tf):
    m, d = x.shape
    ka = a.shape[1]
    ff = w_up.shape[1]
    tm = min(tm, m)
    return pl.pallas_call(
        functools.partial(_mixer_mlp_kernel, tf=tf),
        grid=(m // tm,),
        in_specs=[pl.BlockSpec((tm, d), lambda i: (i, 0)),
                  pl.BlockSpec((tm, ka), lambda i: (i, 0)),
                  _resident((ka, d)), _resident((1, d)), _resident((d, ff)), _resident((ff, d))],
        out_specs=pl.BlockSpec((tm, d), lambda i: (i, 0)),
        out_shape=jax.ShapeDtypeStruct((m, d), F32),
        scratch_shapes=[pltpu.VMEM((tm, d), BF16)],
        compiler_params=_params(1),
        name="mixer_mlp",
    )(x, a, w_out, gain.reshape(1, d), w_up, w_down)


DEINTERLEAVE_STRIDE = 4
DSW_INPROJ_ROWS = 1024
DSW_INPROJ_MIN_SUB = 64


def _dsw_inproj_kernel(x_ref, g_ref, hn_ref, w_ref, o_ref, xn_ref, head_scr, part_scr, *, dil):
    rows = x_ref.shape[0]
    sub = rows // dil
    width = w_ref.shape[2]
    xn_ref[...] = _rms_rows(x_ref[...], g_ref[...]).astype(BF16)
    for t in range(3):
        acc = jnp.dot(xn_ref[...], w_ref[t], preferred_element_type=F32)
        for h in range(width // HEAD_DIM):
            a = acc[:, h * HEAD_DIM:(h + 1) * HEAD_DIM]
            if t < 2:
                a = _rms_rows(a, hn_ref[t:t + 1, :])
            cols = slice(t * width + h * HEAD_DIM, t * width + (h + 1) * HEAD_DIM)
            if dil == 1:
                o_ref[0, :, cols] = a.astype(o_ref.dtype)
            elif dil <= DEINTERLEAVE_STRIDE:
                head_scr[h] = a
                for r in range(dil):
                    o_ref[r, :, cols] = head_scr[h, pl.ds(r, sub, stride=dil), :].astype(o_ref.dtype)
            else:
                st = DEINTERLEAVE_STRIDE
                part = rows // st
                head_scr[h] = a
                for q in range(st):
                    part_scr[h, q * part:(q + 1) * part, :] = head_scr[h, pl.ds(q, part, stride=st), :]
                for q in range(st):
                    for s in range(dil // st):
                        o_ref[st * s + q, :, cols] = part_scr[
                            h, pl.ds(q * part + s, sub, stride=dil // st), :].astype(o_ref.dtype)


def dsw_inproj(x, gain, head_gains, w3, dil, tile_rows):
    b, t, d = x.shape
    width = w3.shape[2]
    rows = min(tile_rows, t)
    return pl.pallas_call(
        functools.partial(_dsw_inproj_kernel, dil=dil),
        grid=(b, t // rows),
        in_specs=[pl.BlockSpec((None, rows, d), lambda bi, i: (bi, i, 0)),
                  pl.BlockSpec((1, d), lambda bi, i: (0, 0)),
                  pl.BlockSpec((2, HEAD_DIM), lambda bi, i: (0, 0)),
                  _resident((3, d, width))],
        out_specs=pl.BlockSpec((None, dil, rows // dil, 3 * width), lambda bi, i: (bi, 0, i, 0)),
        out_shape=jax.ShapeDtypeStruct((b, dil, t // dil, 3 * width), BF16),
        scratch_shapes=[pltpu.VMEM((rows, d), BF16),
                        pltpu.VMEM((width // HEAD_DIM, rows, HEAD_DIM), F32),
                        pltpu.VMEM((width // HEAD_DIM, rows, HEAD_DIM), F32)],
        compiler_params=_params(2),
        name="dsw_inproj",
    )(x, gain.reshape(1, d), head_gains, w3)


TRI_BASE = 8


def _pair_block_diag(m):
    n = m.shape[0]
    mb = m.astype(BF16)
    lane = lax.broadcasted_iota(jnp.int32, mb.shape, 1)
    zero = jnp.zeros_like(mb)
    return jnp.concatenate([jnp.where(lane < n, mb, zero), jnp.where(lane >= n, mb, zero)], axis=0)


def _pair_mm(u, w):
    return jnp.dot(u.astype(BF16), _pair_block_diag(w), preferred_element_type=F32)


def _tri_inverse_pairs(a_list, between=lambda: None):
    n = a_list[0].shape[0]
    ii = lax.broadcasted_iota(jnp.int32, (n, 2 * n), 0)
    jj = lax.broadcasted_iota(jnp.int32, (n, 2 * n), 1) % n
    same_block = lambda size: (ii // size) == (jj // size)
    ps = [jnp.where(same_block(TRI_BASE), a, 0.0) for a in a_list]
    xs = [jnp.where(ii == jj, 1.0, 0.0) - p for p in ps]
    for _ in range((TRI_BASE - 1).bit_length() - 1):
        ps = [_pair_mm(p, p) for p in ps]
        between()
        xs = [x + _pair_mm(x, p) for x, p in zip(xs, ps)]
        between()
    size = TRI_BASE
    while size < n:
        off = same_block(2 * size) & jnp.logical_not(same_block(size))
        ys = [_pair_mm(x, jnp.where(off, a, 0.0)) for x, a in zip(xs, a_list)]
        between()
        xs = [x - _pair_mm(y, x) for x, y in zip(xs, ys)]
        between()
        size *= 2
    return xs


RING_POINTS = 4
RING_SLOTS = 4


class _ShiftRing:
    def __init__(self, step, n_steps, cache_ref, out_ref, stage_ref, sem_in, sem_out, sem_row,
                 zero_ref=None, new_ref=None, first_pair=0):
        self.step, self.n_steps, self.first_pair = step, n_steps, first_pair
        self.cache, self.out, self.stage, self.zero, self.new = cache_ref, out_ref, stage_ref, zero_ref, new_ref
        self.sem_in, self.sem_out, self.sem_row = sem_in, sem_out, sem_row
        self.batch, self.rows = cache_ref.shape[1], cache_ref.shape[2]
        self.chunk = self.rows // RING_POINTS

    def _n_rows(self, q):
        return self.chunk if q < RING_POINTS - 1 else self.chunk - 1

    def _pair(self, step):
        pair = self.first_pair + step
        return pair // self.batch, pair % self.batch

    def _in(self, step, q, slot):
        n = self._n_rows(q)
        l, b = self._pair(step)
        return pltpu.make_async_copy(self.cache.at[l, b, pl.ds(1 + self.chunk * q, n)],
                                     self.stage.at[slot, pl.ds(0, n)], self.sem_in.at[slot])

    def _out(self, step, q, slot):
        n = self._n_rows(q)
        l, b = self._pair(step)
        return pltpu.make_async_copy(self.stage.at[slot, pl.ds(0, n)],
                                     self.out.at[l, b, pl.ds(self.chunk * q, n)], self.sem_out.at[slot])

    def _row(self):
        l, b = self._pair(self.step)
        src = self.zero if self.new is None else self.new.at[l, b]
        return pltpu.make_async_copy(src, self.out.at[l, b, self.rows - 1], self.sem_row)

    def begin(self):
        step = self.step
        if self.new is None:
            self.zero[...] = jnp.zeros_like(self.zero)

        @pl.when(step == 0)
        def _():
            for n in range(RING_SLOTS - 1):
                self._in(step + n // RING_POINTS, n % RING_POINTS, n).start()

        self._row().start(priority=1)

    def point(self, q):
        step = self.step
        ahead = RING_SLOTS - 1
        n = step * RING_POINTS + q
        slot = n % RING_SLOTS
        self._in(step, q, slot).wait()
        self._out(step, q, slot).start(priority=1)
        prev_step, prev_q = (step, q - 1) if q > 0 else (step - 1, RING_POINTS - 1)
        free_slot = (n + ahead) % RING_SLOTS

        @pl.when(n > 0)
        def _():
            self._out(prev_step, prev_q, free_slot).wait()

        next_step, next_q = step + (q + ahead) // RING_POINTS, (q + ahead) % RING_POINTS

        @pl.when(n + ahead < self.n_steps * RING_POINTS)
        def _():
            self._in(next_step, next_q, free_slot).start()

    def end(self):
        step = self.step
        self._row().wait()

        @pl.when(step == self.n_steps - 1)
        def _():
            last = self.n_steps * RING_POINTS - 1
            self._out(step, RING_POINTS - 1, last % RING_SLOTS).wait()


def _gdn_prompt_kernel(*refs, n_qk, n_v, with_shift):
    if with_shift:
        (qkv_ref, z_ref, gate_ref, cw_ref, alog_ref, dtb_ref, onorm_ref, cache_ref,
         o_ref, s_out_ref, shift_ref,
         xe_ref, qk_ref, v_ref, gate_scr, bb_ref, gcb_ref, tp_ref, pp_ref, s_ref, oscr_ref,
         stage_ref, zero_ref, sem_in, sem_out, sem_row) = refs
    else:
        (qkv_ref, z_ref, gate_ref, cw_ref, alog_ref, dtb_ref, onorm_ref,
         o_ref, s_out_ref,
         xe_ref, qk_ref, v_ref, gate_scr, bb_ref, gcb_ref, tp_ref, pp_ref, s_ref, oscr_ref) = refs
    tb = qkv_ref.shape[0]
    nchunk = tb // CHUNK
    t_idx = pl.program_id(1)
    halo = 8
    ring_point = lambda q: None
    if with_shift:
        ring = _ShiftRing(pl.program_id(0) * pl.num_programs(1) + t_idx, pl.num_programs(0) * pl.num_programs(1),
                          cache_ref, shift_ref, stage_ref, sem_in, sem_out, sem_row, zero_ref=zero_ref)
        ring.begin()
        ring_point = ring.point

    @pl.when(t_idx == 0)
    def _():
        xe_ref[:, halo - (CONV_TAPS - 1):halo, :] = jnp.zeros((xe_ref.shape[0], CONV_TAPS - 1, LANES), F32)
        s_ref[...] = jnp.zeros_like(s_ref)

    ring_point(0)

    gate_scr[0] = _sigmoid(gate_ref[:, :LANES])
    g = -jnp.exp(alog_ref[...]) * _softplus(gate_ref[:, LANES:] + dtb_ref[...])
    pos = lax.broadcasted_iota(jnp.int32, (tb, LANES), 0) % CHUNK
    shift = 1
    while shift < CHUNK:
        g = g + jnp.where(pos >= shift, pltpu.roll(g, shift, axis=0), 0.0)
        shift *= 2
    gate_scr[1] = g
    for h in range(n_v):
        bb_ref[h] = jnp.broadcast_to(gate_scr[0, :, h:h + 1], (tb, LANES))
        gcb_ref[h] = jnp.broadcast_to(gate_scr[1, :, h:h + 1], (tb, LANES))

    def conv_block(cb):
        lo = cb * LANES
        xe_ref[cb, halo:halo + tb, :] = qkv_ref[:, lo:lo + LANES].astype(F32)
        y = cw_ref[0:1, lo:lo + LANES] * xe_ref[cb, halo - 3:halo - 3 + tb, :]
        for j in range(1, CONV_TAPS):
            y = y + cw_ref[j:j + 1, lo:lo + LANES] * xe_ref[cb, halo - 3 + j:halo - 3 + j + tb, :]
        y = _silu(y)
        xe_ref[cb, halo - 3:halo, :] = xe_ref[cb, halo + tb - 3:halo + tb, :]
        if cb < 2 * n_qk:
            y = y * lax.rsqrt(jnp.sum(y * y, axis=-1, keepdims=True) + EPS)
            if cb < n_qk:
                y = y * (HEAD_DIM ** -0.5)
            yb = y.astype(BF16)
            half = 0 if cb < n_qk else CHUNK
            for c in range(nchunk):
                qk_ref[cb % n_qk, c, half:half + CHUNK, :] = yb[c * CHUNK:(c + 1) * CHUNK, :]
        else:
            v_ref[cb - 2 * n_qk] = y

    for cb in range(2 * n_qk):
        conv_block(cb)
    ring_point(1)
    pending = [functools.partial(conv_block, cb) for cb in range(2 * n_qk, 2 * n_qk + n_v)]

    def filler():
        if pending:
            pending.pop(0)()

    ii = lax.broadcasted_iota(jnp.int32, (CHUNK, 2 * CHUNK), 0)
    lane = lax.broadcasted_iota(jnp.int32, (CHUNK, 2 * CHUNK), 1)
    jj = lane % CHUNK
    causal = ii >= jj
    strict = ii > jj
    pairs_per_iter = 4

    for it in range(n_qk // pairs_per_iter):
        probs = [(it * pairs_per_iter + dj, c) for dj in range(pairs_per_iter) for c in range(nchunk)]
        grams, a_list = [], []
        for j, c in probs:
            qk = qk_ref[j, c]
            k2 = jnp.concatenate([qk[CHUNK:], qk[CHUNK:]], axis=0)
            grams.append(lax.dot_general(qk, k2, (((1,), (1,)), ((), ())), preferred_element_type=F32))
        filler()
        for (j, c), gram in zip(probs, grams):
            rows = slice(c * CHUNK, (c + 1) * CHUNK)
            gc = jnp.where(lane < CHUNK, gcb_ref[2 * j, rows, :], gcb_ref[2 * j + 1, rows, :])
            bt = jnp.where(lane < CHUNK, bb_ref[2 * j, rows, :], bb_ref[2 * j + 1, rows, :])
            gc_row = jnp.sum(jnp.where(ii == jj, gc, 0.0), axis=0, keepdims=True)
            decay = jnp.where(causal, jnp.exp(jnp.where(causal, gc - gc_row, 0.0)), 0.0)
            a_list.append(jnp.where(strict, bt * gram[CHUNK:] * decay, 0.0))
            pp_ref[j, c] = (gram[:CHUNK] * decay).astype(BF16)
        for (j, c), t_inv in zip(probs, _tri_inverse_pairs(a_list, filler)):
            tp_ref[j, c] = t_inv.astype(BF16)
    while pending:
        filler()
    ring_point(2)

    def block_diag2(m):
        mb = m.astype(BF16)
        zero = jnp.zeros((CHUNK, HEAD_DIM), BF16)
        return jnp.concatenate([jnp.concatenate([mb[:, :HEAD_DIM], zero], axis=1),
                                jnp.concatenate([zero, mb[:, HEAD_DIM:]], axis=1)], axis=0)

    def finish_chunk(c):
        rows = slice(c * CHUNK, (c + 1) * CHUNK)
        for h in range(n_v):
            lo = h * HEAD_DIM
            o = _rms_rows(oscr_ref[h, rows, :], onorm_ref[...])
            o_ref[rows, lo:lo + HEAD_DIM] = (o * _silu(z_ref[rows, lo:lo + HEAD_DIM].astype(F32))).astype(o_ref.dtype)

    for c in range(nchunk):
        if c > 0:
            finish_chunk(c - 1)
        rows = slice(c * CHUNK, (c + 1) * CHUNK)
        heads = range(n_qk)
        pair = lambda ref, j: jnp.concatenate([ref[2 * j, rows, :], ref[2 * j + 1, rows, :]], axis=1)
        qks = [qk_ref[j, c] for j in heads]
        projs = [jnp.dot(qks[j], jnp.concatenate([s_ref[2 * j], s_ref[2 * j + 1]], axis=1).astype(BF16),
                         preferred_element_type=F32) for j in heads]
        gcs = [pair(gcb_ref, j) for j in heads]
        e_gcs = [jnp.exp(gc) for gc in gcs]
        rhss = [pair(bb_ref, j) * (pair(v_ref, j) - e_gcs[j] * projs[j][CHUNK:]) for j in heads]
        v_news = [jnp.dot(tp_ref[j, c], block_diag2(rhss[j]), preferred_element_type=F32) for j in heads]
        outs = [e_gcs[j] * projs[j][:CHUNK]
                + jnp.dot(pp_ref[j, c], block_diag2(v_news[j]), preferred_element_type=F32) for j in heads]
        g_lasts = [gc[CHUNK - 1:CHUNK, :] for gc in gcs]
        v_decs = [(jnp.exp(g_lasts[j] - gcs[j]) * v_news[j]).astype(BF16) for j in heads]
        d_states = [lax.dot_general(qks[j][CHUNK:], v_decs[j], (((0,), (0,)), ((), ())),
                                    preferred_element_type=F32) for j in heads]
        for j in heads:
            for e in range(2):
                lanes = slice(e * HEAD_DIM, (e + 1) * HEAD_DIM)
                oscr_ref[2 * j + e, rows, :] = outs[j][:, lanes]
                s_ref[2 * j + e] = jnp.exp(g_lasts[j][:, lanes]) * s_ref[2 * j + e] + d_states[j][:, lanes]
    ring_point(3)
    finish_chunk(nchunk - 1)

    @pl.when(t_idx == pl.num_programs(1) - 1)
    def _():
        s_out_ref[...] = s_ref[...]

    if with_shift:
        ring.end()


def gdn_prompt(qkvz, gates, conv_w, a_log, dt_bias, o_norm, n_qk, n_v, tb, shift_cache=None):
    b, t, _ = qkvz.shape
    assert n_v == 2 * n_qk and 2 * CHUNK == LANES
    conv_dim = (2 * n_qk + n_v) * HEAD_DIM
    v_dim = n_v * HEAD_DIM
    tb = min(tb, t)
    nchunk = tb // CHUNK
    pad = lambda p: jnp.zeros((1, LANES), F32).at[0, :n_v].set(p.astype(F32))
    extra_in, extra_in_specs, extra_out_specs, extra_out_shape, extra_scratch = [], [], [], [], []
    if shift_cache is not None:
        n_l, n_b, rows = shift_cache.shape[:3]
        assert n_l * n_b == b * (t // tb) and rows % RING_POINTS == 0
        extra_in = [shift_cache]
        extra_in_specs = [pl.BlockSpec(memory_space=pl.ANY)]
        extra_out_specs = [pl.BlockSpec(memory_space=pl.ANY)]
        extra_out_shape = [jax.ShapeDtypeStruct(shift_cache.shape, shift_cache.dtype)]
        extra_scratch = [pltpu.VMEM((RING_SLOTS, rows // RING_POINTS) + shift_cache.shape[3:], shift_cache.dtype),
                         pltpu.VMEM(shift_cache.shape[3:], shift_cache.dtype),
                         pltpu.SemaphoreType.DMA((RING_SLOTS,)), pltpu.SemaphoreType.DMA((RING_SLOTS,)),
                         pltpu.SemaphoreType.DMA(())]
    return pl.pallas_call(
        functools.partial(_gdn_prompt_kernel, n_qk=n_qk, n_v=n_v, with_shift=shift_cache is not None),
        grid=(b, t // tb),
        in_specs=[pl.BlockSpec((None, tb, conv_dim), lambda bi, ti: (bi, ti, 0)),
                  pl.BlockSpec((None, tb, v_dim), lambda bi, ti: (bi, ti, conv_dim // v_dim)),
                  pl.BlockSpec((None, tb, 2 * LANES), lambda bi, ti: (bi, ti, 0)),
                  pl.BlockSpec((CONV_TAPS, conv_dim), lambda bi, ti: (0, 0)),
                  pl.BlockSpec((1, LANES), lambda bi, ti: (0, 0)),
                  pl.BlockSpec((1, LANES), lambda bi, ti: (0, 0)),
                  pl.BlockSpec((1, HEAD_DIM), lambda bi, ti: (0, 0))] + extra_in_specs,
        out_specs=[pl.BlockSpec((None, tb, v_dim), lambda bi, ti: (bi, ti, 0)),
                   pl.BlockSpec((None, n_v, HEAD_DIM, HEAD_DIM), lambda bi, ti: (bi, 0, 0, 0))] + extra_out_specs,
        out_shape=[jax.ShapeDtypeStruct((b, t, v_dim), BF16),
                   jax.ShapeDtypeStruct((b, n_v, HEAD_DIM, HEAD_DIM), F32)] + extra_out_shape,
        scratch_shapes=[pltpu.VMEM((2 * n_qk + n_v, tb + 8, LANES), F32),
                        pltpu.VMEM((n_qk, nchunk, 2 * CHUNK, HEAD_DIM), BF16),
                        pltpu.VMEM((n_v, tb, HEAD_DIM), F32),
                        pltpu.VMEM((2, tb, LANES), F32),
                        pltpu.VMEM((n_v, tb, LANES), F32),
                        pltpu.VMEM((n_v, tb, LANES), F32),
                        pltpu.VMEM((n_qk, nchunk, CHUNK, 2 * CHUNK), BF16),
                        pltpu.VMEM((n_qk, nchunk, CHUNK, 2 * CHUNK), BF16),
                        pltpu.VMEM((n_v, HEAD_DIM, HEAD_DIM), F32),
                        pltpu.VMEM((n_v, tb, HEAD_DIM), F32)] + extra_scratch,
        compiler_params=_params(2),
        name="gdn_prompt",
    )(qkvz, qkvz, gates, conv_w, pad(a_log), pad(dt_bias), o_norm.reshape(1, HEAD_DIM), *extra_in)


GDN_QK_HEADS = 8
GDN_V_HEADS = 16
GDN_CONV_DIM = (2 * GDN_QK_HEADS + GDN_V_HEADS) * HEAD_DIM
GDN_V_DIM = GDN_V_HEADS * HEAD_DIM


def prep_gdn_weights(w_in, w_out):
    d = w_in.shape[0]
    main = GDN_CONV_DIM + GDN_V_DIM
    w_main = w_in[:, :main].astype(BF16)
    w_gate = jnp.zeros((d, 2 * LANES), F32)
    w_gate = w_gate.at[:, :GDN_V_HEADS].set(w_in[:, main:main + GDN_V_HEADS])
    w_gate = w_gate.at[:, LANES:LANES + GDN_V_HEADS].set(w_in[:, main + GDN_V_HEADS:])
    return w_main, w_gate.astype(BF16), w_out.astype(BF16)


def gdn_layer_prompt(x, gain, weights, conv_w, a_log, dt_bias, o_norm, shift_cache=None):
    w_main, w_gate, _ = weights
    b, t, d = x.shape
    qkvz, gates = gdn_inproj(x.reshape(b * t, d), gain, w_main, w_gate, BF16, 512)
    qkvz = qkvz.reshape(b, t, -1)
    o, state, *shifted = gdn_prompt(qkvz, gates.reshape(b, t, -1), conv_w, a_log, dt_bias, o_norm,
                                    GDN_QK_HEADS, GDN_V_HEADS, 256, shift_cache)
    conv_tail = qkvz[:, t - (CONV_TAPS - 1):, :GDN_CONV_DIM].astype(F32)
    return (o.reshape(b * t, -1), conv_tail, state, *shifted)


GDN_STEP_SEQS = 2


def _gdn_step_kernel(qkvz_ref, gate_ref, conv_ref, s_ref, cw_ref, alog_ref, dtb_ref, onorm_ref,
                     o_ref, convn_ref, sn_ref, *, n_qk, n_v):
    for i in range(qkvz_ref.shape[0]):
        _gdn_step_one(qkvz_ref.at[i], gate_ref.at[i], conv_ref.at[i], s_ref.at[i], cw_ref, alog_ref, dtb_ref,
                      onorm_ref, o_ref.at[i], convn_ref.at[i], sn_ref.at[i], n_qk=n_qk, n_v=n_v)


def _gdn_step_one(qkvz_ref, gate_ref, conv_ref, s_ref, cw_ref, alog_ref, dtb_ref, onorm_ref,
                  o_ref, convn_ref, sn_ref, *, n_qk, n_v):
    rep = n_v // n_qk
    conv_dim = (2 * n_qk + n_v) * HEAD_DIM
    x = qkvz_ref[:, :conv_dim]
    y = cw_ref[CONV_TAPS - 1:CONV_TAPS, :] * x
    for j in range(CONV_TAPS - 1):
        y = y + cw_ref[j:j + 1, :] * conv_ref[j:j + 1, :]
    y = _silu(y)
    convn_ref[0:CONV_TAPS - 2, :] = conv_ref[1:CONV_TAPS - 1, :]
    convn_ref[CONV_TAPS - 2:CONV_TAPS - 1, :] = x

    beta = _sigmoid(gate_ref[:, :LANES])
    g = -jnp.exp(alog_ref[...]) * _softplus(gate_ref[:, LANES:] + dtb_ref[...])
    eye = (lax.broadcasted_iota(jnp.int32, (HEAD_DIM, HEAD_DIM), 0)
           == lax.broadcasted_iota(jnp.int32, (HEAD_DIM, HEAD_DIM), 1))

    def column(row):
        return jnp.sum(jnp.where(eye, row, 0.0), axis=1, keepdims=True)

    def l2(row):
        return row * lax.rsqrt(jnp.sum(row * row, axis=-1, keepdims=True) + EPS)

    for j in range(n_qk):
        q_col = column(l2(y[:, j * HEAD_DIM:(j + 1) * HEAD_DIM]) * (HEAD_DIM ** -0.5))
        k_col = column(l2(y[:, (n_qk + j) * HEAD_DIM:(n_qk + j + 1) * HEAD_DIM]))
        for e in range(rep):
            h = rep * j + e
            lo = h * HEAD_DIM
            v = y[:, 2 * n_qk * HEAD_DIM + lo:2 * n_qk * HEAD_DIM + lo + HEAD_DIM]
            s = s_ref[h]
            e_g = jnp.exp(g[:, h:h + 1])
            k_s = jnp.sum(s * k_col, axis=0, keepdims=True)
            v_new = beta[:, h:h + 1] * (v - e_g * k_s)
            s_new = e_g * s + k_col * v_new
            sn_ref[h] = s_new
            o = _rms_rows(jnp.sum(s_new * q_col, axis=0, keepdims=True), onorm_ref[...])
            z = qkvz_ref[:, conv_dim + lo:conv_dim + lo + HEAD_DIM]
            o_ref[:, lo:lo + HEAD_DIM] = (o * _silu(z)).astype(o_ref.dtype)


def gdn_step(qkvz, gates, conv_state, state, conv_w, a_log, dt_bias, o_norm, n_qk, n_v):
    b = qkvz.shape[0]
    conv_dim = (2 * n_qk + n_v) * HEAD_DIM
    v_dim = n_v * HEAD_DIM
    pad = lambda p: jnp.zeros((1, LANES), F32).at[0, :n_v].set(p.astype(F32))
    nb = GDN_STEP_SEQS if b % GDN_STEP_SEQS == 0 else 1
    row = lambda n: pl.BlockSpec((nb, 1, n), lambda bi: (bi, 0, 0))
    const = lambda shape: pl.BlockSpec(shape, lambda bi: (0,) * len(shape))
    st = pl.BlockSpec((nb, n_v, HEAD_DIM, HEAD_DIM), lambda bi: (bi, 0, 0, 0))
    cv = pl.BlockSpec((nb, CONV_TAPS - 1, conv_dim), lambda bi: (bi, 0, 0))
    return pl.pallas_call(
        functools.partial(_gdn_step_kernel, n_qk=n_qk, n_v=n_v),
        grid=(b // nb,),
        in_specs=[row(conv_dim + v_dim), row(2 * LANES), cv, st,
                  const((CONV_TAPS, conv_dim)), const((1, LANES)), const((1, LANES)), const((1, HEAD_DIM))],
        out_specs=[row(v_dim), cv, st],
        out_shape=[jax.ShapeDtypeStruct((b, 1, v_dim), BF16),
                   jax.ShapeDtypeStruct(conv_state.shape, F32),
                   jax.ShapeDtypeStruct(state.shape, F32)],
        compiler_params=_params(1),
        name="gdn_step",
    )(qkvz.reshape(b, 1, -1), gates.reshape(b, 1, -1), conv_state, state,
      conv_w, pad(a_log), pad(dt_bias), o_norm.reshape(1, HEAD_DIM))


def gdn_layer_sample(x, gain, weights, conv_w, a_log, dt_bias, o_norm, conv_state, state):
    w_main, w_gate, _ = weights
    b = x.shape[0]
    qkvz, gates = gdn_inproj(x, gain, w_main, w_gate, F32, b)
    o, conv_new, state_new = gdn_step(qkvz, gates, conv_state, state, conv_w, a_log, dt_bias, o_norm,
                                      GDN_QK_HEADS, GDN_V_HEADS)
    return o.reshape(b, -1), conv_new, state_new


DSW_HEADS = 8
DSW_WIDTH = DSW_HEADS * HEAD_DIM
DSW_TILE = DSW_BLOCK * max(d for _, d in DSW_GROUPS)
DSW_BLOCKS_PER_ITER = 8


def _dsw_attn_kernel(*refs, n_shift):
    n_groups = len(DSW_GROUPS)
    ins = [refs[5 * g:5 * g + 5] for g in range(n_groups)]
    refs = refs[5 * n_groups:]
    shift_ins, refs = refs[:2 * n_shift], refs[2 * n_shift:]
    o_ref, shift_outs, scratch = refs[0], refs[1:1 + n_shift], refs[1 + n_shift:]
    kf_refs, vf_refs = scratch[:n_groups], scratch[n_groups:2 * n_groups]
    og_ref, lg_ref = scratch[2 * n_groups:2 * n_groups + 2]
    ring_scr = scratch[2 * n_groups + 2:]
    step = (pl.program_id(0) * pl.num_programs(1) + pl.program_id(1)) * pl.num_programs(2) + pl.program_id(2)
    n_steps = pl.num_programs(0) * pl.num_programs(1) * pl.num_programs(2)
    rings = [_ShiftRing(step, n_steps, shift_ins[2 * i], shift_outs[i], *ring_scr[4 * i:4 * i + 4],
                        new_ref=shift_ins[2 * i + 1]) for i in range(n_shift)]
    for ring in rings:
        ring.begin()
    first_tile = pl.program_id(2) == 0
    blk = DSW_BLOCK
    qi = lax.broadcasted_iota(jnp.int32, (blk, 2 * blk), 0)
    ki = lax.broadcasted_iota(jnp.int32, (blk, 2 * blk), 1)
    band = (ki >= qi) & (ki <= qi + blk)
    scale = HEAD_DIM ** -0.5

    for g, (_, dil) in enumerate(DSW_GROUPS):
        for ring in rings:
            ring.point(g)
        q_ref, kc_ref, vc_ref, kp_ref, vp_ref = ins[g]
        kf_ref, vf_ref = kf_refs[g], vf_refs[g]
        per_res = q_ref.shape[1] // blk
        kf_ref[:, :blk, :] = kp_ref[...]
        kf_ref[:, blk:, :] = kc_ref[...]
        vf_ref[:, :blk, :] = vp_ref[...]
        vf_ref[:, blk:, :] = vc_ref[...]

        def blocks_body(it, carry, *, g=g, dil=dil, per_res=per_res, q_ref=q_ref, kf_ref=kf_ref, vf_ref=vf_ref):
            ids = [it * DSW_BLOCKS_PER_ITER + i for i in range(DSW_BLOCKS_PER_ITER)]
            rs = [bi // per_res for bi in ids]
            ms = [bi % per_res for bi in ids]
            row0s = [pl.multiple_of(m * blk, blk) for m in ms]
            ss = [lax.dot_general(q_ref[r, pl.ds(row0, blk), :], kf_ref[r, pl.ds(row0, 2 * blk), :],
                                  (((1,), (1,)), ((), ())), preferred_element_type=F32) * scale
                  for r, row0 in zip(rs, row0s)]
            ss = [jnp.where(band & ((ki >= blk) | (m > 0) | jnp.logical_not(first_tile)), s, NEG_INF)
                  for s, m in zip(ss, ms)]
            mxs = [jnp.max(s, axis=-1, keepdims=True) for s in ss]
            ps = [jnp.exp(s - mx) for s, mx in zip(ss, mxs)]
            dens = [jnp.sum(p, axis=-1, keepdims=True) for p in ps]
            os_ = [jnp.dot(p.astype(BF16), vf_ref[r, pl.ds(row0, 2 * blk), :], preferred_element_type=F32)
                   for p, r, row0 in zip(ps, rs, row0s)]
            for o, den, mx, r, row0 in zip(os_, dens, mxs, rs, row0s):
                o = o * (1.0 / den)
                lse = jnp.broadcast_to(mx + jnp.log(den), (blk, LANES))
                start = row0 * dil + r
                rows = pl.ds(start, blk) if dil == 1 else pl.ds(start, blk, stride=dil)
                og_ref[g, rows, :] = o
                lg_ref[g, rows, :] = lse
            return carry

        lax.fori_loop(0, dil * per_res // DSW_BLOCKS_PER_ITER, blocks_body, 0)

    for ring in rings:
        ring.point(n_groups)

    top = lg_ref[0]
    for g in range(1, n_groups):
        top = jnp.maximum(top, lg_ref[g])
    num = jnp.zeros_like(top)
    den = jnp.zeros_like(top)
    for g in range(n_groups):
        w = jnp.exp(lg_ref[g] - top)
        num = num + w * og_ref[g]
        den = den + w
    o_ref[...] = (num * (1.0 / den)).astype(o_ref.dtype)
    for ring in rings:
        ring.end()


def dsw_attn(projs, shifts=()):
    assert RING_POINTS == len(DSW_GROUPS) + 1
    b = projs[0].shape[0]
    t = projs[0].shape[1] * projs[0].shape[2]
    tile = DSW_TILE
    heads = DSW_HEADS
    in_specs, args, kv_scratch = [], [], []
    for (_, dil), p in zip(DSW_GROUPS, projs):
        rows = tile // dil
        per_res = rows // DSW_BLOCK
        cur = lambda col: pl.BlockSpec((None, dil, rows, HEAD_DIM),
                                       functools.partial(lambda bi, h, n, col: (bi, 0, n, col * heads + h), col=col))
        prev = lambda col: pl.BlockSpec(
            (None, dil, DSW_BLOCK, HEAD_DIM),
            functools.partial(lambda bi, h, n, col, per_res: (bi, 0, jnp.maximum(n * per_res - 1, 0), col * heads + h),
                              col=col, per_res=per_res))
        in_specs += [cur(0), cur(1), cur(2), prev(1), prev(2)]
        args += [p] * 5
        kv_scratch.append(pltpu.VMEM((dil, DSW_BLOCK + rows, HEAD_DIM), BF16))
    n_groups = len(DSW_GROUPS)
    shift_out_specs, shift_out_shape, ring_scratch = [], [], []
    for cache, new in shifts:
        n_l, n_b, rows = cache.shape[:3]
        assert n_l * n_b == b * heads * (t // tile) and rows % RING_POINTS == 0
        in_specs += [pl.BlockSpec(memory_space=pl.ANY), _resident(new.shape)]
        args += [cache, new]
        shift_out_specs.append(pl.BlockSpec(memory_space=pl.ANY))
        shift_out_shape.append(jax.ShapeDtypeStruct(cache.shape, cache.dtype))
        ring_scratch += [pltpu.VMEM((RING_SLOTS, rows // RING_POINTS) + cache.shape[3:], cache.dtype),
                         pltpu.SemaphoreType.DMA((RING_SLOTS,)), pltpu.SemaphoreType.DMA((RING_SLOTS,)),
                         pltpu.SemaphoreType.DMA(())]
    outs = pl.pallas_call(
        functools.partial(_dsw_attn_kernel, n_shift=len(shifts)),
        grid=(b, heads, t // tile),
        in_specs=in_specs,
        out_specs=[pl.BlockSpec((None, tile, HEAD_DIM), lambda bi, h, n: (bi, n, h))] + shift_out_specs,
        out_shape=[jax.ShapeDtypeStruct((b, t, heads * HEAD_DIM), BF16)] + shift_out_shape,
        scratch_shapes=kv_scratch + kv_scratch + [pltpu.VMEM((n_groups, tile, HEAD_DIM), F32),
                                                  pltpu.VMEM((n_groups, tile, LANES), F32)] + ring_scratch,
        compiler_params=_params(3),
        name="dsw_attn",
    )(*args)
    return outs[0], list(outs[1:])


def prep_dsw_weights(w_in, w_out):
    d = w_in.shape[0]
    w = w_in.reshape(d, len(DSW_GROUPS), 3, DSW_WIDTH).transpose(1, 2, 0, 3).astype(BF16)
    return [w[g] for g in range(len(DSW_GROUPS))], w_out.astype(BF16)


def dsw_layer_prompt(x, gain, weights, q_norm, k_norm, shifts=()):
    w_groups, _ = weights
    b, t, d = x.shape
    projs = [dsw_inproj(x, gain, jnp.stack([q_norm[g], k_norm[g]]), w_groups[g], dil,
                        max(DSW_INPROJ_ROWS, DSW_INPROJ_MIN_SUB * dil))
             for g, (_, dil) in enumerate(DSW_GROUPS)]
    o, shifted = dsw_attn(projs, shifts)
    return o.reshape(b * t, -1), projs, shifted


CACHE_RESIDUES = 4


def _dsw_cache_kernel(*refs):
    *p_refs, o_ref, slab_scr = refs
    layer = pl.program_id(0)
    rows, n_res = o_ref.shape[:2]
    for li in range(len(p_refs) // 2):
        @pl.when(layer == li)
        def _(k_ref=p_refs[2 * li], v_ref=p_refs[2 * li + 1]):
            for rr in range(n_res):
                for kv, p_ref in enumerate((k_ref, v_ref)):
                    slab = slab_scr.at[2 * rr + kv]
                    slab[...] = p_ref[rr].astype(o_ref.dtype).reshape(rows, DSW_HEADS, HEAD_DIM)
                    o_ref[:, rr, kv] = slab[...]


def dsw_prompt_caches(projs_layers):
    n_layers = len(projs_layers)
    outs = []
    for g, (window, dil) in enumerate(DSW_GROUPS):
        ps = [pl_[g] for pl_ in projs_layers]
        b, _, sub, _ = ps[0].shape
        last = sub // DSW_BLOCK - 1
        n_res = min(dil, CACHE_RESIDUES)

        def in_map(l, bi, rb, *, li, col):
            before, after = l < li, l > li
            pick = lambda lo, x, hi: jnp.where(before, lo, jnp.where(after, hi, x))
            return (pick(0, bi, b - 1), pick(0, rb, dil // n_res - 1), last, col)

        out = pl.pallas_call(
            _dsw_cache_kernel,
            grid=(n_layers, b, dil // n_res),
            in_specs=[pl.BlockSpec((None, n_res, DSW_BLOCK, DSW_WIDTH), functools.partial(in_map, li=li, col=col))
                      for li in range(n_layers) for col in (1, 2)],
            out_specs=pl.BlockSpec((None, None, DSW_BLOCK, n_res, 2, DSW_HEADS, HEAD_DIM),
                                   lambda l, bi, rb: (l, bi, 0, rb, 0, 0, 0)),
            out_shape=jax.ShapeDtypeStruct((n_layers, b, DSW_BLOCK, dil, 2, DSW_HEADS, HEAD_DIM), F32),
            scratch_shapes=[pltpu.VMEM((2 * n_res, DSW_BLOCK, DSW_HEADS, HEAD_DIM), F32)],
            compiler_params=_params(3),
            name="dsw_prompt_cache",
        )(*[p for p in ps for _ in (1, 2)])
        outs.append(out.reshape(n_layers, b, window, 2, DSW_HEADS, HEAD_DIM))
    return outs


def _dsw_decode_kernel(p_ref, *refs):
    n_groups = len(DSW_GROUPS)
    c_refs = refs[:n_groups]
    o_ref, new_ref = refs[n_groups:]
    scale = HEAD_DIM ** -0.5
    outs, lses = [], []
    for g in range(n_groups):
        q = p_ref[g, 0].astype(F32)
        k_new = p_ref[g, 1].astype(F32)
        v_new = p_ref[g, 2].astype(F32)
        new_ref[g, 0] = k_new
        new_ref[g, 1] = v_new
        s = jnp.sum(c_refs[g][:, 0] * q[None], axis=-1, keepdims=True) * scale
        s_new = jnp.sum(k_new * q, axis=-1, keepdims=True) * scale
        mx = jnp.maximum(jnp.max(s, axis=0), s_new)
        p = jnp.exp(s - mx[None])
        p_new = jnp.exp(s_new - mx)
        den = jnp.sum(p, axis=0) + p_new
        outs.append((jnp.sum(p * c_refs[g][:, 1], axis=0) + p_new * v_new) * (1.0 / den))
        lses.append(mx + jnp.log(den))
    top = functools.reduce(jnp.maximum, lses)
    ws = [jnp.exp(l - top) for l in lses]
    num = sum(w * o for w, o in zip(ws, outs))
    o_ref[...] = (num * (1.0 / sum(ws))).astype(o_ref.dtype)


def dsw_decode(proj, caches, layer):
    b = proj.shape[0]
    n_groups = len(DSW_GROUPS)
    c_specs, c_args = [], []
    for (window, dil), c in zip(DSW_GROUPS, caches):
        c_args.append(c.reshape(c.shape[0], b, window // dil, dil, 2, DSW_HEADS, HEAD_DIM))
        c_specs.append(pl.BlockSpec((None, None, window // dil, None, 2, DSW_HEADS, HEAD_DIM),
                                    lambda bi: (layer, bi, 0, 0, 0, 0, 0)))
    return pl.pallas_call(
        _dsw_decode_kernel,
        grid=(b,),
        in_specs=[pl.BlockSpec((None, n_groups, 3, DSW_HEADS, HEAD_DIM), lambda bi: (bi, 0, 0, 0, 0))] + c_specs,
        out_specs=[pl.BlockSpec((None, DSW_HEADS, HEAD_DIM), lambda bi: (bi, 0, 0)),
                   pl.BlockSpec((None, n_groups, 2, DSW_HEADS, HEAD_DIM), lambda bi: (bi, 0, 0, 0, 0))],
        out_shape=[jax.ShapeDtypeStruct((b, DSW_HEADS, HEAD_DIM), BF16),
                   jax.ShapeDtypeStruct((b, n_groups, 2, DSW_HEADS, HEAD_DIM), F32)],
        compiler_params=_params(1),
        name="dsw_decode",
    )(proj, *c_args)


def dsw_layer_sample(x, gain, weights, q_norm, k_norm, caches, layer):
    w_groups, _ = weights
    b, d = x.shape
    proj = jnp.stack([dsw_inproj(x.reshape(1, b, d), gain, jnp.stack([q_norm[g], k_norm[g]]), w_groups[g], 1, b)
                      .reshape(b, 3, DSW_HEADS, HEAD_DIM) for g in range(len(DSW_GROUPS))], axis=1)
    o, new_rows = dsw_decode(proj, caches, layer)
    return o.reshape(b, -1), new_rows


def _last_row_kernel(shifted_ref, new_ref, o_ref):
    del shifted_ref
    o_ref[0] = new_ref[...]


def write_last_rows(shifted, new_rows_layers, g):
    new = jnp.stack(new_rows_layers)
    n_layers, b, window = shifted.shape[:3]
    tile = shifted.shape[3:]
    return pl.pallas_call(
        _last_row_kernel,
        grid=(n_layers, b),
        in_specs=[pl.BlockSpec(memory_space=pl.ANY),
                  pl.BlockSpec((None, None, None) + tile, lambda l, bi: (l, bi, g, 0, 0, 0))],
        out_specs=pl.BlockSpec((None, None, 1) + tile, lambda l, bi: (l, bi, window - 1, 0, 0, 0)),
        out_shape=jax.ShapeDtypeStruct(shifted.shape, shifted.dtype),
        input_output_aliases={0: 0},
        compiler_params=_params(2),
        name="cache_last_row",
    )(shifted, new)


def kernel(x_prompt, x_sample, state_gdn, state_conv, cache_kv_w128, cache_kv_w512, cache_kv_w2048,
           norm_mix, norm_mlp, gdn_w_in, gdn_conv_w, gdn_a_log, gdn_dt_bias, gdn_o_norm, gdn_w_out,
           dsw_w_in, dsw_q_norm, dsw_k_norm, dsw_w_out, mlp_w_up, mlp_w_down):
    b, t, d = x_prompt.shape
    bs = x_sample.shape[0]
    depth = norm_mix.shape[0]
    caches = [cache_kv_w128, cache_kv_w512, cache_kv_w2048]
    yp = x_prompt.reshape(b * t, d)
    ys = x_sample.reshape(bs, d)
    p_gdn, p_conv, s_gdn, s_conv, p_projs, s_rows = [], [], [], [], [], []
    for i in range(depth):
        j = i // 2
        if i % 2 == 0:
            wts = prep_gdn_weights(gdn_w_in[j], gdn_w_out[j])
            par = (gdn_conv_w[j], gdn_a_log[j], gdn_dt_bias[j], gdn_o_norm[j])
            host = i == 2 * ((depth - 1) // 2)
            op, conv_p, state_p, *shifted = gdn_layer_prompt(yp.reshape(b, t, d), norm_mix[i], wts, *par,
                                                             caches[-1] if host else None)
            if host:
                shifted_big = shifted[0]
            os_, conv_s, state_s = gdn_layer_sample(ys, norm_mix[i], wts, *par, state_conv[j], state_gdn[j])
            p_gdn.append(state_p)
            p_conv.append(conv_p)
            s_gdn.append(state_s)
            s_conv.append(conv_s)
        else:
            wts = prep_dsw_weights(dsw_w_in[j], dsw_w_out[j])
            os_, rows = dsw_layer_sample(ys, norm_mix[i], wts, dsw_q_norm[j], dsw_k_norm[j], caches, j)
            s_rows.append(rows)
            shifts = ()
            if len(s_rows) == caches[0].shape[0]:
                new = jnp.stack(s_rows)
                shifts = tuple((caches[g], new[:, :, g]) for g in range(len(caches) - 1))
            op, projs, shifted = dsw_layer_prompt(yp.reshape(b, t, d), norm_mix[i], wts, dsw_q_norm[j], dsw_k_norm[j],
                                                  shifts)
            if shifts:
                shifted_small = shifted
            p_projs.append(projs)
        w_up, w_down = mlp_w_up[i].astype(BF16), mlp_w_down[i].astype(BF16)
        yp = mixer_mlp(yp, op, wts[-1], norm_mlp[i], w_up, w_down, MLP_ROWS, MLP_COLS)
        ys = mixer_mlp(ys, os_, wts[-1], norm_mlp[i], w_up, w_down, bs, MLP_COLS)
    yp = yp.reshape(b, t, d)
    p_kv = dsw_prompt_caches(p_projs)
    s_kv = shifted_small + [write_last_rows(shifted_big, s_rows, len(caches) - 1)]
    return (yp, ys.reshape(x_sample.shape),
            jnp.stack(p_gdn), jnp.stack(p_conv), p_kv[0], p_kv[1], p_kv[2],
            jnp.stack(s_gdn), jnp.stack(s_conv), s_kv[0], s_kv[1], s_kv[2])
```

```python
import functools

import jax
import jax.numpy as jnp
from jax import lax
from jax.experimental import pallas as pl
from jax.experimental.pallas import tpu as pltpu

F32 = jnp.float32
BF16 = jnp.bfloat16

EPS = 1e-6
NEG_INF = -1e30
LANES = 128
CONV_TAPS = 4
CHUNK = 64
HEAD_DIM = 128
DSW_BLOCK = 128
DSW_GROUPS = ((128, 1), (512, 4), (2048, 16))
VMEM_LIMIT = 56 * 1024 * 1024


def _params(n_axes, vmem=VMEM_LIMIT):
    return pltpu.CompilerParams(dimension_semantics=("arbitrary",) * n_axes,
                                vmem_limit_bytes=vmem)


def _sigmoid(x):
    return 0.5 + 0.5 * jnp.tanh(0.5 * x)


def _silu(x):
    h = 0.5 * x
    return h + h * jnp.tanh(h)


def _softplus(x):
    return jnp.maximum(x, 0.0) + jnp.log(1.0 + jnp.exp(-jnp.abs(x)))


def _rms_rows(x, gain_row):
    ms = jnp.mean(x * x, axis=-1, keepdims=True)
    return x * lax.rsqrt(ms + EPS) * gain_row


INPROJ_COLS = 512
GDN_INPROJ_ROWS = 512
GDN_BLOCK = 256
MLP_ROWS = 512
MLP_COLS = 1024


def _gdn_inproj_kernel(x_ref, g_ref, w_ref, wg_ref, o_ref, og_ref, xn_ref):
    xn_ref[...] = _rms_rows(x_ref[...], g_ref[...]).astype(BF16)
    for lo in range(0, w_ref.shape[1], INPROJ_COLS):
        o_ref[:, lo:lo + INPROJ_COLS] = jnp.dot(
            xn_ref[...], w_ref[:, lo:lo + INPROJ_COLS], preferred_element_type=F32).astype(o_ref.dtype)
    og_ref[...] = jnp.dot(xn_ref[...], wg_ref[...], preferred_element_type=F32)


def gdn_inproj(x, gain, w_main, w_gate, out_dtype, tm):
    m, k = x.shape
    n, ng = w_main.shape[1], w_gate.shape[1]
    tm = min(tm, m)
    return pl.pallas_call(
        _gdn_inproj_kernel,
        grid=(m // tm,),
        in_specs=[pl.BlockSpec((tm, k), lambda i: (i, 0)),
                  pl.BlockSpec((1, k), lambda i: (0, 0)),
                  _resident((k, n)), _resident((k, ng))],
        out_specs=[pl.BlockSpec((tm, n), lambda i: (i, 0)),
                   pl.BlockSpec((tm, ng), lambda i: (i, 0))],
        out_shape=[jax.ShapeDtypeStruct((m, n), out_dtype), jax.ShapeDtypeStruct((m, ng), F32)],
        scratch_shapes=[pltpu.VMEM((tm, k), BF16)],
        compiler_params=_params(1),
        name="gdn_inproj",
    )(x, gain.reshape(1, k), w_main, w_gate)


def _mixer_mlp_kernel(x_ref, a_ref, wo_ref, g_ref, wu_ref, wd_ref, o_ref, xn_ref, *, tf):
    x1 = x_ref[...] + jnp.dot(a_ref[...], wo_ref[...], preferred_element_type=F32)
    xn_ref[...] = _rms_rows(x1, g_ref[...]).astype(BF16)
    o_ref[...] = x1
    for lo in range(0, wu_ref.shape[1], tf):
        h = jnp.dot(xn_ref[...], wu_ref[:, lo:lo + tf], preferred_element_type=F32)
        h = jnp.square(jnp.maximum(h, 0.0)).astype(BF16)
        o_ref[...] += jnp.dot(h, wd_ref[lo:lo + tf, :], preferred_element_type=F32)


def _resident(shape):
    return pl.BlockSpec(shape, lambda *_: (0,) * len(shape), pipeline_mode=pl.Buffered(1))


def mixer_mlp(x, a, w_out, gain, w_up, w_down, tm, tf):
    m, d = x.shape
    ka = a.shape[1]
    ff = w_up.shape[1]
    tm = min(tm, m)
    return pl.pallas_call(
        functools.partial(_mixer_mlp_kernel, tf=tf),
        grid=(m // tm,),
        in_specs=[pl.BlockSpec((tm, d), lambda i: (i, 0)),
                  pl.BlockSpec((tm, ka), lambda i: (i, 0)),
                  _resident((ka, d)), _resident((1, d)), _resident((d, ff)), _resident((ff, d))],
        out_specs=pl.BlockSpec((tm, d), lambda i: (i, 0)),
        out_shape=jax.ShapeDtypeStruct((m, d), F32),
        scratch_shapes=[pltpu.VMEM((tm, d), BF16)],
        compiler_params=_params(1),
        name="mixer_mlp",
    )(x, a, w_out, gain.reshape(1, d), w_up, w_down)


DEINTERLEAVE_STRIDE = 4
DSW_INPROJ_ROWS = 1024
DSW_INPROJ_MIN_SUB = 64


def _dsw_inproj_kernel(x_ref, g_ref, hn_ref, w_ref, o_ref, xn_ref, head_scr, part_scr, *, dil):
    rows = x_ref.shape[0]
    sub = rows // dil
    width = w_ref.shape[2]
    xn_ref[...] = _rms_rows(x_ref[...], g_ref[...]).astype(BF16)
    for t in range(3):
        acc = jnp.dot(xn_ref[...], w_ref[t], preferred_element_type=F32)
        for h in range(width // HEAD_DIM):
            a = acc[:, h * HEAD_DIM:(h + 1) * HEAD_DIM]
            if t < 2:
                a = _rms_rows(a, hn_ref[t:t + 1, :])
            cols = slice(t * width + h * HEAD_DIM, t * width + (h + 1) * HEAD_DIM)
            if dil == 1:
                o_ref[0, :, cols] = a.astype(o_ref.dtype)
            elif dil <= DEINTERLEAVE_STRIDE:
                head_scr[h] = a
                for r in range(dil):
                    o_ref[r, :, cols] = head_scr[h, pl.ds(r, sub, stride=dil), :].astype(o_ref.dtype)
            else:
                st = DEINTERLEAVE_STRIDE
                part = rows // st
                head_scr[h] = a
                for q in range(st):
                    part_scr[h, q * part:(q + 1) * part, :] = head_scr[h, pl.ds(q, part, stride=st), :]
                for q in range(st):
                    for s in range(dil // st):
                        o_ref[st * s + q, :, cols] = part_scr[
                            h, pl.ds(q * part + s, sub, stride=dil // st), :].astype(o_ref.dtype)


def dsw_inproj(x, gain, head_gains, w3, dil, tile_rows):
    b, t, d = x.shape
    width = w3.shape[2]
    rows = min(tile_rows, t)
    return pl.pallas_call(
        functools.partial(_dsw_inproj_kernel, dil=dil),
        grid=(b, t // rows),
        in_specs=[pl.BlockSpec((None, rows, d), lambda bi, i: (bi, i, 0)),
                  pl.BlockSpec((1, d), lambda bi, i: (0, 0)),
                  pl.BlockSpec((2, HEAD_DIM), lambda bi, i: (0, 0)),
                  _resident((3, d, width))],
        out_specs=pl.BlockSpec((None, dil, rows // dil, 3 * width), lambda bi, i: (bi, 0, i, 0)),
        out_shape=jax.ShapeDtypeStruct((b, dil, t // dil, 3 * width), BF16),
        scratch_shapes=[pltpu.VMEM((rows, d), BF16),
                        pltpu.VMEM((width // HEAD_DIM, rows, HEAD_DIM), F32),
                        pltpu.VMEM((width // HEAD_DIM, rows, HEAD_DIM), F32)],
        compiler_params=_params(2),
        name="dsw_inproj",
    )(x, gain.reshape(1, d), head_gains, w3)


TRI_BASE = 8


def _pair_block_diag(m):
    n = m.shape[0]
    mb = m.astype(BF16)
    lane = lax.broadcasted_iota(jnp.int32, mb.shape, 1)
    zero = jnp.zeros_like(mb)
    return jnp.concatenate([jnp.where(lane < n, mb, zero), jnp.where(lane >= n, mb, zero)], axis=0)


def _pair_mm(u, w):
    return jnp.dot(u.astype(BF16), _pair_block_diag(w), preferred_element_type=F32)


def _tri_inverse_pairs(a_list, between=lambda: None):
    n = a_list[0].shape[0]
    ii = lax.broadcasted_iota(jnp.int32, (n, 2 * n), 0)
    jj = lax.broadcasted_iota(jnp.int32, (n, 2 * n), 1) % n
    same_block = lambda size: (ii // size) == (jj // size)
    ps = [jnp.where(same_block(TRI_BASE), a, 0.0) for a in a_list]
    xs = [jnp.where(ii == jj, 1.0, 0.0) - p for p in ps]
    for _ in range((TRI_BASE - 1).bit_length() - 1):
        ps = [_pair_mm(p, p) for p in ps]
        between()
        xs = [x + _pair_mm(x, p) for x, p in zip(xs, ps)]
        between()
    size = TRI_BASE
    while size < n:
        off = same_block(2 * size) & jnp.logical_not(same_block(size))
        ys = [_pair_mm(x, jnp.where(off, a, 0.0)) for x, a in zip(xs, a_list)]
        between()
        xs = [x - _pair_mm(y, x) for x, y in zip(xs, ys)]
        between()
        size *= 2
    return xs


RING_POINTS = 4
RING_SLOTS = 4


class _ShiftRing:
    def __init__(self, step, n_steps, cache_ref, out_ref, stage_ref, sem_in, sem_out, sem_row,
                 zero_ref=None, new_ref=None):
        self.step, self.n_steps = step, n_steps
        self.cache, self.out, self.stage, self.zero, self.new = cache_ref, out_ref, stage_ref, zero_ref, new_ref
        self.sem_in, self.sem_out, self.sem_row = sem_in, sem_out, sem_row
        self.batch, self.rows = cache_ref.shape[1], cache_ref.shape[2]
        self.chunk = self.rows // RING_POINTS

    def _n_rows(self, q):
        return self.chunk if q < RING_POINTS - 1 else self.chunk - 1

    def _pair(self, step):
        return step // self.batch, step % self.batch

    def _in(self, step, q, slot):
        n = self._n_rows(q)
        l, b = self._pair(step)
        return pltpu.make_async_copy(self.cache.at[l, b, pl.ds(1 + self.chunk * q, n)],
                                     self.stage.at[slot, pl.ds(0, n)], self.sem_in.at[slot])

    def _out(self, step, q, slot):
        n = self._n_rows(q)
        l, b = self._pair(step)
        return pltpu.make_async_copy(self.stage.at[slot, pl.ds(0, n)],
                                     self.out.at[l, b, pl.ds(self.chunk * q, n)], self.sem_out.at[slot])

    def _row(self):
        l, b = self._pair(self.step)
        src = self.zero if self.new is None else self.new.at[l, b]
        return pltpu.make_async_copy(src, self.out.at[l, b, self.rows - 1], self.sem_row)

    def begin(self):
        step = self.step
        if self.new is None:
            self.zero[...] = jnp.zeros_like(self.zero)

        @pl.when(step == 0)
        def _():
            for n in range(RING_SLOTS - 1):
                self._in(step + n // RING_POINTS, n % RING_POINTS, n).start()

        self._row().start(priority=1)

    def point(self, q):
        step = self.step
        ahead = RING_SLOTS - 1
        n = step * RING_POINTS + q
        slot = n % RING_SLOTS
        self._in(step, q, slot).wait()
        self._out(step, q, slot).start(priority=1)
        prev_step, prev_q = (step, q - 1) if q > 0 else (step - 1, RING_POINTS - 1)
        free_slot = (n + ahead) % RING_SLOTS

        @pl.when(n > 0)
        def _():
            self._out(prev_step, prev_q, free_slot).wait()

        next_step, next_q = step + (q + ahead) // RING_POINTS, (q + ahead) % RING_POINTS

        @pl.when(n + ahead < self.n_steps * RING_POINTS)
        def _():
            self._in(next_step, next_q, free_slot).start()

    def end(self):
        step = self.step
        self._row().wait()

        @pl.when(step == self.n_steps - 1)
        def _():
            last = self.n_steps * RING_POINTS - 1
            self._out(step, RING_POINTS - 1, last % RING_SLOTS).wait()


def _gdn_prompt_kernel(*refs, n_qk, n_v, with_shift):
    if with_shift:
        (qkv_ref, z_ref, gate_ref, cw_ref, alog_ref, dtb_ref, onorm_ref, cache_ref,
         o_ref, s_out_ref, shift_ref,
         xe_ref, qk_ref, v_ref, gate_scr, bb_ref, gcb_ref, tp_ref, pp_ref, s_ref, oscr_ref,
         stage_ref, zero_ref, sem_in, sem_out, sem_row) = refs
    else:
        (qkv_ref, z_ref, gate_ref, cw_ref, alog_ref, dtb_ref, onorm_ref,
         o_ref, s_out_ref,
         xe_ref, qk_ref, v_ref, gate_scr, bb_ref, gcb_ref, tp_ref, pp_ref, s_ref, oscr_ref) = refs
    tb = qkv_ref.shape[0]
    nchunk = tb // CHUNK
    t_idx = pl.program_id(1)
    halo = 8
    ring_point = lambda q: None
    if with_shift:
        ring = _ShiftRing(pl.program_id(0) * pl.num_programs(1) + t_idx, pl.num_programs(0) * pl.num_programs(1),
                          cache_ref, shift_ref, stage_ref, sem_in, sem_out, sem_row, zero_ref=zero_ref)
        ring.begin()
        ring_point = ring.point

    @pl.when(t_idx == 0)
    def _():
        xe_ref[:, halo - (CONV_TAPS - 1):halo, :] = jnp.zeros((xe_ref.shape[0], CONV_TAPS - 1, LANES), F32)
        s_ref[...] = jnp.zeros_like(s_ref)

    ring_point(0)

    gate_scr[0] = _sigmoid(gate_ref[:, :LANES])
    g = -jnp.exp(alog_ref[...]) * _softplus(gate_ref[:, LANES:] + dtb_ref[...])
    pos = lax.broadcasted_iota(jnp.int32, (tb, LANES), 0) % CHUNK
    shift = 1
    while shift < CHUNK:
        g = g + jnp.where(pos >= shift, pltpu.roll(g, shift, axis=0), 0.0)
        shift *= 2
    gate_scr[1] = g
    for h in range(n_v):
        bb_ref[h] = jnp.broadcast_to(gate_scr[0, :, h:h + 1], (tb, LANES))
        gcb_ref[h] = jnp.broadcast_to(gate_scr[1, :, h:h + 1], (tb, LANES))

    def conv_block(cb):
        lo = cb * LANES
        xe_ref[cb, halo:halo + tb, :] = qkv_ref[:, lo:lo + LANES].astype(F32)
        y = cw_ref[0:1, lo:lo + LANES] * xe_ref[cb, halo - 3:halo - 3 + tb, :]
        for j in range(1, CONV_TAPS):
            y = y + cw_ref[j:j + 1, lo:lo + LANES] * xe_ref[cb, halo - 3 + j:halo - 3 + j + tb, :]
        y = _silu(y)
        xe_ref[cb, halo - 3:halo, :] = xe_ref[cb, halo + tb - 3:halo + tb, :]
        if cb < 2 * n_qk:
            y = y * lax.rsqrt(jnp.sum(y * y, axis=-1, keepdims=True) + EPS)
            if cb < n_qk:
                y = y * (HEAD_DIM ** -0.5)
            yb = y.astype(BF16)
            half = 0 if cb < n_qk else CHUNK
            for c in range(nchunk):
                qk_ref[cb % n_qk, c, half:half + CHUNK, :] = yb[c * CHUNK:(c + 1) * CHUNK, :]
        else:
            v_ref[cb - 2 * n_qk] = y

    for cb in range(2 * n_qk):
        conv_block(cb)
    ring_point(1)
    pending = [functools.partial(conv_block, cb) for cb in range(2 * n_qk, 2 * n_qk + n_v)]

    def filler():
        if pending:
            pending.pop(0)()

    ii = lax.broadcasted_iota(jnp.int32, (CHUNK, 2 * CHUNK), 0)
    lane = lax.broadcasted_iota(jnp.int32, (CHUNK, 2 * CHUNK), 1)
    jj = lane % CHUNK
    causal = ii >= jj
    strict = ii > jj
    pairs_per_iter = 4

    for it in range(n_qk // pairs_per_iter):
        probs = [(it * pairs_per_iter + dj, c) for dj in range(pairs_per_iter) for c in range(nchunk)]
        grams, a_list = [], []
        for j, c in probs:
            qk = qk_ref[j, c]
            k2 = jnp.concatenate([qk[CHUNK:], qk[CHUNK:]], axis=0)
            grams.append(lax.dot_general(qk, k2, (((1,), (1,)), ((), ())), preferred_element_type=F32))
        filler()
        for (j, c), gram in zip(probs, grams):
            rows = slice(c * CHUNK, (c + 1) * CHUNK)
            gc = jnp.where(lane < CHUNK, gcb_ref[2 * j, rows, :], gcb_ref[2 * j + 1, rows, :])
            bt = jnp.where(lane < CHUNK, bb_ref[2 * j, rows, :], bb_ref[2 * j + 1, rows, :])
            gc_row = jnp.sum(jnp.where(ii == jj, gc, 0.0), axis=0, keepdims=True)
            decay = jnp.where(causal, jnp.exp(jnp.where(causal, gc - gc_row, 0.0)), 0.0)
            a_list.append(jnp.where(strict, bt * gram[CHUNK:] * decay, 0.0))
            pp_ref[j, c] = (gram[:CHUNK] * decay).astype(BF16)
        for (j, c), t_inv in zip(probs, _tri_inverse_pairs(a_list, filler)):
            tp_ref[j, c] = t_inv.astype(BF16)
    while pending:
        filler()
    ring_point(2)

    def block_diag2(m):
        mb = m.astype(BF16)
        zero = jnp.zeros((CHUNK, HEAD_DIM), BF16)
        return jnp.concatenate([jnp.concatenate([mb[:, :HEAD_DIM], zero], axis=1),
                                jnp.concatenate([zero, mb[:, HEAD_DIM:]], axis=1)], axis=0)

    def finish_chunk(c):
        rows = slice(c * CHUNK, (c + 1) * CHUNK)
        for h in range(n_v):
            lo = h * HEAD_DIM
            o = _rms_rows(oscr_ref[h, rows, :], onorm_ref[...])
            o_ref[rows, lo:lo + HEAD_DIM] = (o * _silu(z_ref[rows, lo:lo + HEAD_DIM].astype(F32))).astype(o_ref.dtype)

    for c in range(nchunk):
        if c > 0:
            finish_chunk(c - 1)
        rows = slice(c * CHUNK, (c + 1) * CHUNK)
        heads = range(n_qk)
        pair = lambda ref, j: jnp.concatenate([ref[2 * j, rows, :], ref[2 * j + 1, rows, :]], axis=1)
        qks = [qk_ref[j, c] for j in heads]
        projs = [jnp.dot(qks[j], jnp.concatenate([s_ref[2 * j], s_ref[2 * j + 1]], axis=1).astype(BF16),
                         preferred_element_type=F32) for j in heads]
        gcs = [pair(gcb_ref, j) for j in heads]
        e_gcs = [jnp.exp(gc) for gc in gcs]
        rhss = [pair(bb_ref, j) * (pair(v_ref, j) - e_gcs[j] * projs[j][CHUNK:]) for j in heads]
        v_news = [jnp.dot(tp_ref[j, c], block_diag2(rhss[j]), preferred_element_type=F32) for j in heads]
        outs = [e_gcs[j] * projs[j][:CHUNK]
                + jnp.dot(pp_ref[j, c], block_diag2(v_news[j]), preferred_element_type=F32) for j in heads]
        g_lasts = [gc[CHUNK - 1:CHUNK, :] for gc in gcs]
        v_decs = [(jnp.exp(g_lasts[j] - gcs[j]) * v_news[j]).astype(BF16) for j in heads]
        d_states = [lax.dot_general(qks[j][CHUNK:], v_decs[j], (((0,), (0,)), ((), ())),
                                    preferred_element_type=F32) for j in heads]
        for j in heads:
            for e in range(2):
                lanes = slice(e * HEAD_DIM, (e + 1) * HEAD_DIM)
                oscr_ref[2 * j + e, rows, :] = outs[j][:, lanes]
                s_ref[2 * j + e] = jnp.exp(g_lasts[j][:, lanes]) * s_ref[2 * j + e] + d_states[j][:, lanes]
    ring_point(3)
    finish_chunk(nchunk - 1)

    @pl.when(t_idx == pl.num_programs(1) - 1)
    def _():
        s_out_ref[...] = s_ref[...]

    if with_shift:
        ring.end()


def gdn_prompt(qkvz, gates, conv_w, a_log, dt_bias, o_norm, n_qk, n_v, tb, shift_cache=None):
    b, t, _ = qkvz.shape
    assert n_v == 2 * n_qk and 2 * CHUNK == LANES
    conv_dim = (2 * n_qk + n_v) * HEAD_DIM
    v_dim = n_v * HEAD_DIM
    tb = min(tb, t)
    nchunk = tb // CHUNK
    pad = lambda p: jnp.zeros((1, LANES), F32).at[0, :n_v].set(p.astype(F32))
    extra_in, extra_in_specs, extra_out_specs, extra_out_shape, extra_scratch = [], [], [], [], []
    if shift_cache is not None:
        n_l, n_b, rows = shift_cache.shape[:3]
        assert n_l * n_b == b * (t // tb) and rows % RING_POINTS == 0
        extra_in = [shift_cache]
        extra_in_specs = [pl.BlockSpec(memory_space=pl.ANY)]
        extra_out_specs = [pl.BlockSpec(memory_space=pl.ANY)]
        extra_out_shape = [jax.ShapeDtypeStruct(shift_cache.shape, shift_cache.dtype)]
        extra_scratch = [pltpu.VMEM((RING_SLOTS, rows // RING_POINTS) + shift_cache.shape[3:], shift_cache.dtype),
                         pltpu.VMEM(shift_cache.shape[3:], shift_cache.dtype),
                         pltpu.SemaphoreType.DMA((RING_SLOTS,)), pltpu.SemaphoreType.DMA((RING_SLOTS,)),
                         pltpu.SemaphoreType.DMA(())]
    return pl.pallas_call(
        functools.partial(_gdn_prompt_kernel, n_qk=n_qk, n_v=n_v, with_shift=shift_cache is not None),
        grid=(b, t // tb),
        in_specs=[pl.BlockSpec((None, tb, conv_dim), lambda bi, ti: (bi, ti, 0)),
                  pl.BlockSpec((None, tb, v_dim), lambda bi, ti: (bi, ti, conv_dim // v_dim)),
                  pl.BlockSpec((None, tb, 2 * LANES), lambda bi, ti: (bi, ti, 0)),
                  pl.BlockSpec((CONV_TAPS, conv_dim), lambda bi, ti: (0, 0)),
                  pl.BlockSpec((1, LANES), lambda bi, ti: (0, 0)),
                  pl.BlockSpec((1, LANES), lambda bi, ti: (0, 0)),
                  pl.BlockSpec((1, HEAD_DIM), lambda bi, ti: (0, 0))] + extra_in_specs,
        out_specs=[pl.BlockSpec((None, tb, v_dim), lambda bi, ti: (bi, ti, 0)),
                   pl.BlockSpec((None, n_v, HEAD_DIM, HEAD_DIM), lambda bi, ti: (bi, 0, 0, 0))] + extra_out_specs,
        out_shape=[jax.ShapeDtypeStruct((b, t, v_dim), BF16),
                   jax.ShapeDtypeStruct((b, n_v, HEAD_DIM, HEAD_DIM), F32)] + extra_out_shape,
        scratch_shapes=[pltpu.VMEM((2 * n_qk + n_v, tb + 8, LANES), F32),
                        pltpu.VMEM((n_qk, nchunk, 2 * CHUNK, HEAD_DIM), BF16),
                        pltpu.VMEM((n_v, tb, HEAD_DIM), F32),
                        pltpu.VMEM((2, tb, LANES), F32),
                        pltpu.VMEM((n_v, tb, LANES), F32),
                        pltpu.VMEM((n_v, tb, LANES), F32),
                        pltpu.VMEM((n_qk, nchunk, CHUNK, 2 * CHUNK), BF16),
                        pltpu.VMEM((n_qk, nchunk, CHUNK, 2 * CHUNK), BF16),
                        pltpu.VMEM((n_v, HEAD_DIM, HEAD_DIM), F32),
                        pltpu.VMEM((n_v, tb, HEAD_DIM), F32)] + extra_scratch,
        compiler_params=_params(2),
        name="gdn_prompt",
    )(qkvz, qkvz, gates, conv_w, pad(a_log), pad(dt_bias), o_norm.reshape(1, HEAD_DIM), *extra_in)


GDN_QK_HEADS = 8
GDN_V_HEADS = 16
GDN_CONV_DIM = (2 * GDN_QK_HEADS + GDN_V_HEADS) * HEAD_DIM
GDN_V_DIM = GDN_V_HEADS * HEAD_DIM


def prep_gdn_weights(w_in, w_out):
    d = w_in.shape[0]
    main = GDN_CONV_DIM + GDN_V_DIM
    w_main = w_in[:, :main].astype(BF16)
    w_gate = jnp.zeros((d, 2 * LANES), F32)
    w_gate = w_gate.at[:, :GDN_V_HEADS].set(w_in[:, main:main + GDN_V_HEADS])
    w_gate = w_gate.at[:, LANES:LANES + GDN_V_HEADS].set(w_in[:, main + GDN_V_HEADS:])
    return w_main, w_gate.astype(BF16), w_out.astype(BF16)


def gdn_layer_prompt(x, gain, weights, conv_w, a_log, dt_bias, o_norm, shift_cache=None):
    w_main, w_gate, _ = weights
    b, t, d = x.shape
    qkvz, gates = gdn_inproj(x.reshape(b * t, d), gain, w_main, w_gate, BF16, GDN_INPROJ_ROWS)
    qkvz = qkvz.reshape(b, t, -1)
    o, state, *shifted = gdn_prompt(qkvz, gates.reshape(b, t, -1), conv_w, a_log, dt_bias, o_norm,
                                    GDN_QK_HEADS, GDN_V_HEADS, GDN_BLOCK, shift_cache)
    conv_tail = qkvz[:, t - (CONV_TAPS - 1):, :GDN_CONV_DIM].astype(F32)
    return (o.reshape(b * t, -1), conv_tail, state, *shifted)


GDN_STEP_SEQS = 1


def _gdn_step_kernel(qkvz_ref, gate_ref, conv_ref, s_ref, cw_ref, alog_ref, dtb_ref, onorm_ref,
                     o_ref, convn_ref, sn_ref, *, n_qk, n_v):
    for i in range(qkvz_ref.shape[0]):
        _gdn_step_one(qkvz_ref.at[i], gate_ref.at[i], conv_ref.at[i], s_ref.at[i], cw_ref, alog_ref, dtb_ref,
                      onorm_ref, o_ref.at[i], convn_ref.at[i], sn_ref.at[i], n_qk=n_qk, n_v=n_v)


def _gdn_step_one(qkvz_ref, gate_ref, conv_ref, s_ref, cw_ref, alog_ref, dtb_ref, onorm_ref,
                  o_ref, convn_ref, sn_ref, *, n_qk, n_v):
    rep = n_v // n_qk
    conv_dim = (2 * n_qk + n_v) * HEAD_DIM
    x = qkvz_ref[:, :conv_dim]
    y = cw_ref[CONV_TAPS - 1:CONV_TAPS, :] * x
    for j in range(CONV_TAPS - 1):
        y = y + cw_ref[j:j + 1, :] * conv_ref[j:j + 1, :]
    y = _silu(y)
    convn_ref[0:CONV_TAPS - 2, :] = conv_ref[1:CONV_TAPS - 1, :]
    convn_ref[CONV_TAPS - 2:CONV_TAPS - 1, :] = x

    beta = _sigmoid(gate_ref[:, :LANES])
    g = -jnp.exp(alog_ref[...]) * _softplus(gate_ref[:, LANES:] + dtb_ref[...])
    eye = (lax.broadcasted_iota(jnp.int32, (HEAD_DIM, HEAD_DIM), 0)
           == lax.broadcasted_iota(jnp.int32, (HEAD_DIM, HEAD_DIM), 1))

    def column(row):
        return jnp.sum(jnp.where(eye, row, 0.0), axis=1, keepdims=True)

    def l2(row):
        return row * lax.rsqrt(jnp.sum(row * row, axis=-1, keepdims=True) + EPS)

    heads = range(n_v)
    q_cols = [column(l2(y[:, j * HEAD_DIM:(j + 1) * HEAD_DIM]) * (HEAD_DIM ** -0.5)) for j in range(n_qk)]
    k_cols = [column(l2(y[:, (n_qk + j) * HEAD_DIM:(n_qk + j + 1) * HEAD_DIM])) for j in range(n_qk)]
    e_gs = [jnp.exp(g[:, h:h + 1]) for h in heads]
    k_ss = [jnp.sum(s_ref[h] * k_cols[h // rep], axis=0, keepdims=True) for h in heads]
    for h in heads:
        lo = 2 * n_qk * HEAD_DIM + h * HEAD_DIM
        v_new = beta[:, h:h + 1] * (y[:, lo:lo + HEAD_DIM] - e_gs[h] * k_ss[h])
        sn_ref[h] = e_gs[h] * s_ref[h] + k_cols[h // rep] * v_new
    outs = [jnp.sum(sn_ref[h] * q_cols[h // rep], axis=0, keepdims=True) for h in heads]
    outs = [_rms_rows(o, onorm_ref[...]) for o in outs]
    for h in heads:
        lo = h * HEAD_DIM
        z = qkvz_ref[:, conv_dim + lo:conv_dim + lo + HEAD_DIM]
        o_ref[:, lo:lo + HEAD_DIM] = (outs[h] * _silu(z)).astype(o_ref.dtype)


def gdn_step(qkvz, gates, conv_state, state, conv_w, a_log, dt_bias, o_norm, n_qk, n_v):
    b = qkvz.shape[0]
    conv_dim = (2 * n_qk + n_v) * HEAD_DIM
    v_dim = n_v * HEAD_DIM
    pad = lambda p: jnp.zeros((1, LANES), F32).at[0, :n_v].set(p.astype(F32))
    nb = GDN_STEP_SEQS if b % GDN_STEP_SEQS == 0 else 1
    row = lambda n: pl.BlockSpec((nb, 1, n), lambda bi: (bi, 0, 0))
    const = lambda shape: pl.BlockSpec(shape, lambda bi: (0,) * len(shape))
    st = pl.BlockSpec((nb, n_v, HEAD_DIM, HEAD_DIM), lambda bi: (bi, 0, 0, 0))
    cv = pl.BlockSpec((nb, CONV_TAPS - 1, conv_dim), lambda bi: (bi, 0, 0))
    return pl.pallas_call(
        functools.partial(_gdn_step_kernel, n_qk=n_qk, n_v=n_v),
        grid=(b // nb,),
        in_specs=[row(conv_dim + v_dim), row(2 * LANES), cv, st,
                  const((CONV_TAPS, conv_dim)), const((1, LANES)), const((1, LANES)), const((1, HEAD_DIM))],
        out_specs=[row(v_dim), cv, st],
        out_shape=[jax.ShapeDtypeStruct((b, 1, v_dim), BF16),
                   jax.ShapeDtypeStruct(conv_state.shape, F32),
                   jax.ShapeDtypeStruct(state.shape, F32)],
        compiler_params=_params(1),
        name="gdn_step",
    )(qkvz.reshape(b, 1, -1), gates.reshape(b, 1, -1), conv_state, state,
      conv_w, pad(a_log), pad(dt_bias), o_norm.reshape(1, HEAD_DIM))


def gdn_layer_sample(x, gain, weights, conv_w, a_log, dt_bias, o_norm, conv_state, state):
    w_main, w_gate, _ = weights
    b = x.shape[0]
    qkvz, gates = gdn_inproj(x, gain, w_main, w_gate, F32, b)
    o, conv_new, state_new = gdn_step(qkvz, gates, conv_state, state, conv_w, a_log, dt_bias, o_norm,
                                      GDN_QK_HEADS, GDN_V_HEADS)
    return o.reshape(b, -1), conv_new, state_new


DSW_HEADS = 8
DSW_WIDTH = DSW_HEADS * HEAD_DIM
DSW_TILE = DSW_BLOCK * max(d for _, d in DSW_GROUPS)
DSW_BLOCKS_PER_ITER = 8


def _dsw_attn_kernel(*refs, n_shift):
    n_groups = len(DSW_GROUPS)
    ins = [refs[5 * g:5 * g + 5] for g in range(n_groups)]
    refs = refs[5 * n_groups:]
    shift_ins, refs = refs[:2 * n_shift], refs[2 * n_shift:]
    o_ref, shift_outs, scratch = refs[0], refs[1:1 + n_shift], refs[1 + n_shift:]
    kf_refs, vf_refs = scratch[:n_groups], scratch[n_groups:2 * n_groups]
    og_ref, lg_ref = scratch[2 * n_groups:2 * n_groups + 2]
    ring_scr = scratch[2 * n_groups + 2:]
    step = (pl.program_id(0) * pl.num_programs(1) + pl.program_id(1)) * pl.num_programs(2) + pl.program_id(2)
    n_steps = pl.num_programs(0) * pl.num_programs(1) * pl.num_programs(2)
    rings = [_ShiftRing(step, n_steps, shift_ins[2 * i], shift_outs[i], *ring_scr[4 * i:4 * i + 4],
                        new_ref=shift_ins[2 * i + 1]) for i in range(n_shift)]
    for ring in rings:
        ring.begin()
    first_tile = pl.program_id(2) == 0
    blk = DSW_BLOCK
    qi = lax.broadcasted_iota(jnp.int32, (blk, 2 * blk), 0)
    ki = lax.broadcasted_iota(jnp.int32, (blk, 2 * blk), 1)
    band = (ki >= qi) & (ki <= qi + blk)
    scale = HEAD_DIM ** -0.5

    for g, (_, dil) in enumerate(DSW_GROUPS):
        for ring in rings:
            ring.point(g)
        q_ref, kc_ref, vc_ref, kp_ref, vp_ref = ins[g]
        kf_ref, vf_ref = kf_refs[g], vf_refs[g]
        per_res = q_ref.shape[1] // blk
        kf_ref[:, :blk, :] = kp_ref[...]
        kf_ref[:, blk:, :] = kc_ref[...]
        vf_ref[:, :blk, :] = vp_ref[...]
        vf_ref[:, blk:, :] = vc_ref[...]

        def blocks_body(it, carry, *, g=g, dil=dil, per_res=per_res, q_ref=q_ref, kf_ref=kf_ref, vf_ref=vf_ref):
            ids = [it * DSW_BLOCKS_PER_ITER + i for i in range(DSW_BLOCKS_PER_ITER)]
            rs = [bi // per_res for bi in ids]
            ms = [bi % per_res for bi in ids]
            row0s = [pl.multiple_of(m * blk, blk) for m in ms]
            ss = [lax.dot_general(q_ref[r, pl.ds(row0, blk), :], kf_ref[r, pl.ds(row0, 2 * blk), :],
                                  (((1,), (1,)), ((), ())), preferred_element_type=F32) * scale
                  for r, row0 in zip(rs, row0s)]
            ss = [jnp.where(band & ((ki >= blk) | (m > 0) | jnp.logical_not(first_tile)), s, NEG_INF)
                  for s, m in zip(ss, ms)]
            mxs = [jnp.max(s, axis=-1, keepdims=True) for s in ss]
            ps = [jnp.exp(s - mx) for s, mx in zip(ss, mxs)]
            dens = [jnp.sum(p, axis=-1, keepdims=True) for p in ps]
            os_ = [jnp.dot(p.astype(BF16), vf_ref[r, pl.ds(row0, 2 * blk), :], preferred_element_type=F32)
                   for p, r, row0 in zip(ps, rs, row0s)]
            for o, den, mx, r, row0 in zip(os_, dens, mxs, rs, row0s):
                o = o * (1.0 / den)
                lse = jnp.broadcast_to(mx + jnp.log(den), (blk, LANES))
                start = row0 * dil + r
                rows = pl.ds(start, blk) if dil == 1 else pl.ds(start, blk, stride=dil)
                og_ref[g, rows, :] = o
                lg_ref[g, rows, :] = lse
            return carry

        lax.fori_loop(0, dil * per_res // DSW_BLOCKS_PER_ITER, blocks_body, 0)

    for ring in rings:
        ring.point(n_groups)

    top = lg_ref[0]
    for g in range(1, n_groups):
        top = jnp.maximum(top, lg_ref[g])
    num = jnp.zeros_like(top)
    den = jnp.zeros_like(top)
    for g in range(n_groups):
        w = jnp.exp(lg_ref[g] - top)
        num = num + w * og_ref[g]
        den = den + w
    o_ref[...] = (num * (1.0 / den)).astype(o_ref.dtype)
    for ring in rings:
        ring.end()


def dsw_attn(projs, shifts=()):
    assert RING_POINTS == len(DSW_GROUPS) + 1
    b = projs[0].shape[0]
    t = projs[0].shape[1] * projs[0].shape[2]
    tile = DSW_TILE
    heads = DSW_HEADS
    in_specs, args, kv_scratch = [], [], []
    for (_, dil), p in zip(DSW_GROUPS, projs):
        rows = tile // dil
        per_res = rows // DSW_BLOCK
        cur = lambda col: pl.BlockSpec((None, dil, rows, HEAD_DIM),
                                       functools.partial(lambda bi, h, n, col: (bi, 0, n, col * heads + h), col=col))
        prev = lambda col: pl.BlockSpec(
            (None, dil, DSW_BLOCK, HEAD_DIM),
            functools.partial(lambda bi, h, n, col, per_res: (bi, 0, jnp.maximum(n * per_res - 1, 0), col * heads + h),
                              col=col, per_res=per_res))
        in_specs += [cur(0), cur(1), cur(2), prev(1), prev(2)]
        args += [p] * 5
        kv_scratch.append(pltpu.VMEM((dil, DSW_BLOCK + rows, HEAD_DIM), BF16))
    n_groups = len(DSW_GROUPS)
    shift_out_specs, shift_out_shape, ring_scratch = [], [], []
    for cache, new in shifts:
        n_l, n_b, rows = cache.shape[:3]
        assert n_l * n_b == b * heads * (t // tile) and rows % RING_POINTS == 0
        in_specs += [pl.BlockSpec(memory_space=pl.ANY), _resident(new.shape)]
        args += [cache, new]
        shift_out_specs.append(pl.BlockSpec(memory_space=pl.ANY))
        shift_out_shape.append(jax.ShapeDtypeStruct(cache.shape, cache.dtype))
        ring_scratch += [pltpu.VMEM((RING_SLOTS, rows // RING_POINTS) + cache.shape[3:], cache.dtype),
                         pltpu.SemaphoreType.DMA((RING_SLOTS,)), pltpu.SemaphoreType.DMA((RING_SLOTS,)),
                         pltpu.SemaphoreType.DMA(())]
    outs = pl.pallas_call(
        functools.partial(_dsw_attn_kernel, n_shift=len(shifts)),
        grid=(b, heads, t // tile),
        in_specs=in_specs,
        out_specs=[pl.BlockSpec((None, tile, HEAD_DIM), lambda bi, h, n: (bi, n, h))] + shift_out_specs,
        out_shape=[jax.ShapeDtypeStruct((b, t, heads * HEAD_DIM), BF16)] + shift_out_shape,
        scratch_shapes=kv_scratch + kv_scratch + [pltpu.VMEM((n_groups, tile, HEAD_DIM), F32),
                                                  pltpu.VMEM((n_groups, tile, LANES), F32)] + ring_scratch,
        compiler_params=_params(3),
        name="dsw_attn",
    )(*args)
    return outs[0], list(outs[1:])


def prep_dsw_weights(w_in, w_out):
    d = w_in.shape[0]
    w = w_in.reshape(d, len(DSW_GROUPS), 3, DSW_WIDTH).transpose(1, 2, 0, 3).astype(BF16)
    return [w[g] for g in range(len(DSW_GROUPS))], w_out.astype(BF16)


def dsw_layer_prompt(x, gain, weights, q_norm, k_norm, shifts=()):
    w_groups, _ = weights
    b, t, d = x.shape
    projs = [dsw_inproj(x, gain, jnp.stack([q_norm[g], k_norm[g]]), w_groups[g], dil,
                        max(DSW_INPROJ_ROWS, DSW_INPROJ_MIN_SUB * dil))
             for g, (_, dil) in enumerate(DSW_GROUPS)]
    o, shifted = dsw_attn(projs, shifts)
    return o.reshape(b * t, -1), projs, shifted


CACHE_RESIDUES = 4


def _dsw_cache_kernel(*refs):
    *p_refs, o_ref, slab_scr = refs
    layer = pl.program_id(0)
    rows, n_res = o_ref.shape[:2]
    for li in range(len(p_refs) // 2):
        @pl.when(layer == li)
        def _(k_ref=p_refs[2 * li], v_ref=p_refs[2 * li + 1]):
            for rr in range(n_res):
                for kv, p_ref in enumerate((k_ref, v_ref)):
                    slab = slab_scr.at[2 * rr + kv]
                    slab[...] = p_ref[rr].astype(o_ref.dtype).reshape(rows, DSW_HEADS, HEAD_DIM)
                    o_ref[:, rr, kv] = slab[...]


def dsw_prompt_caches(projs_layers):
    n_layers = len(projs_layers)
    outs = []
    for g, (window, dil) in enumerate(DSW_GROUPS):
        ps = [pl_[g] for pl_ in projs_layers]
        b, _, sub, _ = ps[0].shape
        last = sub // DSW_BLOCK - 1
        n_res = min(dil, CACHE_RESIDUES)

        def in_map(l, bi, rb, *, li, col):
            before, after = l < li, l > li
            pick = lambda lo, x, hi: jnp.where(before, lo, jnp.where(after, hi, x))
            return (pick(0, bi, b - 1), pick(0, rb, dil // n_res - 1), last, col)

        out = pl.pallas_call(
            _dsw_cache_kernel,
            grid=(n_layers, b, dil // n_res),
            in_specs=[pl.BlockSpec((None, n_res, DSW_BLOCK, DSW_WIDTH), functools.partial(in_map, li=li, col=col))
                      for li in range(n_layers) for col in (1, 2)],
            out_specs=pl.BlockSpec((None, None, DSW_BLOCK, n_res, 2, DSW_HEADS, HEAD_DIM),
                                   lambda l, bi, rb: (l, bi, 0, rb, 0, 0, 0)),
            out_shape=jax.ShapeDtypeStruct((n_layers, b, DSW_BLOCK, dil, 2, DSW_HEADS, HEAD_DIM), F32),
            scratch_shapes=[pltpu.VMEM((2 * n_res, DSW_BLOCK, DSW_HEADS, HEAD_DIM), F32)],
            compiler_params=_params(3),
            name="dsw_prompt_cache",
        )(*[p for p in ps for _ in (1, 2)])
        outs.append(out.reshape(n_layers, b, window, 2, DSW_HEADS, HEAD_DIM))
    return outs


def _dsw_decode_kernel(p_ref, *refs):
    n_groups = len(DSW_GROUPS)
    c_refs = refs[:n_groups]
    o_ref, new_ref = refs[n_groups:]
    scale = HEAD_DIM ** -0.5
    outs, lses = [], []
    for g in range(n_groups):
        q = p_ref[g, 0].astype(F32)
        k_new = p_ref[g, 1].astype(F32)
        v_new = p_ref[g, 2].astype(F32)
        new_ref[g, 0] = k_new
        new_ref[g, 1] = v_new
        s = jnp.sum(c_refs[g][:, 0] * q[None], axis=-1, keepdims=True) * scale
        s_new = jnp.sum(k_new * q, axis=-1, keepdims=True) * scale
        mx = jnp.maximum(jnp.max(s, axis=0), s_new)
        p = jnp.exp(s - mx[None])
        p_new = jnp.exp(s_new - mx)
        den = jnp.sum(p, axis=0) + p_new
        outs.append((jnp.sum(p * c_refs[g][:, 1], axis=0) + p_new * v_new) * (1.0 / den))
        lses.append(mx + jnp.log(den))
    top = functools.reduce(jnp.maximum, lses)
    ws = [jnp.exp(l - top) for l in lses]
    num = sum(w * o for w, o in zip(ws, outs))
    o_ref[...] = (num * (1.0 / sum(ws))).astype(o_ref.dtype)


def dsw_decode(proj, caches, layer):
    b = proj.shape[0]
    n_groups = len(DSW_GROUPS)
    c_specs, c_args = [], []
    for (window, dil), c in zip(DSW_GROUPS, caches):
        c_args.append(c.reshape(c.shape[0], b, window // dil, dil, 2, DSW_HEADS, HEAD_DIM))
        c_specs.append(pl.BlockSpec((None, None, window // dil, None, 2, DSW_HEADS, HEAD_DIM),
                                    lambda bi: (layer, bi, 0, 0, 0, 0, 0)))
    return pl.pallas_call(
        _dsw_decode_kernel,
        grid=(b,),
        in_specs=[pl.BlockSpec((None, n_groups, 3, DSW_HEADS, HEAD_DIM), lambda bi: (bi, 0, 0, 0, 0))] + c_specs,
        out_specs=[pl.BlockSpec((None, DSW_HEADS, HEAD_DIM), lambda bi: (bi, 0, 0)),
                   pl.BlockSpec((None, n_groups, 2, DSW_HEADS, HEAD_DIM), lambda bi: (bi, 0, 0, 0, 0))],
        out_shape=[jax.ShapeDtypeStruct((b, DSW_HEADS, HEAD_DIM), BF16),
                   jax.ShapeDtypeStruct((b, n_groups, 2, DSW_HEADS, HEAD_DIM), F32)],
        compiler_params=_params(1),
        name="dsw_decode",
    )(proj, *c_args)


def dsw_layer_sample(x, gain, weights, q_norm, k_norm, caches, layer):
    w_groups, _ = weights
    b, d = x.shape
    proj = jnp.stack([dsw_inproj(x.reshape(1, b, d), gain, jnp.stack([q_norm[g], k_norm[g]]), w_groups[g], 1, b)
                      .reshape(b, 3, DSW_HEADS, HEAD_DIM) for g in range(len(DSW_GROUPS))], axis=1)
    o, new_rows = dsw_decode(proj, caches, layer)
    return o.reshape(b, -1), new_rows


def _last_row_kernel(shifted_ref, new_ref, o_ref):
    del shifted_ref
    o_ref[0] = new_ref[...]


def write_last_rows(shifted, new_rows_layers, g):
    new = jnp.stack(new_rows_layers)
    n_layers, b, window = shifted.shape[:3]
    tile = shifted.shape[3:]
    return pl.pallas_call(
        _last_row_kernel,
        grid=(n_layers, b),
        in_specs=[pl.BlockSpec(memory_space=pl.ANY),
                  pl.BlockSpec((None, None, None) + tile, lambda l, bi: (l, bi, g, 0, 0, 0))],
        out_specs=pl.BlockSpec((None, None, 1) + tile, lambda l, bi: (l, bi, window - 1, 0, 0, 0)),
        out_shape=jax.ShapeDtypeStruct(shifted.shape, shifted.dtype),
        input_output_aliases={0: 0},
        compiler_params=_params(2),
        name="cache_last_row",
    )(shifted, new)


def kernel(x_prompt, x_sample, state_gdn, state_conv, cache_kv_w128, cache_kv_w512, cache_kv_w2048,
           norm_mix, norm_mlp, gdn_w_in, gdn_conv_w, gdn_a_log, gdn_dt_bias, gdn_o_norm, gdn_w_out,
           dsw_w_in, dsw_q_norm, dsw_k_norm, dsw_w_out, mlp_w_up, mlp_w_down):
    b, t, d = x_prompt.shape
    bs = x_sample.shape[0]
    depth = norm_mix.shape[0]
    caches = [cache_kv_w128, cache_kv_w512, cache_kv_w2048]
    yp = x_prompt.reshape(b * t, d)
    ys = x_sample.reshape(bs, d)
    p_gdn, p_conv, s_gdn, s_conv, p_projs, s_rows = [], [], [], [], [], []
    for i in range(depth):
        j = i // 2
        if i % 2 == 0:
            wts = prep_gdn_weights(gdn_w_in[j], gdn_w_out[j])
            par = (gdn_conv_w[j], gdn_a_log[j], gdn_dt_bias[j], gdn_o_norm[j])
            host = i == 2 * ((depth - 1) // 2)
            op, conv_p, state_p, *shifted = gdn_layer_prompt(yp.reshape(b, t, d), norm_mix[i], wts, *par,
                                                             caches[-1] if host else None)
            if host:
                shifted_big = shifted[0]
            os_, conv_s, state_s = gdn_layer_sample(ys, norm_mix[i], wts, *par, state_conv[j], state_gdn[j])
            p_gdn.append(state_p)
            p_conv.append(conv_p)
            s_gdn.append(state_s)
            s_conv.append(conv_s)
        else:
            wts = prep_dsw_weights(dsw_w_in[j], dsw_w_out[j])
            os_, rows = dsw_layer_sample(ys, norm_mix[i], wts, dsw_q_norm[j], dsw_k_norm[j], caches, j)
            s_rows.append(rows)
            shifts = ()
            if len(s_rows) == caches[0].shape[0]:
                new = jnp.stack(s_rows)
                shifts = tuple((caches[g], new[:, :, g]) for g in range(len(caches) - 1))
            op, projs, shifted = dsw_layer_prompt(yp.reshape(b, t, d), norm_mix[i], wts, dsw_q_norm[j], dsw_k_norm[j],
                                                  shifts)
            if shifts:
                shifted_small = shifted
            p_projs.append(projs)
        w_up, w_down = mlp_w_up[i].astype(BF16), mlp_w_down[i].astype(BF16)
        yp = mixer_mlp(yp, op, wts[-1], norm_mlp[i], w_up, w_down, MLP_ROWS, MLP_COLS)
        ys = mixer_mlp(ys, os_, wts[-1], norm_mlp[i], w_up, w_down, bs, MLP_COLS)
    yp = yp.reshape(b, t, d)
    p_kv = dsw_prompt_caches(p_projs)
    s_kv = shifted_small + [write_last_rows(shifted_big, s_rows, len(caches) - 1)]
    return (yp, ys.reshape(x_sample.shape),
            jnp.stack(p_gdn), jnp.stack(p_conv), p_kv[0], p_kv[1], p_kv[2],
            jnp.stack(s_gdn), jnp.stack(s_conv), s_kv[0], s_kv[1], s_kv[2])
```

```python
import functools

import jax
import jax.numpy as jnp
from jax import lax
from jax.experimental import pallas as pl
from jax.experimental.pallas import tpu as pltpu

F32 = jnp.float32
BF16 = jnp.bfloat16

EPS = 1e-6
NEG_INF = -1e30
LANES = 128
CONV_TAPS = 4
CHUNK = 64
HEAD_DIM = 128
DSW_BLOCK = 128
DSW_GROUPS = ((128, 1), (512, 4), (2048, 16))
VMEM_LIMIT = 56 * 1024 * 1024


def _params(n_axes, vmem=VMEM_LIMIT):
    return pltpu.CompilerParams(dimension_semantics=("arbitrary",) * n_axes,
                                vmem_limit_bytes=vmem)


def _sigmoid(x):
    return 0.5 + 0.5 * jnp.tanh(0.5 * x)


def _silu(x):
    h = 0.5 * x
    return h + h * jnp.tanh(h)


def _softplus(x):
    return jnp.maximum(x, 0.0) + jnp.log(1.0 + jnp.exp(-jnp.abs(x)))


def _rms_rows(x, gain_row):
    ms = jnp.mean(x * x, axis=-1, keepdims=True)
    return x * lax.rsqrt(ms + EPS) * gain_row


INPROJ_COLS = 512
GDN_INPROJ_ROWS = 512
GDN_BLOCK = 256
MLP_ROWS = 512
MLP_COLS = 1024


def _gdn_inproj_kernel(x_ref, g_ref, w_ref, wg_ref, o_ref, og_ref, xn_ref):
    xn_ref[...] = _rms_rows(x_ref[...], g_ref[...]).astype(BF16)
    for lo in range(0, w_ref.shape[1], INPROJ_COLS):
        o_ref[:, lo:lo + INPROJ_COLS] = jnp.dot(
            xn_ref[...], w_ref[:, lo:lo + INPROJ_COLS], preferred_element_type=F32).astype(o_ref.dtype)
    og_ref[...] = jnp.dot(xn_ref[...], wg_ref[...], preferred_element_type=F32)


def gdn_inproj(x, gain, w_main, w_gate, out_dtype, tm):
    m, k = x.shape
    n, ng = w_main.shape[1], w_gate.shape[1]
    tm = min(tm, m)
    return pl.pallas_call(
        _gdn_inproj_kernel,
        grid=(m // tm,),
        in_specs=[pl.BlockSpec((tm, k), lambda i: (i, 0)),
                  pl.BlockSpec((1, k), lambda i: (0, 0)),
                  _resident((k, n)), _resident((k, ng))],
        out_specs=[pl.BlockSpec((tm, n), lambda i: (i, 0)),
                   pl.BlockSpec((tm, ng), lambda i: (i, 0))],
        out_shape=[jax.ShapeDtypeStruct((m, n), out_dtype), jax.ShapeDtypeStruct((m, ng), F32)],
        scratch_shapes=[pltpu.VMEM((tm, k), BF16)],
        compiler_params=_params(1),
        name="gdn_inproj",
    )(x, gain.reshape(1, k), w_main, w_gate)


def _mixer_mlp_kernel(x_ref, a_ref, wo_ref, g_ref, wu_ref, wd_ref, o_ref, xn_ref, *, tf):
    x1 = x_ref[...] + jnp.dot(a_ref[...], wo_ref[...], preferred_element_type=F32)
    xn_ref[...] = _rms_rows(x1, g_ref[...]).astype(BF16)
    o_ref[...] = x1
    for lo in range(0, wu_ref.shape[1], tf):
        h = jnp.dot(xn_ref[...], wu_ref[:, lo:lo + tf], preferred_element_type=F32)
        h = jnp.square(jnp.maximum(h, 0.0)).astype(BF16)
        o_ref[...] += jnp.dot(h, wd_ref[lo:lo + tf, :], preferred_element_type=F32)


def _resident(shape):
    return pl.BlockSpec(shape, lambda *_: (0,) * len(shape), pipeline_mode=pl.Buffered(1))


def mixer_mlp(x, a, w_out, gain, w_up, w_down, tm, tf):
    m, d = x.shape
    ka = a.shape[1]
    ff = w_up.shape[1]
    tm = min(tm, m)
    return pl.pallas_call(
        functools.partial(_mixer_mlp_kernel, tf=tf),
        grid=(m // tm,),
        in_specs=[pl.BlockSpec((tm, d), lambda i: (i, 0)),
                  pl.BlockSpec((tm, ka), lambda i: (i, 0)),
                  _resident((ka, d)), _resident((1, d)), _resident((d, ff)), _resident((ff, d))],
        out_specs=pl.BlockSpec((tm, d), lambda i: (i, 0)),
        out_shape=jax.ShapeDtypeStruct((m, d), F32),
        scratch_shapes=[pltpu.VMEM((tm, d), BF16)],
        compiler_params=_params(1),
        name="mixer_mlp",
    )(x, a, w_out, gain.reshape(1, d), w_up, w_down)


DEINTERLEAVE_STRIDE = 4
DSW_INPROJ_ROWS = 1024
DSW_INPROJ_MIN_SUB = 64


def _dsw_inproj_kernel(x_ref, g_ref, hn_ref, w_ref, o_ref, xn_ref, head_scr, part_scr, *, dil):
    rows = x_ref.shape[0]
    sub = rows // dil
    width = w_ref.shape[2]
    xn_ref[...] = _rms_rows(x_ref[...], g_ref[...]).astype(BF16)
    for t in range(3):
        acc = jnp.dot(xn_ref[...], w_ref[t], preferred_element_type=F32)
        for h in range(width // HEAD_DIM):
            a = acc[:, h * HEAD_DIM:(h + 1) * HEAD_DIM]
            if t < 2:
                a = _rms_rows(a, hn_ref[t:t + 1, :])
            cols = slice(t * width + h * HEAD_DIM, t * width + (h + 1) * HEAD_DIM)
            if dil == 1:
                o_ref[0, :, cols] = a.astype(o_ref.dtype)
            elif dil <= DEINTERLEAVE_STRIDE:
                head_scr[h] = a
                for r in range(dil):
                    o_ref[r, :, cols] = head_scr[h, pl.ds(r, sub, stride=dil), :].astype(o_ref.dtype)
            else:
                st = DEINTERLEAVE_STRIDE
                part = rows // st
                head_scr[h] = a
                for q in range(st):
                    part_scr[h, q * part:(q + 1) * part, :] = head_scr[h, pl.ds(q, part, stride=st), :]
                for q in range(st):
                    for s in range(dil // st):
                        o_ref[st * s + q, :, cols] = part_scr[
                            h, pl.ds(q * part + s, sub, stride=dil // st), :].astype(o_ref.dtype)


def dsw_inproj(x, gain, head_gains, w3, dil, tile_rows):
    b, t, d = x.shape
    width = w3.shape[2]
    rows = min(tile_rows, t)
    return pl.pallas_call(
        functools.partial(_dsw_inproj_kernel, dil=dil),
        grid=(b, t // rows),
        in_specs=[pl.BlockSpec((None, rows, d), lambda bi, i: (bi, i, 0)),
                  pl.BlockSpec((1, d), lambda bi, i: (0, 0)),
                  pl.BlockSpec((2, HEAD_DIM), lambda bi, i: (0, 0)),
                  _resident((3, d, width))],
        out_specs=pl.BlockSpec((None, dil, rows // dil, 3 * width), lambda bi, i: (bi, 0, i, 0)),
        out_shape=jax.ShapeDtypeStruct((b, dil, t // dil, 3 * width), BF16),
        scratch_shapes=[pltpu.VMEM((rows, d), BF16),
                        pltpu.VMEM((width // HEAD_DIM, rows, HEAD_DIM), F32),
                        pltpu.VMEM((width // HEAD_DIM, rows, HEAD_DIM), F32)],
        compiler_params=_params(2),
        name="dsw_inproj",
    )(x, gain.reshape(1, d), head_gains, w3)


TRI_BASE = 8


def _pair_block_diag(m):
    n = m.shape[0]
    mb = m.astype(BF16)
    lane = lax.broadcasted_iota(jnp.int32, mb.shape, 1)
    zero = jnp.zeros_like(mb)
    return jnp.concatenate([jnp.where(lane < n, mb, zero), jnp.where(lane >= n, mb, zero)], axis=0)


def _pair_mm(u, w):
    return jnp.dot(u.astype(BF16), _pair_block_diag(w), preferred_element_type=F32)


def _tri_inverse_pairs(a_list, between=lambda: None):
    n = a_list[0].shape[0]
    ii = lax.broadcasted_iota(jnp.int32, (n, 2 * n), 0)
    jj = lax.broadcasted_iota(jnp.int32, (n, 2 * n), 1) % n
    same_block = lambda size: (ii // size) == (jj // size)
    ps = [jnp.where(same_block(TRI_BASE), a, 0.0) for a in a_list]
    xs = [jnp.where(ii == jj, 1.0, 0.0) - p for p in ps]
    for _ in range((TRI_BASE - 1).bit_length() - 1):
        ps = [_pair_mm(p, p) for p in ps]
        between()
        xs = [x + _pair_mm(x, p) for x, p in zip(xs, ps)]
        between()
    size = TRI_BASE
    while size < n:
        off = same_block(2 * size) & jnp.logical_not(same_block(size))
        ys = [_pair_mm(x, jnp.where(off, a, 0.0)) for x, a in zip(xs, a_list)]
        between()
        xs = [x - _pair_mm(y, x) for x, y in zip(xs, ys)]
        between()
        size *= 2
    return xs


RING_POINTS = 4
RING_SLOTS = 4


class _ShiftRing:
    def __init__(self, step, n_steps, cache_ref, out_ref, stage_ref, sem_in, sem_out, sem_row,
                 zero_ref=None, new_ref=None):
        self.step, self.n_steps = step, n_steps
        self.cache, self.out, self.stage, self.zero, self.new = cache_ref, out_ref, stage_ref, zero_ref, new_ref
        self.sem_in, self.sem_out, self.sem_row = sem_in, sem_out, sem_row
        self.batch, self.rows = cache_ref.shape[1], cache_ref.shape[2]
        self.chunk = self.rows // RING_POINTS

    def _n_rows(self, q):
        return self.chunk if q < RING_POINTS - 1 else self.chunk - 1

    def _pair(self, step):
        return step // self.batch, step % self.batch

    def _in(self, step, q, slot):
        n = self._n_rows(q)
        l, b = self._pair(step)
        return pltpu.make_async_copy(self.cache.at[l, b, pl.ds(1 + self.chunk * q, n)],
                                     self.stage.at[slot, pl.ds(0, n)], self.sem_in.at[slot])

    def _out(self, step, q, slot):
        n = self._n_rows(q)
        l, b = self._pair(step)
        return pltpu.make_async_copy(self.stage.at[slot, pl.ds(0, n)],
                                     self.out.at[l, b, pl.ds(self.chunk * q, n)], self.sem_out.at[slot])

    def _row(self):
        l, b = self._pair(self.step)
        src = self.zero if self.new is None else self.new.at[l, b]
        return pltpu.make_async_copy(src, self.out.at[l, b, self.rows - 1], self.sem_row)

    def begin(self):
        step = self.step
        if self.new is None:
            self.zero[...] = jnp.zeros_like(self.zero)

        @pl.when(step == 0)
        def _():
            for n in range(RING_SLOTS - 1):
                self._in(step + n // RING_POINTS, n % RING_POINTS, n).start()

        self._row().start(priority=1)

    def point(self, q):
        step = self.step
        ahead = RING_SLOTS - 1
        n = step * RING_POINTS + q
        slot = n % RING_SLOTS
        self._in(step, q, slot).wait()
        self._out(step, q, slot).start(priority=1)
        prev_step, prev_q = (step, q - 1) if q > 0 else (step - 1, RING_POINTS - 1)
        free_slot = (n + ahead) % RING_SLOTS

        @pl.when(n > 0)
        def _():
            self._out(prev_step, prev_q, free_slot).wait()

        next_step, next_q = step + (q + ahead) // RING_POINTS, (q + ahead) % RING_POINTS

        @pl.when(n + ahead < self.n_steps * RING_POINTS)
        def _():
            self._in(next_step, next_q, free_slot).start()

    def end(self):
        step = self.step
        self._row().wait()

        @pl.when(step == self.n_steps - 1)
        def _():
            last = self.n_steps * RING_POINTS - 1
            self._out(step, RING_POINTS - 1, last % RING_SLOTS).wait()


def _gdn_prompt_kernel(*refs, n_qk, n_v, with_shift):
    if with_shift:
        (qkv_ref, z_ref, gate_ref, cw_ref, alog_ref, dtb_ref, onorm_ref, cache_ref,
         o_ref, s_out_ref, shift_ref,
         xe_ref, qk_ref, v_ref, gate_scr, bb_ref, gcb_ref, tp_ref, pp_ref, s_ref, oscr_ref,
         stage_ref, zero_ref, sem_in, sem_out, sem_row) = refs
    else:
        (qkv_ref, z_ref, gate_ref, cw_ref, alog_ref, dtb_ref, onorm_ref,
         o_ref, s_out_ref,
         xe_ref, qk_ref, v_ref, gate_scr, bb_ref, gcb_ref, tp_ref, pp_ref, s_ref, oscr_ref) = refs
    tb = qkv_ref.shape[0]
    nchunk = tb // CHUNK
    t_idx = pl.program_id(1)
    halo = 8
    ring_point = lambda q: None
    if with_shift:
        ring = _ShiftRing(pl.program_id(0) * pl.num_programs(1) + t_idx, pl.num_programs(0) * pl.num_programs(1),
                          cache_ref, shift_ref, stage_ref, sem_in, sem_out, sem_row, zero_ref=zero_ref)
        ring.begin()
        ring_point = ring.point

    @pl.when(t_idx == 0)
    def _():
        xe_ref[:, halo - (CONV_TAPS - 1):halo, :] = jnp.zeros((xe_ref.shape[0], CONV_TAPS - 1, LANES), F32)
        s_ref[...] = jnp.zeros_like(s_ref)

    ring_point(0)

    gate_scr[0] = _sigmoid(gate_ref[:, :LANES])
    g = -jnp.exp(alog_ref[...]) * _softplus(gate_ref[:, LANES:] + dtb_ref[...])
    pos = lax.broadcasted_iota(jnp.int32, (tb, LANES), 0) % CHUNK
    shift = 1
    while shift < CHUNK:
        g = g + jnp.where(pos >= shift, pltpu.roll(g, shift, axis=0), 0.0)
        shift *= 2
    gate_scr[1] = g
    for h in range(n_v):
        bb_ref[h] = jnp.broadcast_to(gate_scr[0, :, h:h + 1], (tb, LANES))
        gcb_ref[h] = jnp.broadcast_to(gate_scr[1, :, h:h + 1], (tb, LANES))

    def conv_block(cb):
        lo = cb * LANES
        xe_ref[cb, halo:halo + tb, :] = qkv_ref[:, lo:lo + LANES].astype(F32)
        y = cw_ref[0:1, lo:lo + LANES] * xe_ref[cb, halo - 3:halo - 3 + tb, :]
        for j in range(1, CONV_TAPS):
            y = y + cw_ref[j:j + 1, lo:lo + LANES] * xe_ref[cb, halo - 3 + j:halo - 3 + j + tb, :]
        y = _silu(y)
        xe_ref[cb, halo - 3:halo, :] = xe_ref[cb, halo + tb - 3:halo + tb, :]
        if cb < 2 * n_qk:
            y = y * lax.rsqrt(jnp.sum(y * y, axis=-1, keepdims=True) + EPS)
            if cb < n_qk:
                y = y * (HEAD_DIM ** -0.5)
            yb = y.astype(BF16)
            half = 0 if cb < n_qk else CHUNK
            for c in range(nchunk):
                qk_ref[cb % n_qk, c, half:half + CHUNK, :] = yb[c * CHUNK:(c + 1) * CHUNK, :]
        else:
            v_ref[cb - 2 * n_qk] = y

    for cb in range(2 * n_qk):
        conv_block(cb)
    ring_point(1)
    pending = [functools.partial(conv_block, cb) for cb in range(2 * n_qk, 2 * n_qk + n_v)]

    def filler():
        if pending:
            pending.pop(0)()

    ii = lax.broadcasted_iota(jnp.int32, (CHUNK, 2 * CHUNK), 0)
    lane = lax.broadcasted_iota(jnp.int32, (CHUNK, 2 * CHUNK), 1)
    jj = lane % CHUNK
    causal = ii >= jj
    strict = ii > jj
    pairs_per_iter = 4

    for it in range(n_qk // pairs_per_iter):
        probs = [(it * pairs_per_iter + dj, c) for dj in range(pairs_per_iter) for c in range(nchunk)]
        grams, a_list = [], []
        for j, c in probs:
            qk = qk_ref[j, c]
            k2 = jnp.concatenate([qk[CHUNK:], qk[CHUNK:]], axis=0)
            grams.append(lax.dot_general(qk, k2, (((1,), (1,)), ((), ())), preferred_element_type=F32))
        filler()
        for (j, c), gram in zip(probs, grams):
            rows = slice(c * CHUNK, (c + 1) * CHUNK)
            gc = jnp.where(lane < CHUNK, gcb_ref[2 * j, rows, :], gcb_ref[2 * j + 1, rows, :])
            bt = jnp.where(lane < CHUNK, bb_ref[2 * j, rows, :], bb_ref[2 * j + 1, rows, :])
            gc_row = jnp.sum(jnp.where(ii == jj, gc, 0.0), axis=0, keepdims=True)
            decay = jnp.where(causal, jnp.exp(jnp.where(causal, gc - gc_row, 0.0)), 0.0)
            a_list.append(jnp.where(strict, bt * gram[CHUNK:] * decay, 0.0))
            pp_ref[j, c] = (gram[:CHUNK] * decay).astype(BF16)
        for (j, c), t_inv in zip(probs, _tri_inverse_pairs(a_list, filler)):
            tp_ref[j, c] = t_inv.astype(BF16)
    while pending:
        filler()
    ring_point(2)

    def block_diag2(m):
        mb = m.astype(BF16)
        zero = jnp.zeros((CHUNK, HEAD_DIM), BF16)
        return jnp.concatenate([jnp.concatenate([mb[:, :HEAD_DIM], zero], axis=1),
                                jnp.concatenate([zero, mb[:, HEAD_DIM:]], axis=1)], axis=0)

    def finish_chunk(c):
        rows = slice(c * CHUNK, (c + 1) * CHUNK)
        for h in range(n_v):
            lo = h * HEAD_DIM
            o = _rms_rows(oscr_ref[h, rows, :], onorm_ref[...])
            o_ref[rows, lo:lo + HEAD_DIM] = (o * _silu(z_ref[rows, lo:lo + HEAD_DIM].astype(F32))).astype(o_ref.dtype)

    for c in range(nchunk):
        if c > 0:
            finish_chunk(c - 1)
        rows = slice(c * CHUNK, (c + 1) * CHUNK)
        heads = range(n_qk)
        pair = lambda ref, j: jnp.concatenate([ref[2 * j, rows, :], ref[2 * j + 1, rows, :]], axis=1)
        qks = [qk_ref[j, c] for j in heads]
        projs = [jnp.dot(qks[j], jnp.concatenate([s_ref[2 * j], s_ref[2 * j + 1]], axis=1).astype(BF16),
                         preferred_element_type=F32) for j in heads]
        gcs = [pair(gcb_ref, j) for j in heads]
        e_gcs = [jnp.exp(gc) for gc in gcs]
        rhss = [pair(bb_ref, j) * (pair(v_ref, j) - e_gcs[j] * projs[j][CHUNK:]) for j in heads]
        v_news = [jnp.dot(tp_ref[j, c], block_diag2(rhss[j]), preferred_element_type=F32) for j in heads]
        outs = [e_gcs[j] * projs[j][:CHUNK]
                + jnp.dot(pp_ref[j, c], block_diag2(v_news[j]), preferred_element_type=F32) for j in heads]
        g_lasts = [gc[CHUNK - 1:CHUNK, :] for gc in gcs]
        v_decs = [(jnp.exp(g_lasts[j] - gcs[j]) * v_news[j]).astype(BF16) for j in heads]
        d_states = [lax.dot_general(qks[j][CHUNK:], v_decs[j], (((0,), (0,)), ((), ())),
                                    preferred_element_type=F32) for j in heads]
        for j in heads:
            for e in range(2):
                lanes = slice(e * HEAD_DIM, (e + 1) * HEAD_DIM)
                oscr_ref[2 * j + e, rows, :] = outs[j][:, lanes]
                s_ref[2 * j + e] = jnp.exp(g_lasts[j][:, lanes]) * s_ref[2 * j + e] + d_states[j][:, lanes]
    ring_point(3)
    finish_chunk(nchunk - 1)

    @pl.when(t_idx == pl.num_programs(1) - 1)
    def _():
        s_out_ref[...] = s_ref[...]

    if with_shift:
        ring.end()


def gdn_prompt(qkvz, gates, conv_w, a_log, dt_bias, o_norm, n_qk, n_v, tb, shift_cache=None):
    b, t, _ = qkvz.shape
    assert n_v == 2 * n_qk and 2 * CHUNK == LANES
    conv_dim = (2 * n_qk + n_v) * HEAD_DIM
    v_dim = n_v * HEAD_DIM
    tb = min(tb, t)
    nchunk = tb // CHUNK
    pad = lambda p: jnp.zeros((1, LANES), F32).at[0, :n_v].set(p.astype(F32))
    extra_in, extra_in_specs, extra_out_specs, extra_out_shape, extra_scratch = [], [], [], [], []
    if shift_cache is not None:
        n_l, n_b, rows = shift_cache.shape[:3]
        assert n_l * n_b == b * (t // tb) and rows % RING_POINTS == 0
        extra_in = [shift_cache]
        extra_in_specs = [pl.BlockSpec(memory_space=pl.ANY)]
        extra_out_specs = [pl.BlockSpec(memory_space=pl.ANY)]
        extra_out_shape = [jax.ShapeDtypeStruct(shift_cache.shape, shift_cache.dtype)]
        extra_scratch = [pltpu.VMEM((RING_SLOTS, rows // RING_POINTS) + shift_cache.shape[3:], shift_cache.dtype),
                         pltpu.VMEM(shift_cache.shape[3:], shift_cache.dtype),
                         pltpu.SemaphoreType.DMA((RING_SLOTS,)), pltpu.SemaphoreType.DMA((RING_SLOTS,)),
                         pltpu.SemaphoreType.DMA(())]
    return pl.pallas_call(
        functools.partial(_gdn_prompt_kernel, n_qk=n_qk, n_v=n_v, with_shift=shift_cache is not None),
        grid=(b, t // tb),
        in_specs=[pl.BlockSpec((None, tb, conv_dim), lambda bi, ti: (bi, ti, 0)),
                  pl.BlockSpec((None, tb, v_dim), lambda bi, ti: (bi, ti, conv_dim // v_dim)),
                  pl.BlockSpec((None, tb, 2 * LANES), lambda bi, ti: (bi, ti, 0)),
                  pl.BlockSpec((CONV_TAPS, conv_dim), lambda bi, ti: (0, 0)),
                  pl.BlockSpec((1, LANES), lambda bi, ti: (0, 0)),
                  pl.BlockSpec((1, LANES), lambda bi, ti: (0, 0)),
                  pl.BlockSpec((1, HEAD_DIM), lambda bi, ti: (0, 0))] + extra_in_specs,
        out_specs=[pl.BlockSpec((None, tb, v_dim), lambda bi, ti: (bi, ti, 0)),
                   pl.BlockSpec((None, n_v, HEAD_DIM, HEAD_DIM), lambda bi, ti: (bi, 0, 0, 0))] + extra_out_specs,
        out_shape=[jax.ShapeDtypeStruct((b, t, v_dim), BF16),
                   jax.ShapeDtypeStruct((b, n_v, HEAD_DIM, HEAD_DIM), F32)] + extra_out_shape,
        scratch_shapes=[pltpu.VMEM((2 * n_qk + n_v, tb + 8, LANES), F32),
                        pltpu.VMEM((n_qk, nchunk, 2 * CHUNK, HEAD_DIM), BF16),
                        pltpu.VMEM((n_v, tb, HEAD_DIM), F32),
                        pltpu.VMEM((2, tb, LANES), F32),
                        pltpu.VMEM((n_v, tb, LANES), F32),
                        pltpu.VMEM((n_v, tb, LANES), F32),
                        pltpu.VMEM((n_qk, nchunk, CHUNK, 2 * CHUNK), BF16),
                        pltpu.VMEM((n_qk, nchunk, CHUNK, 2 * CHUNK), BF16),
                        pltpu.VMEM((n_v, HEAD_DIM, HEAD_DIM), F32),
                        pltpu.VMEM((n_v, tb, HEAD_DIM), F32)] + extra_scratch,
        compiler_params=_params(2),
        name="gdn_prompt",
    )(qkvz, qkvz, gates, conv_w, pad(a_log), pad(dt_bias), o_norm.reshape(1, HEAD_DIM), *extra_in)


GDN_QK_HEADS = 8
GDN_V_HEADS = 16
GDN_CONV_DIM = (2 * GDN_QK_HEADS + GDN_V_HEADS) * HEAD_DIM
GDN_V_DIM = GDN_V_HEADS * HEAD_DIM


def prep_gdn_weights(w_in, w_out):
    d = w_in.shape[0]
    main = GDN_CONV_DIM + GDN_V_DIM
    w_main = w_in[:, :main].astype(BF16)
    gate_cols = w_in[:, main:].astype(BF16).reshape(d, 2, GDN_V_HEADS)
    w_gate = jnp.pad(gate_cols, ((0, 0), (0, 0), (0, LANES - GDN_V_HEADS))).reshape(d, 2 * LANES)
    return w_main, w_gate, w_out.astype(BF16)


def gdn_layer_prompt(x, gain, weights, conv_w, a_log, dt_bias, o_norm, shift_cache=None):
    w_main, w_gate, _ = weights
    b, t, d = x.shape
    qkvz, gates = gdn_inproj(x.reshape(b * t, d), gain, w_main, w_gate, BF16, GDN_INPROJ_ROWS)
    qkvz = qkvz.reshape(b, t, -1)
    o, state, *shifted = gdn_prompt(qkvz, gates.reshape(b, t, -1), conv_w, a_log, dt_bias, o_norm,
                                    GDN_QK_HEADS, GDN_V_HEADS, GDN_BLOCK, shift_cache)
    conv_tail = qkvz[:, t - (CONV_TAPS - 1):, :GDN_CONV_DIM].astype(F32)
    return (o.reshape(b * t, -1), conv_tail, state, *shifted)


GDN_STEP_SEQS = 1


def _gdn_step_kernel(qkvz_ref, gate_ref, conv_ref, s_ref, cw_ref, alog_ref, dtb_ref, onorm_ref,
                     o_ref, convn_ref, sn_ref, *, n_qk, n_v):
    for i in range(qkvz_ref.shape[0]):
        _gdn_step_one(qkvz_ref.at[i], gate_ref.at[i], conv_ref.at[i], s_ref.at[i], cw_ref, alog_ref, dtb_ref,
                      onorm_ref, o_ref.at[i], convn_ref.at[i], sn_ref.at[i], n_qk=n_qk, n_v=n_v)


def _gdn_step_one(qkvz_ref, gate_ref, conv_ref, s_ref, cw_ref, alog_ref, dtb_ref, onorm_ref,
                  o_ref, convn_ref, sn_ref, *, n_qk, n_v):
    rep = n_v // n_qk
    conv_dim = (2 * n_qk + n_v) * HEAD_DIM
    x = qkvz_ref[:, :conv_dim]
    y = cw_ref[CONV_TAPS - 1:CONV_TAPS, :] * x
    for j in range(CONV_TAPS - 1):
        y = y + cw_ref[j:j + 1, :] * conv_ref[j:j + 1, :]
    y = _silu(y)
    convn_ref[0:CONV_TAPS - 2, :] = conv_ref[1:CONV_TAPS - 1, :]
    convn_ref[CONV_TAPS - 2:CONV_TAPS - 1, :] = x

    beta = _sigmoid(gate_ref[:, :LANES])
    g = -jnp.exp(alog_ref[...]) * _softplus(gate_ref[:, LANES:] + dtb_ref[...])
    eye = (lax.broadcasted_iota(jnp.int32, (HEAD_DIM, HEAD_DIM), 0)
           == lax.broadcasted_iota(jnp.int32, (HEAD_DIM, HEAD_DIM), 1))

    def column(row):
        return jnp.sum(jnp.where(eye, row, 0.0), axis=1, keepdims=True)

    def l2(row):
        return row * lax.rsqrt(jnp.sum(row * row, axis=-1, keepdims=True) + EPS)

    heads = range(n_v)
    q_cols = [column(l2(y[:, j * HEAD_DIM:(j + 1) * HEAD_DIM]) * (HEAD_DIM ** -0.5)) for j in range(n_qk)]
    k_cols = [column(l2(y[:, (n_qk + j) * HEAD_DIM:(n_qk + j + 1) * HEAD_DIM])) for j in range(n_qk)]
    e_gs = [jnp.exp(g[:, h:h + 1]) for h in heads]
    k_ss = [jnp.sum(s_ref[h] * k_cols[h // rep], axis=0, keepdims=True) for h in heads]
    for h in heads:
        lo = 2 * n_qk * HEAD_DIM + h * HEAD_DIM
        v_new = beta[:, h:h + 1] * (y[:, lo:lo + HEAD_DIM] - e_gs[h] * k_ss[h])
        sn_ref[h] = e_gs[h] * s_ref[h] + k_cols[h // rep] * v_new
    outs = [jnp.sum(sn_ref[h] * q_cols[h // rep], axis=0, keepdims=True) for h in heads]
    outs = [_rms_rows(o, onorm_ref[...]) for o in outs]
    for h in heads:
        lo = h * HEAD_DIM
        z = qkvz_ref[:, conv_dim + lo:conv_dim + lo + HEAD_DIM]
        o_ref[:, lo:lo + HEAD_DIM] = (outs[h] * _silu(z)).astype(o_ref.dtype)


def gdn_step(qkvz, gates, conv_state, state, conv_w, a_log, dt_bias, o_norm, n_qk, n_v):
    b = qkvz.shape[0]
    conv_dim = (2 * n_qk + n_v) * HEAD_DIM
    v_dim = n_v * HEAD_DIM
    pad = lambda p: jnp.zeros((1, LANES), F32).at[0, :n_v].set(p.astype(F32))
    nb = GDN_STEP_SEQS if b % GDN_STEP_SEQS == 0 else 1
    row = lambda n: pl.BlockSpec((nb, 1, n), lambda bi: (bi, 0, 0))
    const = lambda shape: pl.BlockSpec(shape, lambda bi: (0,) * len(shape))
    st = pl.BlockSpec((nb, n_v, HEAD_DIM, HEAD_DIM), lambda bi: (bi, 0, 0, 0))
    cv = pl.BlockSpec((nb, CONV_TAPS - 1, conv_dim), lambda bi: (bi, 0, 0))
    return pl.pallas_call(
        functools.partial(_gdn_step_kernel, n_qk=n_qk, n_v=n_v),
        grid=(b // nb,),
        in_specs=[row(conv_dim + v_dim), row(2 * LANES), cv, st,
                  const((CONV_TAPS, conv_dim)), const((1, LANES)), const((1, LANES)), const((1, HEAD_DIM))],
        out_specs=[row(v_dim), cv, st],
        out_shape=[jax.ShapeDtypeStruct((b, 1, v_dim), BF16),
                   jax.ShapeDtypeStruct(conv_state.shape, F32),
                   jax.ShapeDtypeStruct(state.shape, F32)],
        compiler_params=_params(1),
        name="gdn_step",
    )(qkvz.reshape(b, 1, -1), gates.reshape(b, 1, -1), conv_state, state,
      conv_w, pad(a_log), pad(dt_bias), o_norm.reshape(1, HEAD_DIM))


def gdn_layer_sample(x, gain, weights, conv_w, a_log, dt_bias, o_norm, conv_state, state):
    w_main, w_gate, _ = weights
    b = x.shape[0]
    qkvz, gates = gdn_inproj(x, gain, w_main, w_gate, F32, b)
    o, conv_new, state_new = gdn_step(qkvz, gates, conv_state, state, conv_w, a_log, dt_bias, o_norm,
                                      GDN_QK_HEADS, GDN_V_HEADS)
    return o.reshape(b, -1), conv_new, state_new


DSW_HEADS = 8
DSW_WIDTH = DSW_HEADS * HEAD_DIM
DSW_TILE = DSW_BLOCK * max(d for _, d in DSW_GROUPS)
DSW_BLOCKS_PER_ITER = 16


def _dsw_attn_kernel(*refs, n_shift):
    n_groups = len(DSW_GROUPS)
    ins = [refs[5 * g:5 * g + 5] for g in range(n_groups)]
    refs = refs[5 * n_groups:]
    shift_ins, refs = refs[:2 * n_shift], refs[2 * n_shift:]
    o_ref, shift_outs, scratch = refs[0], refs[1:1 + n_shift], refs[1 + n_shift:]
    kf_refs, vf_refs = scratch[:n_groups], scratch[n_groups:2 * n_groups]
    og_ref, lg_ref = scratch[2 * n_groups:2 * n_groups + 2]
    ring_scr = scratch[2 * n_groups + 2:]
    step = (pl.program_id(0) * pl.num_programs(1) + pl.program_id(1)) * pl.num_programs(2) + pl.program_id(2)
    n_steps = pl.num_programs(0) * pl.num_programs(1) * pl.num_programs(2)
    rings = [_ShiftRing(step, n_steps, shift_ins[2 * i], shift_outs[i], *ring_scr[4 * i:4 * i + 4],
                        new_ref=shift_ins[2 * i + 1]) for i in range(n_shift)]
    for ring in rings:
        ring.begin()
    first_tile = pl.program_id(2) == 0
    blk = DSW_BLOCK
    qi = lax.broadcasted_iota(jnp.int32, (blk, 2 * blk), 0)
    ki = lax.broadcasted_iota(jnp.int32, (blk, 2 * blk), 1)
    band = (ki >= qi) & (ki <= qi + blk)
    scale = HEAD_DIM ** -0.5

    for g, (_, dil) in enumerate(DSW_GROUPS):
        for ring in rings:
            ring.point(g)
        q_ref, kc_ref, vc_ref, kp_ref, vp_ref = ins[g]
        kf_ref, vf_ref = kf_refs[g], vf_refs[g]
        per_res = q_ref.shape[1] // blk
        kf_ref[:, :blk, :] = kp_ref[...]
        kf_ref[:, blk:, :] = kc_ref[...]
        vf_ref[:, :blk, :] = vp_ref[...]
        vf_ref[:, blk:, :] = vc_ref[...]

        def blocks_body(it, carry, *, g=g, dil=dil, per_res=per_res, q_ref=q_ref, kf_ref=kf_ref, vf_ref=vf_ref):
            ids = [it * DSW_BLOCKS_PER_ITER + i for i in range(DSW_BLOCKS_PER_ITER)]
            rs = [bi // per_res for bi in ids]
            ms = [bi % per_res for bi in ids]
            row0s = [pl.multiple_of(m * blk, blk) for m in ms]
            ss = [lax.dot_general(q_ref[r, pl.ds(row0, blk), :], kf_ref[r, pl.ds(row0, 2 * blk), :],
                                  (((1,), (1,)), ((), ())), preferred_element_type=F32) * scale
                  for r, row0 in zip(rs, row0s)]
            ss = [jnp.where(band & ((ki >= blk) | (m > 0) | jnp.logical_not(first_tile)), s, NEG_INF)
                  for s, m in zip(ss, ms)]
            mxs = [jnp.max(s, axis=-1, keepdims=True) for s in ss]
            ps = [jnp.exp(s - mx) for s, mx in zip(ss, mxs)]
            dens = [jnp.sum(p, axis=-1, keepdims=True) for p in ps]
            os_ = [jnp.dot(p.astype(BF16), vf_ref[r, pl.ds(row0, 2 * blk), :], preferred_element_type=F32)
                   for p, r, row0 in zip(ps, rs, row0s)]
            for o, den, mx, r, row0 in zip(os_, dens, mxs, rs, row0s):
                o = o * (1.0 / den)
                lse = jnp.broadcast_to(mx + jnp.log(den), (blk, LANES))
                start = row0 * dil + r
                rows = pl.ds(start, blk) if dil == 1 else pl.ds(start, blk, stride=dil)
                og_ref[g, rows, :] = o
                lg_ref[g, rows, :] = lse
            return carry

        lax.fori_loop(0, dil * per_res // DSW_BLOCKS_PER_ITER, blocks_body, 0)

    for ring in rings:
        ring.point(n_groups)

    top = lg_ref[0]
    for g in range(1, n_groups):
        top = jnp.maximum(top, lg_ref[g])
    num = jnp.zeros_like(top)
    den = jnp.zeros_like(top)
    for g in range(n_groups):
        w = jnp.exp(lg_ref[g] - top)
        num = num + w * og_ref[g]
        den = den + w
    o_ref[...] = (num * (1.0 / den)).astype(o_ref.dtype)
    for ring in rings:
        ring.end()


def dsw_attn(projs, shifts=()):
    assert RING_POINTS == len(DSW_GROUPS) + 1
    b = projs[0].shape[0]
    t = projs[0].shape[1] * projs[0].shape[2]
    tile = DSW_TILE
    heads = DSW_HEADS
    in_specs, args, kv_scratch = [], [], []
    for (_, dil), p in zip(DSW_GROUPS, projs):
        rows = tile // dil
        per_res = rows // DSW_BLOCK
        cur = lambda col: pl.BlockSpec((None, dil, rows, HEAD_DIM),
                                       functools.partial(lambda bi, h, n, col: (bi, 0, n, col * heads + h), col=col))
        prev = lambda col: pl.BlockSpec(
            (None, dil, DSW_BLOCK, HEAD_DIM),
            functools.partial(lambda bi, h, n, col, per_res: (bi, 0, jnp.maximum(n * per_res - 1, 0), col * heads + h),
                              col=col, per_res=per_res))
        in_specs += [cur(0), cur(1), cur(2), prev(1), prev(2)]
        args += [p] * 5
        kv_scratch.append(pltpu.VMEM((dil, DSW_BLOCK + rows, HEAD_DIM), BF16))
    n_groups = len(DSW_GROUPS)
    shift_out_specs, shift_out_shape, ring_scratch = [], [], []
    for cache, new in shifts:
        n_l, n_b, rows = cache.shape[:3]
        assert n_l * n_b == b * heads * (t // tile) and rows % RING_POINTS == 0
        in_specs += [pl.BlockSpec(memory_space=pl.ANY), _resident(new.shape)]
        args += [cache, new]
        shift_out_specs.append(pl.BlockSpec(memory_space=pl.ANY))
        shift_out_shape.append(jax.ShapeDtypeStruct(cache.shape, cache.dtype))
        ring_scratch += [pltpu.VMEM((RING_SLOTS, rows // RING_POINTS) + cache.shape[3:], cache.dtype),
                         pltpu.SemaphoreType.DMA((RING_SLOTS,)), pltpu.SemaphoreType.DMA((RING_SLOTS,)),
                         pltpu.SemaphoreType.DMA(())]
    outs = pl.pallas_call(
        functools.partial(_dsw_attn_kernel, n_shift=len(shifts)),
        grid=(b, heads, t // tile),
        in_specs=in_specs,
        out_specs=[pl.BlockSpec((None, tile, HEAD_DIM), lambda bi, h, n: (bi, n, h))] + shift_out_specs,
        out_shape=[jax.ShapeDtypeStruct((b, t, heads * HEAD_DIM), BF16)] + shift_out_shape,
        scratch_shapes=kv_scratch + kv_scratch + [pltpu.VMEM((n_groups, tile, HEAD_DIM), F32),
                                                  pltpu.VMEM((n_groups, tile, LANES), F32)] + ring_scratch,
        compiler_params=_params(3),
        name="dsw_attn",
    )(*args)
    return outs[0], list(outs[1:])


def prep_dsw_weights(w_in, w_out):
    d = w_in.shape[0]
    w = w_in.reshape(d, len(DSW_GROUPS), 3, DSW_WIDTH).transpose(1, 2, 0, 3).astype(BF16)
    return [w[g] for g in range(len(DSW_GROUPS))], w_out.astype(BF16)


def dsw_layer_prompt(x, gain, weights, q_norm, k_norm, shifts=()):
    w_groups, _ = weights
    b, t, d = x.shape
    projs = [dsw_inproj(x, gain, jnp.stack([q_norm[g], k_norm[g]]), w_groups[g], dil,
                        max(DSW_INPROJ_ROWS, DSW_INPROJ_MIN_SUB * dil))
             for g, (_, dil) in enumerate(DSW_GROUPS)]
    o, shifted = dsw_attn(projs, shifts)
    return o.reshape(b * t, -1), projs, shifted


CACHE_RESIDUES = 4


def _dsw_cache_kernel(*refs):
    *p_refs, o_ref, slab_scr = refs
    layer = pl.program_id(0)
    rows, n_res = o_ref.shape[:2]
    for li in range(len(p_refs) // 2):
        @pl.when(layer == li)
        def _(k_ref=p_refs[2 * li], v_ref=p_refs[2 * li + 1]):
            for rr in range(n_res):
                for kv, p_ref in enumerate((k_ref, v_ref)):
                    slab = slab_scr.at[2 * rr + kv]
                    slab[...] = p_ref[rr].astype(o_ref.dtype).reshape(rows, DSW_HEADS, HEAD_DIM)
                    o_ref[:, rr, kv] = slab[...]


def dsw_prompt_caches(projs_layers):
    n_layers = len(projs_layers)
    outs = []
    for g, (window, dil) in enumerate(DSW_GROUPS):
        ps = [pl_[g] for pl_ in projs_layers]
        b, _, sub, _ = ps[0].shape
        last = sub // DSW_BLOCK - 1
        n_res = min(dil, CACHE_RESIDUES)

        def in_map(l, bi, rb, *, li, col):
            before, after = l < li, l > li
            pick = lambda lo, x, hi: jnp.where(before, lo, jnp.where(after, hi, x))
            return (pick(0, bi, b - 1), pick(0, rb, dil // n_res - 1), last, col)

        out = pl.pallas_call(
            _dsw_cache_kernel,
            grid=(n_layers, b, dil // n_res),
            in_specs=[pl.BlockSpec((None, n_res, DSW_BLOCK, DSW_WIDTH), functools.partial(in_map, li=li, col=col))
                      for li in range(n_layers) for col in (1, 2)],
            out_specs=pl.BlockSpec((None, None, DSW_BLOCK, n_res, 2, DSW_HEADS, HEAD_DIM),
                                   lambda l, bi, rb: (l, bi, 0, rb, 0, 0, 0)),
            out_shape=jax.ShapeDtypeStruct((n_layers, b, DSW_BLOCK, dil, 2, DSW_HEADS, HEAD_DIM), F32),
            scratch_shapes=[pltpu.VMEM((2 * n_res, DSW_BLOCK, DSW_HEADS, HEAD_DIM), F32)],
            compiler_params=_params(3),
            name="dsw_prompt_cache",
        )(*[p for p in ps for _ in (1, 2)])
        outs.append(out.reshape(n_layers, b, window, 2, DSW_HEADS, HEAD_DIM))
    return outs


def _dsw_decode_kernel(p_ref, *refs):
    n_groups = len(DSW_GROUPS)
    c_refs = refs[:n_groups]
    o_ref, new_ref = refs[n_groups:]
    scale = HEAD_DIM ** -0.5
    outs, lses = [], []
    for g in range(n_groups):
        q = p_ref[g, 0].astype(F32)
        k_new = p_ref[g, 1].astype(F32)
        v_new = p_ref[g, 2].astype(F32)
        new_ref[g, 0] = k_new
        new_ref[g, 1] = v_new
        s = jnp.sum(c_refs[g][:, 0] * q[None], axis=-1, keepdims=True) * scale
        s_new = jnp.sum(k_new * q, axis=-1, keepdims=True) * scale
        mx = jnp.maximum(jnp.max(s, axis=0), s_new)
        p = jnp.exp(s - mx[None])
        p_new = jnp.exp(s_new - mx)
        den = jnp.sum(p, axis=0) + p_new
        outs.append((jnp.sum(p * c_refs[g][:, 1], axis=0) + p_new * v_new) * (1.0 / den))
        lses.append(mx + jnp.log(den))
    top = functools.reduce(jnp.maximum, lses)
    ws = [jnp.exp(l - top) for l in lses]
    num = sum(w * o for w, o in zip(ws, outs))
    o_ref[...] = (num * (1.0 / sum(ws))).astype(o_ref.dtype)


def dsw_decode(proj, caches, layer):
    b = proj.shape[0]
    n_groups = len(DSW_GROUPS)
    c_specs, c_args = [], []
    for (window, dil), c in zip(DSW_GROUPS, caches):
        c_args.append(c.reshape(c.shape[0], b, window // dil, dil, 2, DSW_HEADS, HEAD_DIM))
        c_specs.append(pl.BlockSpec((None, None, window // dil, None, 2, DSW_HEADS, HEAD_DIM),
                                    lambda bi: (layer, bi, 0, 0, 0, 0, 0)))
    return pl.pallas_call(
        _dsw_decode_kernel,
        grid=(b,),
        in_specs=[pl.BlockSpec((None, n_groups, 3, DSW_HEADS, HEAD_DIM), lambda bi: (bi, 0, 0, 0, 0))] + c_specs,
        out_specs=[pl.BlockSpec((None, DSW_HEADS, HEAD_DIM), lambda bi: (bi, 0, 0)),
                   pl.BlockSpec((None, n_groups, 2, DSW_HEADS, HEAD_DIM), lambda bi: (bi, 0, 0, 0, 0))],
        out_shape=[jax.ShapeDtypeStruct((b, DSW_HEADS, HEAD_DIM), BF16),
                   jax.ShapeDtypeStruct((b, n_groups, 2, DSW_HEADS, HEAD_DIM), F32)],
        compiler_params=_params(1),
        name="dsw_decode",
    )(proj, *c_args)


def dsw_layer_sample(x, gain, weights, q_norm, k_norm, caches, layer):
    w_groups, _ = weights
    b, d = x.shape
    proj = jnp.stack([dsw_inproj(x.reshape(1, b, d), gain, jnp.stack([q_norm[g], k_norm[g]]), w_groups[g], 1, b)
                      .reshape(b, 3, DSW_HEADS, HEAD_DIM) for g in range(len(DSW_GROUPS))], axis=1)
    o, new_rows = dsw_decode(proj, caches, layer)
    return o.reshape(b, -1), new_rows


def _last_row_kernel(shifted_ref, new_ref, o_ref):
    del shifted_ref
    o_ref[:, 0] = new_ref[...]


def write_last_rows(shifted, new_rows_layers, g):
    new = jnp.stack(new_rows_layers)
    n_layers, b, window = shifted.shape[:3]
    tile = shifted.shape[3:]
    return pl.pallas_call(
        _last_row_kernel,
        grid=(n_layers,),
        in_specs=[pl.BlockSpec(memory_space=pl.ANY),
                  pl.BlockSpec((None, b, None) + tile, lambda l: (l, 0, g, 0, 0, 0))],
        out_specs=pl.BlockSpec((None, b, 1) + tile, lambda l: (l, 0, window - 1, 0, 0, 0)),
        out_shape=jax.ShapeDtypeStruct(shifted.shape, shifted.dtype),
        input_output_aliases={0: 0},
        compiler_params=_params(1),
        name="cache_last_row",
    )(shifted, new)


def kernel(x_prompt, x_sample, state_gdn, state_conv, cache_kv_w128, cache_kv_w512, cache_kv_w2048,
           norm_mix, norm_mlp, gdn_w_in, gdn_conv_w, gdn_a_log, gdn_dt_bias, gdn_o_norm, gdn_w_out,
           dsw_w_in, dsw_q_norm, dsw_k_norm, dsw_w_out, mlp_w_up, mlp_w_down):
    b, t, d = x_prompt.shape
    bs = x_sample.shape[0]
    depth = norm_mix.shape[0]
    caches = [cache_kv_w128, cache_kv_w512, cache_kv_w2048]
    yp = x_prompt.reshape(b * t, d)
    ys = x_sample.reshape(bs, d)
    p_gdn, p_conv, s_gdn, s_conv, p_projs, s_rows = [], [], [], [], [], []
    for i in range(depth):
        j = i // 2
        if i % 2 == 0:
            wts = prep_gdn_weights(gdn_w_in[j], gdn_w_out[j])
            par = (gdn_conv_w[j], gdn_a_log[j], gdn_dt_bias[j], gdn_o_norm[j])
            host = i == 2 * ((depth - 1) // 2)
            op, conv_p, state_p, *shifted = gdn_layer_prompt(yp.reshape(b, t, d), norm_mix[i], wts, *par,
                                                             caches[-1] if host else None)
            if host:
                shifted_big = shifted[0]
            os_, conv_s, state_s = gdn_layer_sample(ys, norm_mix[i], wts, *par, state_conv[j], state_gdn[j])
            p_gdn.append(state_p)
            p_conv.append(conv_p)
            s_gdn.append(state_s)
            s_conv.append(conv_s)
        else:
            wts = prep_dsw_weights(dsw_w_in[j], dsw_w_out[j])
            os_, rows = dsw_layer_sample(ys, norm_mix[i], wts, dsw_q_norm[j], dsw_k_norm[j], caches, j)
            s_rows.append(rows)
            shifts = ()
            if len(s_rows) == caches[0].shape[0]:
                new = jnp.stack(s_rows)
                shifts = tuple((caches[g], new[:, :, g]) for g in range(len(caches) - 1))
            op, projs, shifted = dsw_layer_prompt(yp.reshape(b, t, d), norm_mix[i], wts, dsw_q_norm[j], dsw_k_norm[j],
                                                  shifts)
            if shifts:
                shifted_small = shifted
            p_projs.append(projs)
        w_up, w_down = mlp_w_up[i].astype(BF16), mlp_w_down[i].astype(BF16)
        yp = mixer_mlp(yp, op, wts[-1], norm_mlp[i], w_up, w_down, MLP_ROWS, MLP_COLS)
        ys = mixer_mlp(ys, os_, wts[-1], norm_mlp[i], w_up, w_down, bs, MLP_COLS)
    yp = yp.reshape(b, t, d)
    p_kv = dsw_prompt_caches(p_projs)
    s_kv = shifted_small + [write_last_rows(shifted_big, s_rows, len(caches) - 1)]
    return (yp, ys.reshape(x_sample.shape),
            jnp.stack(p_gdn), jnp.stack(p_conv), p_kv[0], p_kv[1], p_kv[2],
            jnp.stack(s_gdn), jnp.stack(s_conv), s_kv[0], s_kv[1], s_kv[2])
```
